```python
import math
import jax
import jax.numpy as jnp
from jax import lax
import numpy as np

D_MODEL = 1024
BATCH = 2
SEQ = 8192
DEPTH = 1

NORM_EPS = 1e-6
D_FF = 2816
MACARON_WEIGHT = 0.5

DN_HEADS = 4
DN_HEAD_DIM = 128
DN_WIDTH = DN_HEADS * DN_HEAD_DIM
DN_CONV = 4
DN_CHUNK = 64

MB_HEADS = 8
MB_HEAD_DIM = 64
MB_WIDTH = MB_HEADS * MB_HEAD_DIM
MB_BLOCK = 256
MB_TOPK = 3
MB_QBLOCK = 64
ALIBI_MAX_BIAS = 8.0

PROJ_WIDTH = 4 * DN_WIDTH + 2 * DN_HEADS + 3 * MB_WIDTH + 2 * D_MODEL

kernel_name = "hybrid_deltanet_moba_macaron"


def rms_norm(x, w):
    xf = x.astype(jnp.float32)
    y = xf * lax.rsqrt(jnp.mean(xf * xf, axis=-1, keepdims=True) + NORM_EPS)
    return (y * w.astype(jnp.float32)).astype(x.dtype)


def l2_normalize(x):
    return x * lax.rsqrt(jnp.sum(x * x, axis=-1, keepdims=True) + NORM_EPS)


def swiglu(h, w_gate, w_up, w_down):
    return (jax.nn.silu(h @ w_gate) * (h @ w_up)) @ w_down


def causal_depthwise_conv(x, w):
    k_len, t_len = w.shape[0], x.shape[1]
    xp = jnp.pad(x, ((0, 0), (k_len - 1, 0), (0, 0)))
    y = xp[:, 0:t_len] * w[0]
    for i in range(1, k_len):
        y = y + xp[:, i:i + t_len] * w[i]
    return y


def split_projection(p):
    sizes = (3 * DN_WIDTH, DN_WIDTH, DN_HEADS, DN_HEADS, MB_WIDTH, MB_WIDTH, MB_WIDTH, D_MODEL, D_MODEL)
    parts, start = [], 0
    for s in sizes:
        parts.append(p[..., start:start + s])
        start += s
    return parts


def chunk_gated_delta_rule(q, k, v, g, beta):
    bsz, nh, t_len, dk = q.shape
    dv = v.shape[-1]
    c = DN_CHUNK
    n = t_len // c
    q = q.reshape(bsz, nh, n, c, dk)
    k = k.reshape(bsz, nh, n, c, dk)
    v = v.reshape(bsz, nh, n, c, dv)
    g = jnp.cumsum(g.reshape(bsz, nh, n, c), axis=-1)
    beta = beta.reshape(bsz, nh, n, c)
    incl = jnp.tril(jnp.ones((c, c), dtype=bool))
    strict = jnp.tril(jnp.ones((c, c), dtype=bool), -1)
    decay = jnp.exp(jnp.where(incl, g[..., :, None] - g[..., None, :], -jnp.inf))
    k_beta = k * beta[..., None]
    v_beta = v * beta[..., None]
    eye = jnp.eye(c, dtype=q.dtype)
    a_mat = jnp.where(strict, jnp.einsum('bhnid,bhnjd->bhnij', k_beta, k) * decay, 0.0) + eye
    t_mat = lax.linalg.triangular_solve(a_mat, jnp.broadcast_to(eye, a_mat.shape),
                                        left_side=True, lower=True, unit_diagonal=True)
    u = t_mat @ v_beta
    w = t_mat @ (k_beta * jnp.exp(g)[..., None])
    attn = jnp.where(incl, jnp.einsum('bhnid,bhnjd->bhnij', q, k) * decay, 0.0)
    q_dec = q * jnp.exp(g)[..., None]
    k_dec = k * jnp.exp(g[..., -1:] - g)[..., None]
    chunk_decay = jnp.exp(g[..., -1])

    def step(state, inp):
        q_i, k_i, u_i, w_i, a_i, d_i = inp
        v_new = u_i - w_i @ state
        o_i = q_i @ state + a_i @ v_new
        state = state * d_i[..., None, None] + jnp.einsum('bhcd,bhce->bhde', k_i, v_new)
        return state, o_i

    xs = (jnp.moveaxis(q_dec, 2, 0), jnp.moveaxis(k_dec, 2, 0), jnp.moveaxis(u, 2, 0),
          jnp.moveaxis(w, 2, 0), jnp.moveaxis(attn, 2, 0), jnp.moveaxis(chunk_decay, 2, 0))
    state0 = jnp.zeros((bsz, nh, dk, dv), q.dtype)
    _, o = lax.scan(step, state0, xs)
    return jnp.moveaxis(o, 0, 2).reshape(bsz, nh, t_len, dv)


def gated_deltanet(qkv, z, b, a, conv_w, a_log, dt_bias, norm_w):
    bsz, t_len, _ = qkv.shape
    qkv = jax.nn.silu(causal_depthwise_conv(qkv, conv_w)).astype(jnp.float32)
    q, k, v = jnp.split(qkv, 3, axis=-1)

    def heads(t):
        return t.reshape(bsz, t_len, DN_HEADS, DN_HEAD_DIM).transpose(0, 2, 1, 3)

    q = l2_normalize(heads(q)) * (DN_HEAD_DIM ** -0.5)
    k = l2_normalize(heads(k))
    v = heads(v)
    beta = jax.nn.sigmoid(b.astype(jnp.float32)).transpose(0, 2, 1)
    g = (-jnp.exp(a_log.astype(jnp.float32))
         * jax.nn.softplus(a.astype(jnp.float32) + dt_bias.astype(jnp.float32))).transpose(0, 2, 1)
    o = chunk_gated_delta_rule(q, k, v, g, beta).transpose(0, 2, 1, 3)
    zg = jax.nn.silu(z.astype(jnp.float32).reshape(bsz, t_len, DN_HEADS, DN_HEAD_DIM))
    o = rms_norm(o, norm_w) * zg
    return o.reshape(bsz, t_len, DN_WIDTH).astype(z.dtype)


def moba_attention(q, k, v):
    bsz, t_len, _ = q.shape
    n_blk = -(-t_len // MB_BLOCK)
    t_pad = n_blk * MB_BLOCK
    topk = min(MB_TOPK, n_blk)

    def heads(t):
        t = jnp.pad(t, ((0, 0), (0, t_pad - t_len), (0, 0)))
        return t.reshape(bsz, t_pad, MB_HEADS, MB_HEAD_DIM).transpose(0, 2, 1, 3)

    q, k, v = heads(q), heads(k), heads(v)
    kb = k.reshape(bsz, MB_HEADS, n_blk, MB_BLOCK, MB_HEAD_DIM)
    vb = v.reshape(bsz, MB_HEADS, n_blk, MB_BLOCK, MB_HEAD_DIM)
    kmean = jnp.mean(kb.astype(jnp.float32), axis=3)
    slopes = jnp.exp2(-ALIBI_MAX_BIAS * jnp.arange(1, MB_HEADS + 1, dtype=jnp.float32) / MB_HEADS)
    scale = MB_HEAD_DIM ** -0.5
    bi = jnp.arange(bsz)[:, None, None, None]
    hi = jnp.arange(MB_HEADS)[None, :, None, None]
    blk_off = jnp.arange(MB_BLOCK)

    def one_qblock(ci):
        start = ci * MB_QBLOCK
        own = start // MB_BLOCK
        q_c = lax.dynamic_slice_in_dim(q, start, MB_QBLOCK, axis=2)
        q_pos = start + jnp.arange(MB_QBLOCK)
        gate = jnp.einsum('bhqd,bhnd->bhqn', q_c.astype(jnp.float32), kmean)
        gate = jnp.where(jnp.arange(n_blk) < own, gate, -jnp.inf)
        _, idx = lax.top_k(gate, topk)
        sel_ok = idx < own
        k_sel = kb[bi, hi, idx]
        v_sel = vb[bi, hi, idx]
        k_pos_sel = idx[..., None] * MB_BLOCK + blk_off
        s_sel = (jnp.einsum('bhqd,bhqnkd->bhqnk', q_c, k_sel).astype(jnp.float32) * scale
                 - slopes[:, None, None, None] * (q_pos[:, None, None] - k_pos_sel))
        s_sel = jnp.where(sel_ok[..., None], s_sel, -jnp.inf)
        k_own = lax.dynamic_slice_in_dim(kb, own, 1, axis=2)[:, :, 0]
        v_own = lax.dynamic_slice_in_dim(vb, own, 1, axis=2)[:, :, 0]
        dist = q_pos[:, None] - (own * MB_BLOCK + blk_off)[None, :]
        s_own = (jnp.einsum('bhqd,bhkd->bhqk', q_c, k_own).astype(jnp.float32) * scale
                 - slopes[:, None, None] * dist)
        s_own = jnp.where(dist >= 0, s_own, -jnp.inf)
        logits = jnp.concatenate(
            [s_sel.reshape(bsz, MB_HEADS, MB_QBLOCK, topk * MB_BLOCK), s_own], axis=-1)
        p = jax.nn.softmax(logits, axis=-1).astype(v.dtype)
        p_sel = p[..., :topk * MB_BLOCK].reshape(bsz, MB_HEADS, MB_QBLOCK, topk, MB_BLOCK)
        p_own = p[..., topk * MB_BLOCK:]
        return (jnp.einsum('bhqnk,bhqnkd->bhqd', p_sel, v_sel)
                + jnp.einsum('bhqk,bhkd->bhqd', p_own, v_own))

    out = lax.map(one_qblock, jnp.arange(t_pad // MB_QBLOCK))
    out = out.transpose(1, 2, 0, 3, 4).reshape(bsz, MB_HEADS, t_pad, MB_HEAD_DIM)[:, :, :t_len]
    return out.transpose(0, 2, 1, 3).reshape(bsz, t_len, MB_WIDTH)


def setup_inputs(seed: int = 0) -> dict:
    key = jax.random.key(seed)
    ks = jax.random.split(key, 24)
    f32 = jnp.float32

    def dense(k, fan_in, fan_out):
        return jax.random.normal(k, (DEPTH, fan_in, fan_out), f32) * fan_in ** -0.5

    def gain(k, n):
        return 1.0 + 0.02 * jax.random.normal(k, (DEPTH, n), f32)

    x = jax.random.normal(ks[0], (BATCH, SEQ, D_MODEL), f32)
    dt = jnp.exp(jax.random.uniform(ks[9], (DEPTH, DN_HEADS), f32, math.log(1e-3), math.log(1e-1)))
    dn_dt_bias = dt + jnp.log(-jnp.expm1(-dt))
    dn_a_log = jnp.log(jax.random.uniform(ks[10], (DEPTH, DN_HEADS), f32, 1.0, 16.0))
    return {
        'x': x,
        'ffn1_pre_w': gain(ks[1], D_MODEL),
        'ffn1_w_gate': dense(ks[2], D_MODEL, D_FF),
        'ffn1_w_up': dense(ks[3], D_MODEL, D_FF),
        'ffn1_w_down': dense(ks[4], D_FF, D_MODEL),
        'ffn1_post_w': gain(ks[5], D_MODEL),
        'mix_pre_w': gain(ks[6], D_MODEL),
        'w_in': dense(ks[7], D_MODEL, PROJ_WIDTH),
        'dn_conv_w': jax.random.normal(ks[8], (DEPTH, DN_CONV, 3 * DN_WIDTH), f32) * DN_CONV ** -0.5,
        'dn_a_log': dn_a_log,
        'dn_dt_bias': dn_dt_bias,
        'dn_norm_w': gain(ks[11], DN_HEAD_DIM),
        'w_branch_dn': dense(ks[12], DN_WIDTH, D_MODEL),
        'w_branch_mb': dense(ks[13], MB_WIDTH, D_MODEL),
        'w_out': dense(ks[14], D_MODEL, D_MODEL),
        'mix_post_w': gain(ks[15], D_MODEL),
        'ffn2_pre_w': gain(ks[16], D_MODEL),
        'ffn2_w_gate': dense(ks[17], D_MODEL, D_FF),
        'ffn2_w_up': dense(ks[18], D_MODEL, D_FF),
        'ffn2_w_down': dense(ks[19], D_FF, D_MODEL),
        'ffn2_post_w': gain(ks[20], D_MODEL),
    }


def reference(x, ffn1_pre_w, ffn1_w_gate, ffn1_w_up, ffn1_w_down, ffn1_post_w,
              mix_pre_w, w_in, dn_conv_w, dn_a_log, dn_dt_bias, dn_norm_w,
              w_branch_dn, w_branch_mb, w_out, mix_post_w,
              ffn2_pre_w, ffn2_w_gate, ffn2_w_up, ffn2_w_down, ffn2_post_w):
    for l in range(DEPTH):
        h = swiglu(rms_norm(x, ffn1_pre_w[l]), ffn1_w_gate[l], ffn1_w_up[l], ffn1_w_down[l])
        x = x + MACARON_WEIGHT * rms_norm(h, ffn1_post_w[l])
        h = rms_norm(x, mix_pre_w[l])
        dn_qkv, dn_z, dn_b, dn_a, mb_q, mb_k, mb_v, gate_dn, gate_mb = split_projection(h @ w_in[l])
        y_dn = gated_deltanet(dn_qkv, dn_z, dn_b, dn_a, dn_conv_w[l], dn_a_log[l],
                              dn_dt_bias[l], dn_norm_w[l]) @ w_branch_dn[l]
        y_mb = moba_attention(mb_q, mb_k, mb_v) @ w_branch_mb[l]
        merged = jax.nn.sigmoid(gate_dn) * y_dn + jax.nn.sigmoid(gate_mb) * y_mb
        x = x + rms_norm(merged @ w_out[l], mix_post_w[l])
        h = swiglu(rms_norm(x, ffn2_pre_w[l]), ffn2_w_gate[l], ffn2_w_up[l], ffn2_w_down[l])
        x = x + MACARON_WEIGHT * rms_norm(h, ffn2_post_w[l])
    return x
```

```python
import functools
import math

import jax
import jax.numpy as jnp
from jax import lax
from jax.experimental import pallas as pl
from jax.experimental.pallas import tpu as pltpu

F32 = jnp.float32
BF16 = jnp.bfloat16

NORM_EPS = 1e-6
MACARON_WEIGHT = 0.5

DN_HEADS = 4
DN_HEAD_DIM = 128
DN_WIDTH = DN_HEADS * DN_HEAD_DIM
DN_CONV = 4
DN_CHUNK = 64
DN_TILE = 256

MB_HEADS = 8
MB_HEAD_DIM = 64
MB_WIDTH = MB_HEADS * MB_HEAD_DIM
MB_BLOCK = 256
MB_TOPK = 3
ALIBI_MAX_BIAS = 8.0
LANES = 128
MB_PAIRS = MB_WIDTH // LANES
NEG_BIG = -1e30

VMEM_LIMIT = 56 * 1024 * 1024


def _rms(x, w):
    ms = jnp.mean(x * x, axis=-1, keepdims=True)
    return x * lax.rsqrt(ms + NORM_EPS) * w


def _dot(a, b):
    return jnp.dot(a, b, preferred_element_type=F32)


def _dot_nt(a, b):
    return lax.dot_general(a, b, (((1,), (1,)), ((), ())), preferred_element_type=F32)


def _dot_tn(a, b):
    return lax.dot_general(a, b, (((0,), (0,)), ((), ())), preferred_element_type=F32)


def _split2(x):
    hi = x.astype(BF16)
    lo = (x - hi.astype(F32)).astype(BF16)
    return hi, lo


def _const_spec(shape):
    nd = len(shape)
    return pl.BlockSpec(shape, lambda *_: (0,) * nd, pipeline_mode=pl.Buffered(1))


def _ffn_kernel(x_ref, prew_ref, wg_ref, wu_ref, wd_ref, postw_ref, o_ref):
    x = x_ref[...]
    xn = _rms(x, prew_ref[...]).astype(BF16)
    g = _dot(xn, wg_ref[...])
    u = _dot(xn, wu_ref[...])
    a = (g * jax.nn.sigmoid(g) * u).astype(BF16)
    h = _dot(a, wd_ref[...])
    o_ref[...] = x + MACARON_WEIGHT * _rms(h, postw_ref[...])


def _ffn_block(x, pre_w, w_gate, w_up, w_down, post_w, tm=512):
    n, d = x.shape
    dff = w_gate.shape[1]
    return pl.pallas_call(
        _ffn_kernel,
        grid=(n // tm,),
        in_specs=[
            pl.BlockSpec((tm, d), lambda i: (i, 0)),
            _const_spec((1, d)),
            _const_spec((d, dff)),
            _const_spec((d, dff)),
            _const_spec((dff, d)),
            _const_spec((1, d)),
        ],
        out_specs=pl.BlockSpec((tm, d), lambda i: (i, 0)),
        out_shape=jax.ShapeDtypeStruct((n, d), F32),
        compiler_params=pltpu.CompilerParams(
            dimension_semantics=("arbitrary",), vmem_limit_bytes=VMEM_LIMIT),
        name="ffn_block",
    )(x, pre_w, w_gate, w_up, w_down, post_w)


def _inproj_kernel(x_ref, prew_ref, wdn_ref, wba_ref, wqk_hi_ref, wqk_lo_ref, wv_ref,
                   dn_ref, ba_ref, qk_ref, v_ref):
    h = _rms(x_ref[...], prew_ref[...])
    h_hi, h_lo = _split2(h)
    dn_ref[...] = _dot(h_hi, wdn_ref[...])
    ba_ref[...] = _dot(h_hi, wba_ref[0]) + _dot(h_lo, wba_ref[0]) + _dot(h_hi, wba_ref[1])
    qk_ref[...] = (_dot(h_hi, wqk_hi_ref[...]) + _dot(h_lo, wqk_hi_ref[...])
                   + _dot(h_hi, wqk_lo_ref[...]))
    v_ref[...] = _dot(h_hi, wv_ref[...]).astype(BF16)


def _in_proj(x, pre_w, w_dn, w_ba, w_qk_hi, w_qk_lo, w_v, tm=512):
    n, d = x.shape
    return pl.pallas_call(
        _inproj_kernel,
        grid=(n // tm,),
        in_specs=[
            pl.BlockSpec((tm, d), lambda i: (i, 0)),
            _const_spec((1, d)),
            _const_spec(w_dn.shape),
            _const_spec(w_ba.shape),
            _const_spec(w_qk_hi.shape),
            _const_spec(w_qk_lo.shape),
            _const_spec(w_v.shape),
        ],
        out_specs=[
            pl.BlockSpec((tm, w_dn.shape[1]), lambda i: (i, 0)),
            pl.BlockSpec((tm, LANES), lambda i: (i, 0)),
            pl.BlockSpec((tm, w_qk_hi.shape[1]), lambda i: (i, 0)),
            pl.BlockSpec((tm, w_v.shape[1]), lambda i: (i, 0)),
        ],
        out_shape=[
            jax.ShapeDtypeStruct((n, w_dn.shape[1]), F32),
            jax.ShapeDtypeStruct((n, LANES), F32),
            jax.ShapeDtypeStruct((n, w_qk_hi.shape[1]), F32),
            jax.ShapeDtypeStruct((n, w_v.shape[1]), BF16),
        ],
        compiler_params=pltpu.CompilerParams(
            dimension_semantics=("arbitrary",), vmem_limit_bytes=VMEM_LIMIT),
        name="in_proj",
    )(x, pre_w, w_dn, w_ba, w_qk_hi, w_qk_lo, w_v)


def _dn_kernel(qkv_ref, ba_ref, convw_ref, alog_ref, dtb_ref, normw_ref, o_ref,
               xbuf_ref, state_ref):
    tt = DN_TILE
    c = DN_CHUNK
    dk = DN_HEAD_DIM
    t = pl.program_id(1)

    @pl.when(t == 0)
    def _():
        xbuf_ref[0:8, :] = jnp.zeros((8, 3 * DN_WIDTH), F32)
        state_ref[...] = jnp.zeros_like(state_ref)

    x = qkv_ref[0]
    xbuf_ref[8:8 + tt, :] = x
    cw = convw_ref[...]
    y = x * cw[DN_CONV - 1:DN_CONV, :]
    for s in range(1, DN_CONV):
        y = y + xbuf_ref[8 - s:8 - s + tt, :] * cw[DN_CONV - 1 - s:DN_CONV - s, :]
    xbuf_ref[0:8, :] = x[tt - 8:tt, :]
    y = y * jax.nn.sigmoid(y)

    ba = ba_ref[0]
    beta_all = jax.nn.sigmoid(ba)
    g_all = -jnp.exp(alog_ref[...]) * jax.nn.softplus(ba + dtb_ref[...])

    ri = lax.broadcasted_iota(jnp.int32, (tt, tt), 0)
    ci = lax.broadcasted_iota(jnp.int32, (tt, tt), 1)
    same_chunk = (ri // c) == (ci // c)
    incl = same_chunk & (ri >= ci)
    eye = ri == ci
    tril = jnp.where(incl, 1.0, 0.0).astype(BF16)
    ones_bd = jnp.where(same_chunk, 1.0, 0.0).astype(BF16)

    g1 = g_all.astype(BF16)
    r1 = g_all - g1.astype(F32)
    g2 = r1.astype(BF16)
    g3 = (r1 - g2.astype(F32)).astype(BF16)
    gcs_all = _dot(tril, g1) + _dot(tril, g2) + _dot(tril, g3)
    gtot_all = _dot(ones_bd, g1) + _dot(ones_bd, g2) + _dot(ones_bd, g3)

    for h in range(DN_HEADS):
        qr = y[:, h * dk:(h + 1) * dk]
        kr = y[:, DN_WIDTH + h * dk:DN_WIDTH + (h + 1) * dk]
        v = y[:, 2 * DN_WIDTH + h * dk:2 * DN_WIDTH + (h + 1) * dk]
        q = qr * lax.rsqrt(jnp.sum(qr * qr, axis=-1, keepdims=True) + NORM_EPS) * (dk ** -0.5)
        k = kr * lax.rsqrt(jnp.sum(kr * kr, axis=-1, keepdims=True) + NORM_EPS)
        beta = beta_all[:, h:h + 1]
        gcs = gcs_all[:, DN_HEADS + h:DN_HEADS + h + 1]
        gtot = gtot_all[:, DN_HEADS + h:DN_HEADS + h + 1]
        eg = jnp.exp(gcs)

        g_row = jnp.sum(jnp.where(eye, gcs, 0.0), axis=0, keepdims=True)
        decay = jnp.exp(jnp.where(incl, gcs - g_row, NEG_BIG))

        kb = k * beta
        k16 = k.astype(BF16)
        lmat = jnp.where(eye, 0.0, _dot_nt(kb.astype(BF16), k16) * decay)
        attn = _dot_nt(q.astype(BF16), k16) * decay

        xinv = jnp.where(eye, 1.0, -lmat)
        m = lmat.astype(BF16)
        power = 2
        while power < c:
            m32 = _dot(m, m)
            m = m32.astype(BF16)
            xinv = xinv + _dot(xinv.astype(BF16), m)
            power *= 2

        rhs = jnp.concatenate([v * beta, kb * eg], axis=1).astype(BF16)
        uw = _dot(xinv.astype(BF16), rhs)
        u = uw[:, :dk]
        w = uw[:, dk:]
        qd = (q * eg).astype(BF16)
        kd = (k * jnp.exp(gtot - gcs)).astype(BF16)
        w16 = w.astype(BF16)
        attn16 = attn.astype(BF16)

        s = state_ref[h]
        outs = []
        for ch in range(tt // c):
            lo, hi = ch * c, (ch + 1) * c
            s16 = s.astype(BF16)
            v_new = u[lo:hi] - _dot(w16[lo:hi], s16)
            vn16 = v_new.astype(BF16)
            outs.append(_dot(qd[lo:hi], s16) + _dot(attn16[lo:hi, lo:hi], vn16))
            s = s * jnp.exp(gtot[lo:lo + 1, :]) + _dot_tn(kd[lo:hi], vn16)
        state_ref[h] = s
        o = jnp.concatenate(outs, axis=0)
        o_ref[0, :, h * dk:(h + 1) * dk] = _rms(o, normw_ref[...])


def _deltanet(qkv, ba, conv_w, alog_row, dtb_row, norm_w):
    bsz, t_len, width = qkv.shape
    tt = DN_TILE
    return pl.pallas_call(
        _dn_kernel,
        grid=(bsz, t_len // tt),
        in_specs=[
            pl.BlockSpec((1, tt, width), lambda b, t: (b, t, 0)),
            pl.BlockSpec((1, tt, LANES), lambda b, t: (b, t, 0)),
            _const_spec(conv_w.shape),
            _const_spec((1, LANES)),
            _const_spec((1, LANES)),
            _const_spec((1, DN_HEAD_DIM)),
        ],
        out_specs=pl.BlockSpec((1, tt, DN_WIDTH), lambda b, t: (b, t, 0)),
        out_shape=jax.ShapeDtypeStruct((bsz, t_len, DN_WIDTH), F32),
        scratch_shapes=[
            pltpu.VMEM((8 + tt, width), F32),
            pltpu.VMEM((DN_HEADS, DN_HEAD_DIM, DN_HEAD_DIM), F32),
        ],
        compiler_params=pltpu.CompilerParams(
            dimension_semantics=("arbitrary", "arbitrary"), vmem_limit_bytes=VMEM_LIMIT),
        name="deltanet",
    )(qkv, ba, conv_w, alog_row, dtb_row, norm_w)


MB_AUX_BIAS = 4
ALIBI_STEP = int(ALIBI_MAX_BIAS) // MB_HEADS
assert ALIBI_STEP * MB_HEADS == ALIBI_MAX_BIAS


def _moba_kernel(q_ref, k_ref, v_ref, o_ref, kaug_ref, kmean_ref, *, n_blk):
    bs = MB_BLOCK
    p = pl.program_id(1)
    own = pl.program_id(2)
    lane = lax.broadcasted_iota(jnp.int32, (bs, LANES), 1)
    row = lax.broadcasted_iota(jnp.int32, (bs, LANES), 0)

    @pl.when(own == 0)
    def _():
        kmean_ref[...] = jnp.zeros_like(kmean_ref)

        def build(j, carry):
            off = pl.multiple_of(j * bs, bs)
            kblk = k_ref[0, pl.ds(off, bs), :]
            kmean_ref[pl.ds(j, 1), :] = jnp.mean(kblk, axis=0, keepdims=True)
            kaug_ref[pl.ds(off, bs), 0:LANES] = kblk.astype(BF16)
            kstart = jnp.full((bs, LANES), j * bs, jnp.int32).astype(F32)
            aux = jnp.where(lane < 2, 1.0,
                            jnp.where(lane == 2, row.astype(F32),
                                      jnp.where(lane == 3, kstart,
                                                jnp.where(lane == MB_AUX_BIAS + j, 1.0, 0.0))))
            kaug_ref[pl.ds(off, bs), LANES:2 * LANES] = aux.astype(BF16)
            return carry

        lax.fori_loop(0, n_blk, build, 0)

    q32 = q_ref[0]
    kmean = kmean_ref[...]
    km_hi, km_lo = _split2(kmean)
    qoff = pl.multiple_of(own * bs, bs)
    kdiag = kaug_ref[pl.ds(qoff, bs), :]
    vdiag = v_ref[0, pl.ds(qoff, bs), :]
    causal = (lax.broadcasted_iota(jnp.int32, (bs, bs), 0)
              >= lax.broadcasted_iota(jnp.int32, (bs, bs), 1))

    lane_f = lane.astype(F32)
    qi = row.astype(F32)
    qstart = jnp.full((bs, LANES), own * bs, jnp.int32).astype(F32)
    out = jnp.zeros((bs, LANES), F32)
    for hh in range(2):
        head_lanes = (lane >= hh * MB_HEAD_DIM) & (lane < (hh + 1) * MB_HEAD_DIM)
        qh = jnp.where(head_lanes, q32, 0.0)
        slope_bits = (127 - ALIBI_STEP * (2 * p + hh + 1)) << 23
        slope = lax.bitcast_convert_type(jnp.full((bs, LANES), slope_bits, jnp.int32), F32)

        q_hi, q_lo = _split2(qh)
        gate = _dot_nt(q_hi, km_hi) + _dot_nt(q_lo, km_hi) + _dot_nt(q_hi, km_lo)
        gate = jnp.where(lane < own, gate, -jnp.inf)
        sel_f = jnp.zeros((bs, LANES), F32)
        for _ in range(MB_TOPK):
            mx = jnp.max(gate, axis=-1, keepdims=True)
            first = jnp.min(jnp.where(gate == mx, lane_f, float(LANES)), axis=-1, keepdims=True)
            hit = lane_f == first
            sel_f = jnp.where(hit, 1.0, sel_f)
            gate = jnp.where(hit, -jnp.inf, gate)
        keep = jnp.where(lane < own, sel_f, jnp.where(lane == own, 1.0, 0.0))
        maskcol = jnp.where(keep > 0.5, 0.0, NEG_BIG)

        qaux = jnp.where(lane == 0, -slope * qi,
                         jnp.where(lane == 1, -slope * qstart,
                                   jnp.where(lane < MB_AUX_BIAS, slope, 0.0)))
        shifted_sel = pltpu.roll(maskcol, MB_AUX_BIAS, axis=1)
        qaux = jnp.where(lane >= MB_AUX_BIAS, shifted_sel, qaux)
        qaug = jnp.concatenate([(qh * (MB_HEAD_DIM ** -0.5)).astype(BF16), qaux.astype(BF16)],
                               axis=1)

        s = _dot_nt(qaug, kdiag)
        s = jnp.where(causal, s, NEG_BIG)
        m0 = jnp.max(s, axis=-1, keepdims=True)
        pexp = jnp.exp(s - m0)
        l0 = jnp.sum(pexp, axis=-1, keepdims=True)
        acc0 = _dot(pexp.astype(BF16), vdiag)

        def body(j, carry):
            m_i, l_i, acc = carry
            off = pl.multiple_of(j * bs, bs)
            kj = kaug_ref[pl.ds(off, bs), :]
            vj = v_ref[0, pl.ds(off, bs), :]
            sj = _dot_nt(qaug, kj)
            m_new = jnp.maximum(m_i, jnp.max(sj, axis=-1, keepdims=True))
            alpha = jnp.exp(m_i - m_new)
            pj = jnp.exp(sj - m_new)
            l_new = l_i * alpha + jnp.sum(pj, axis=-1, keepdims=True)
            acc_new = acc * alpha + _dot(pj.astype(BF16), vj)
            return m_new, l_new, acc_new

        _, l_f, acc_f = lax.fori_loop(0, own, body, (m0, l0, acc0))
        out = jnp.where(head_lanes, acc_f / l_f, out)

    o_ref[0] = out


def _moba(qk, v):
    bsz, t_len, _ = v.shape
    bs = MB_BLOCK
    n_blk = t_len // bs
    return pl.pallas_call(
        functools.partial(_moba_kernel, n_blk=n_blk),
        grid=(bsz, MB_PAIRS, n_blk),
        in_specs=[
            pl.BlockSpec((1, bs, LANES), lambda b, p, i: (b, i, p)),
            pl.BlockSpec((1, t_len, LANES), lambda b, p, i: (b, 0, MB_PAIRS + p)),
            pl.BlockSpec((1, t_len, LANES), lambda b, p, i: (b, 0, p)),
        ],
        out_specs=pl.BlockSpec((1, bs, LANES), lambda b, p, i: (b, i, p)),
        out_shape=jax.ShapeDtypeStruct((bsz, t_len, MB_WIDTH), F32),
        scratch_shapes=[
            pltpu.VMEM((t_len, 2 * LANES), BF16),
            pltpu.VMEM((LANES, LANES), F32),
        ],
        compiler_params=pltpu.CompilerParams(
            dimension_semantics=("arbitrary", "arbitrary", "arbitrary"),
            vmem_limit_bytes=VMEM_LIMIT),
        name="moba",
    )(qk, qk, v)


def _mixout_kernel(x_ref, odn_ref, omb_ref, prew_ref, wz_ref, wgd_ref, wgm_ref,
                   wbd_ref, wbm_ref, wo_ref, postw_ref, o_ref):
    x = x_ref[...]
    h = _rms(x, prew_ref[...]).astype(BF16)
    z = _dot(h, wz_ref[...])
    gate_dn = jax.nn.sigmoid(_dot(h, wgd_ref[...]))
    gate_mb = jax.nn.sigmoid(_dot(h, wgm_ref[...]))
    o_dn = odn_ref[...] * (z * jax.nn.sigmoid(z))
    y_dn = _dot(o_dn.astype(BF16), wbd_ref[...])
    y_mb = _dot(omb_ref[...].astype(BF16), wbm_ref[...])
    merged = gate_dn * y_dn + gate_mb * y_mb
    y = _dot(merged.astype(BF16), wo_ref[...])
    o_ref[...] = x + _rms(y, postw_ref[...])


def _mix_out(x, o_dn, o_mb, pre_w, w_z, w_gd, w_gm, w_bd, w_bm, w_o, post_w, tm=512):
    n, d = x.shape
    return pl.pallas_call(
        _mixout_kernel,
        grid=(n // tm,),
        in_specs=[
            pl.BlockSpec((tm, d), lambda i: (i, 0)),
            pl.BlockSpec((tm, o_dn.shape[1]), lambda i: (i, 0)),
            pl.BlockSpec((tm, o_mb.shape[1]), lambda i: (i, 0)),
            _const_spec((1, d)),
            _const_spec(w_z.shape),
            _const_spec(w_gd.shape),
            _const_spec(w_gm.shape),
            _const_spec(w_bd.shape),
            _const_spec(w_bm.shape),
            _const_spec(w_o.shape),
            _const_spec((1, d)),
        ],
        out_specs=pl.BlockSpec((tm, d), lambda i: (i, 0)),
        out_shape=jax.ShapeDtypeStruct((n, d), F32),
        compiler_params=pltpu.CompilerParams(
            dimension_semantics=("arbitrary",), vmem_limit_bytes=VMEM_LIMIT),
        name="mix_out",
    )(x, o_dn, o_mb, pre_w, w_z, w_gd, w_gm, w_bd, w_bm, w_o, post_w)


def _layer(x, ffn1_pre_w, ffn1_w_gate, ffn1_w_up, ffn1_w_down, ffn1_post_w,
           mix_pre_w, w_in, dn_conv_w, dn_a_log, dn_dt_bias, dn_norm_w,
           w_branch_dn, w_branch_mb, w_out, mix_post_w,
           ffn2_pre_w, ffn2_w_gate, ffn2_w_up, ffn2_w_down, ffn2_post_w):
    bsz, t_len, d = x.shape
    n = bsz * t_len
    row = lambda w: w.reshape(1, -1).astype(F32)
    b16 = lambda w: w.astype(BF16)

    x = x.reshape(n, d)
    x = _ffn_block(x, row(ffn1_pre_w), b16(ffn1_w_gate), b16(ffn1_w_up), b16(ffn1_w_down),
                   row(ffn1_post_w))

    o = 0
    w_dn = w_in[:, o:o + 3 * DN_WIDTH]; o += 3 * DN_WIDTH
    w_z = w_in[:, o:o + DN_WIDTH]; o += DN_WIDTH
    w_ba = w_in[:, o:o + 2 * DN_HEADS]; o += 2 * DN_HEADS
    w_qk = w_in[:, o:o + 2 * MB_WIDTH]; o += 2 * MB_WIDTH
    w_v = w_in[:, o:o + MB_WIDTH]; o += MB_WIDTH
    w_gd = w_in[:, o:o + d]; o += d
    w_gm = w_in[:, o:o + d]; o += d
    w_ba = jnp.pad(w_ba, ((0, 0), (0, LANES - 2 * DN_HEADS)))
    w_ba_hi = w_ba.astype(BF16)
    w_ba_lo = (w_ba - w_ba_hi.astype(F32)).astype(BF16)
    w_qk_hi = w_qk.astype(BF16)
    w_qk_lo = (w_qk - w_qk_hi.astype(F32)).astype(BF16)

    dn_qkv, ba, mb_qk, mb_v = _in_proj(x, row(mix_pre_w), b16(w_dn), jnp.stack([w_ba_hi, w_ba_lo]),
                                       w_qk_hi, w_qk_lo, b16(w_v))

    pad_heads = lambda p: jnp.pad(p.astype(F32), (DN_HEADS, LANES - 2 * DN_HEADS)).reshape(1, LANES)
    o_dn = _deltanet(dn_qkv.reshape(bsz, t_len, -1), ba.reshape(bsz, t_len, LANES),
                     dn_conv_w.astype(F32), pad_heads(dn_a_log), pad_heads(dn_dt_bias),
                     row(dn_norm_w))

    o_mb = _moba(mb_qk.reshape(bsz, t_len, 2 * MB_WIDTH), mb_v.reshape(bsz, t_len, MB_WIDTH))

    x = _mix_out(x, o_dn.reshape(n, DN_WIDTH), o_mb.reshape(n, MB_WIDTH), row(mix_pre_w),
                 b16(w_z), b16(w_gd), b16(w_gm), b16(w_branch_dn), b16(w_branch_mb), b16(w_out),
                 row(mix_post_w))

    x = _ffn_block(x, row(ffn2_pre_w), b16(ffn2_w_gate), b16(ffn2_w_up), b16(ffn2_w_down),
                   row(ffn2_post_w))
    return x.reshape(bsz, t_len, d)


def kernel(x, ffn1_pre_w, ffn1_w_gate, ffn1_w_up, ffn1_w_down, ffn1_post_w, mix_pre_w, w_in, dn_conv_w, dn_a_log, dn_dt_bias, dn_norm_w, w_branch_dn, w_branch_mb, w_out, mix_post_w, ffn2_pre_w, ffn2_w_gate, ffn2_w_up, ffn2_w_down, ffn2_post_w):
    depth = w_in.shape[0]
    for l in range(depth):
        x = _layer(x, ffn1_pre_w[l], ffn1_w_gate[l], ffn1_w_up[l], ffn1_w_down[l], ffn1_post_w[l],
                   mix_pre_w[l], w_in[l], dn_conv_w[l], dn_a_log[l], dn_dt_bias[l], dn_norm_w[l],
                   w_branch_dn[l], w_branch_mb[l], w_out[l], mix_post_w[l],
                   ffn2_pre_w[l], ffn2_w_gate[l], ffn2_w_up[l], ffn2_w_down[l], ffn2_post_w[l])
    return x
```

```python
import functools
import math

import jax
import jax.numpy as jnp
from jax import lax
from jax.experimental import pallas as pl
from jax.experimental.pallas import tpu as pltpu

F32 = jnp.float32
BF16 = jnp.bfloat16

NORM_EPS = 1e-6
MACARON_WEIGHT = 0.5

DN_HEADS = 4
DN_HEAD_DIM = 128
DN_WIDTH = DN_HEADS * DN_HEAD_DIM
DN_CONV = 4
DN_CHUNK = 64
DN_TILE = 256

MB_HEADS = 8
MB_HEAD_DIM = 64
MB_WIDTH = MB_HEADS * MB_HEAD_DIM
MB_BLOCK = 256
MB_TOPK = 3
ALIBI_MAX_BIAS = 8.0
LANES = 128
MB_PAIRS = MB_WIDTH // LANES
NEG_BIG = -1e30

VMEM_LIMIT = 56 * 1024 * 1024


def _rms(x, w):
    ms = jnp.mean(x * x, axis=-1, keepdims=True)
    return x * lax.rsqrt(ms + NORM_EPS) * w


def _dot(a, b):
    return jnp.dot(a, b, preferred_element_type=F32)


def _dot_nt(a, b):
    return lax.dot_general(a, b, (((1,), (1,)), ((), ())), preferred_element_type=F32)


def _dot_tn(a, b):
    return lax.dot_general(a, b, (((0,), (0,)), ((), ())), preferred_element_type=F32)


def _split2(x):
    hi = x.astype(BF16)
    lo = (x - hi.astype(F32)).astype(BF16)
    return hi, lo


def _const_spec(shape):
    nd = len(shape)
    return pl.BlockSpec(shape, lambda *_: (0,) * nd, pipeline_mode=pl.Buffered(1))


def _ffn_kernel(x_ref, prew_ref, wg_ref, wu_ref, wd_ref, postw_ref, o_ref):
    x = x_ref[...]
    xn = _rms(x, prew_ref[...]).astype(BF16)
    g = _dot(xn, wg_ref[...])
    u = _dot(xn, wu_ref[...])
    a = (g * jax.nn.sigmoid(g) * u).astype(BF16)
    h = _dot(a, wd_ref[...])
    o_ref[...] = x + MACARON_WEIGHT * _rms(h, postw_ref[...])


def _ffn_block(x, pre_w, w_gate, w_up, w_down, post_w, tm=512):
    n, d = x.shape
    dff = w_gate.shape[1]
    return pl.pallas_call(
        _ffn_kernel,
        grid=(n // tm,),
        in_specs=[
            pl.BlockSpec((tm, d), lambda i: (i, 0)),
            _const_spec((1, d)),
            _const_spec((d, dff)),
            _const_spec((d, dff)),
            _const_spec((dff, d)),
            _const_spec((1, d)),
        ],
        out_specs=pl.BlockSpec((tm, d), lambda i: (i, 0)),
        out_shape=jax.ShapeDtypeStruct((n, d), F32),
        compiler_params=pltpu.CompilerParams(
            dimension_semantics=("arbitrary",), vmem_limit_bytes=VMEM_LIMIT),
        name="ffn_block",
    )(x, pre_w, w_gate, w_up, w_down, post_w)


def _inproj_kernel(x_ref, prew_ref, wdn_ref, wba_ref, wk_ref, wqt_ref, wvt_ref,
                   dn_ref, ba_ref, k_ref, qt_ref, vt_ref):
    bs = MB_BLOCK
    h = _rms(x_ref[...], prew_ref[...])
    h_hi, h_lo = _split2(h)
    dn_ref[...] = _dot(h_hi, wdn_ref[...])
    ba_ref[...] = _dot(h_hi, wba_ref[0]) + _dot(h_lo, wba_ref[0]) + _dot(h_hi, wba_ref[1])
    k_ref[...] = _dot(h_hi, wk_ref[0]) + _dot(h_lo, wk_ref[0]) + _dot(h_hi, wk_ref[1])
    qt = _dot_nt(wqt_ref[0], h_hi) + _dot_nt(wqt_ref[0], h_lo) + _dot_nt(wqt_ref[1], h_hi)
    vt = _dot_nt(wvt_ref[...], h_hi).astype(BF16)
    for i in range(qt_ref.shape[0]):
        qt_ref[i] = qt[:, i * bs:(i + 1) * bs]
        vt_ref[i] = vt[:, i * bs:(i + 1) * bs]


def _in_proj(x, pre_w, w_dn, w_ba, w_k, w_qt, w_vt, tm=512):
    n, d = x.shape
    bs = MB_BLOCK
    return pl.pallas_call(
        _inproj_kernel,
        grid=(n // tm,),
        in_specs=[
            pl.BlockSpec((tm, d), lambda i: (i, 0)),
            _const_spec((1, d)),
            _const_spec(w_dn.shape),
            _const_spec(w_ba.shape),
            _const_spec(w_k.shape),
            _const_spec(w_qt.shape),
            _const_spec(w_vt.shape),
        ],
        out_specs=[
            pl.BlockSpec((tm, w_dn.shape[1]), lambda i: (i, 0)),
            pl.BlockSpec((tm, LANES), lambda i: (i, 0)),
            pl.BlockSpec((tm, MB_WIDTH), lambda i: (i, 0)),
            pl.BlockSpec((tm // bs, MB_WIDTH, bs), lambda i: (i, 0, 0)),
            pl.BlockSpec((tm // bs, MB_WIDTH, bs), lambda i: (i, 0, 0)),
        ],
        out_shape=[
            jax.ShapeDtypeStruct((n, w_dn.shape[1]), F32),
            jax.ShapeDtypeStruct((n, LANES), F32),
            jax.ShapeDtypeStruct((n, MB_WIDTH), F32),
            jax.ShapeDtypeStruct((n // bs, MB_WIDTH, bs), F32),
            jax.ShapeDtypeStruct((n // bs, MB_WIDTH, bs), BF16),
        ],
        compiler_params=pltpu.CompilerParams(
            dimension_semantics=("arbitrary",), vmem_limit_bytes=VMEM_LIMIT),
        name="in_proj",
    )(x, pre_w, w_dn, w_ba, w_k, w_qt, w_vt)


def _dn_kernel(qkv_ref, ba_ref, convw_ref, alog_ref, dtb_ref, normw_ref, o_ref,
               xbuf_ref, state_ref):
    tt = DN_TILE
    c = DN_CHUNK
    dk = DN_HEAD_DIM
    t = pl.program_id(1)

    @pl.when(t == 0)
    def _():
        xbuf_ref[0:8, :] = jnp.zeros((8, 3 * DN_WIDTH), F32)
        state_ref[...] = jnp.zeros_like(state_ref)

    x = qkv_ref[0]
    xbuf_ref[8:8 + tt, :] = x
    cw = convw_ref[...]
    y = x * cw[DN_CONV - 1:DN_CONV, :]
    for s in range(1, DN_CONV):
        y = y + xbuf_ref[8 - s:8 - s + tt, :] * cw[DN_CONV - 1 - s:DN_CONV - s, :]
    xbuf_ref[0:8, :] = x[tt - 8:tt, :]
    y = y * jax.nn.sigmoid(y)

    ba = ba_ref[0]
    beta_all = jax.nn.sigmoid(ba)
    g_all = -jnp.exp(alog_ref[...]) * jax.nn.softplus(ba + dtb_ref[...])

    ri = lax.broadcasted_iota(jnp.int32, (tt, tt), 0)
    ci = lax.broadcasted_iota(jnp.int32, (tt, tt), 1)
    same_chunk = (ri // c) == (ci // c)
    incl = same_chunk & (ri >= ci)
    eye = ri == ci
    tril = jnp.where(incl, 1.0, 0.0).astype(BF16)
    ones_bd = jnp.where(same_chunk, 1.0, 0.0).astype(BF16)

    g1 = g_all.astype(BF16)
    r1 = g_all - g1.astype(F32)
    g2 = r1.astype(BF16)
    g3 = (r1 - g2.astype(F32)).astype(BF16)
    gcs_all = _dot(tril, g1) + _dot(tril, g2) + _dot(tril, g3)
    gtot_all = _dot(ones_bd, g1) + _dot(ones_bd, g2) + _dot(ones_bd, g3)

    for h in range(DN_HEADS):
        qr = y[:, h * dk:(h + 1) * dk]
        kr = y[:, DN_WIDTH + h * dk:DN_WIDTH + (h + 1) * dk]
        v = y[:, 2 * DN_WIDTH + h * dk:2 * DN_WIDTH + (h + 1) * dk]
        q = qr * lax.rsqrt(jnp.sum(qr * qr, axis=-1, keepdims=True) + NORM_EPS) * (dk ** -0.5)
        k = kr * lax.rsqrt(jnp.sum(kr * kr, axis=-1, keepdims=True) + NORM_EPS)
        beta = beta_all[:, h:h + 1]
        gcs = gcs_all[:, DN_HEADS + h:DN_HEADS + h + 1]
        gtot = gtot_all[:, DN_HEADS + h:DN_HEADS + h + 1]
        eg = jnp.exp(gcs)

        g_row = jnp.sum(jnp.where(eye, gcs, 0.0), axis=0, keepdims=True)
        decay = jnp.exp(jnp.where(incl, gcs - g_row, NEG_BIG))

        kb = k * beta
        k16 = k.astype(BF16)
        lmat = jnp.where(eye, 0.0, _dot_nt(kb.astype(BF16), k16) * decay)
        attn = _dot_nt(q.astype(BF16), k16) * decay

        xinv = jnp.where(eye, 1.0, -lmat)
        m = lmat.astype(BF16)
        power = 2
        while power < c:
            m32 = _dot(m, m)
            m = m32.astype(BF16)
            xinv = xinv + _dot(xinv.astype(BF16), m)
            power *= 2

        rhs = jnp.concatenate([v * beta, kb * eg], axis=1).astype(BF16)
        uw = _dot(xinv.astype(BF16), rhs)
        u = uw[:, :dk]
        w = uw[:, dk:]
        qd = (q * eg).astype(BF16)
        kd = (k * jnp.exp(gtot - gcs)).astype(BF16)
        w16 = w.astype(BF16)
        attn16 = attn.astype(BF16)

        s = state_ref[h]
        outs = []
        for ch in range(tt // c):
            lo, hi = ch * c, (ch + 1) * c
            s16 = s.astype(BF16)
            v_new = u[lo:hi] - _dot(w16[lo:hi], s16)
            vn16 = v_new.astype(BF16)
            outs.append(_dot(qd[lo:hi], s16) + _dot(attn16[lo:hi, lo:hi], vn16))
            s = s * jnp.exp(gtot[lo:lo + 1, :]) + _dot_tn(kd[lo:hi], vn16)
        state_ref[h] = s
        o = jnp.concatenate(outs, axis=0)
        o_ref[0, :, h * dk:(h + 1) * dk] = _rms(o, normw_ref[...])


def _deltanet(qkv, ba, conv_w, alog_row, dtb_row, norm_w):
    bsz, t_len, width = qkv.shape
    tt = DN_TILE
    return pl.pallas_call(
        _dn_kernel,
        grid=(bsz, t_len // tt),
        in_specs=[
            pl.BlockSpec((1, tt, width), lambda b, t: (b, t, 0)),
            pl.BlockSpec((1, tt, LANES), lambda b, t: (b, t, 0)),
            _const_spec(conv_w.shape),
            _const_spec((1, LANES)),
            _const_spec((1, LANES)),
            _const_spec((1, DN_HEAD_DIM)),
        ],
        out_specs=pl.BlockSpec((1, tt, DN_WIDTH), lambda b, t: (b, t, 0)),
        out_shape=jax.ShapeDtypeStruct((bsz, t_len, DN_WIDTH), F32),
        scratch_shapes=[
            pltpu.VMEM((8 + tt, width), F32),
            pltpu.VMEM((DN_HEADS, DN_HEAD_DIM, DN_HEAD_DIM), F32),
        ],
        compiler_params=pltpu.CompilerParams(
            dimension_semantics=("arbitrary", "arbitrary"), vmem_limit_bytes=VMEM_LIMIT),
        name="deltanet",
    )(qkv, ba, conv_w, alog_row, dtb_row, norm_w)


SUBLANES = 8
MB_SUPER = 4
MB_AUX_MASK = 16
ALIBI_STEP = int(ALIBI_MAX_BIAS) // MB_HEADS
assert ALIBI_STEP * MB_HEADS == ALIBI_MAX_BIAS
LOG2E = math.log2(math.e)
LOG2E_PIECES = (1.4453125, -0.00262451171875, 7.063150405883789e-06, -1.05355866253376e-08)


def _moba_kernel(qt_ref, k_ref, vt_ref, o_ref, kaug_ref, kmean_ref, *, n_blk):
    bs = MB_BLOCK
    hd = MB_HEAD_DIM
    sup = MB_SUPER * bs
    nbp = -(-n_blk // SUBLANES) * SUBLANES
    p = pl.program_id(1)
    own = pl.program_id(2)
    lane = lax.broadcasted_iota(jnp.int32, (bs, LANES), 1)
    row = lax.broadcasted_iota(jnp.int32, (bs, LANES), 0)

    @pl.when(own == 0)
    def _():
        kmean_ref[...] = jnp.zeros_like(kmean_ref)

        def build(j, carry):
            off = pl.multiple_of(j * bs, bs)
            kblk = k_ref[0, pl.ds(off, bs), :]
            kmean_ref[pl.ds(j, 1), :] = jnp.mean(kblk, axis=0, keepdims=True)
            kaug_ref[pl.ds(off, bs), 0:LANES] = kblk.astype(BF16)
            kstart = jnp.full((bs, LANES), j * bs, jnp.int32).astype(F32)
            aux = jnp.where(lane < 2, 1.0,
                            jnp.where(lane < 6, row.astype(F32),
                                      jnp.where(lane < 10, kstart,
                                                jnp.where(lane == MB_AUX_MASK + j, 1.0, 0.0))))
            kaug_ref[pl.ds(off, bs), LANES:2 * LANES] = aux.astype(BF16)
            return carry

        lax.fori_loop(0, n_blk, build, 0)

    qt = qt_ref[0]
    km_hi, km_lo = _split2(kmean_ref[...])

    chan = lax.broadcasted_iota(jnp.int32, (LANES, bs), 0)
    blk = lax.broadcasted_iota(jnp.int32, (nbp, bs), 0)
    blk_f = blk.astype(F32)
    aux_row = lax.broadcasted_iota(jnp.int32, (MB_AUX_MASK, bs), 0)
    qpos = (lax.broadcasted_iota(jnp.int32, (MB_AUX_MASK, bs), 1) + own * bs).astype(F32)
    aux_pad = jnp.zeros((LANES - MB_AUX_MASK - nbp, bs), F32)

    qaug = []
    for hh in range(2):
        qth = jnp.where((chan >= hh * hd) & (chan < (hh + 1) * hd), qt, 0.0)
        slope_bits = (127 - ALIBI_STEP * (2 * p + hh + 1)) << 23
        slope = lax.bitcast_convert_type(jnp.full((MB_AUX_MASK, bs), slope_bits, jnp.int32), F32)

        q_hi, q_lo = _split2(qth)
        gate = (_dot(km_hi, q_hi) + _dot(km_hi, q_lo) + _dot(km_lo, q_hi))[:nbp]
        gate = jnp.where(blk < own, gate, -jnp.inf)
        sel = jnp.zeros((nbp, bs), F32)
        for _ in range(MB_TOPK):
            mx = jnp.max(gate, axis=0, keepdims=True)
            first = jnp.min(jnp.where(gate == mx, blk_f, float(nbp)), axis=0, keepdims=True)
            hit = blk_f == first
            sel = jnp.where(hit, 1.0, sel)
            gate = jnp.where(hit, -jnp.inf, gate)
        keep = jnp.where(blk < own, sel, jnp.where(blk == own, 1.0, 0.0))
        mask_rows = jnp.where(keep > 0.5, 0.0, NEG_BIG)
        qconst = -(slope * LOG2E) * qpos
        qconst_hi = qconst.astype(BF16).astype(F32)
        piece_id = (aux_row + 2) & 3
        piece = jnp.where(piece_id == 0, LOG2E_PIECES[0],
                          jnp.where(piece_id == 1, LOG2E_PIECES[1],
                                    jnp.where(piece_id == 2, LOG2E_PIECES[2], LOG2E_PIECES[3])))
        bias_rows = jnp.where(aux_row == 0, qconst_hi,
                              jnp.where(aux_row == 1, qconst - qconst_hi,
                                        jnp.where(aux_row < 10, slope * piece, 0.0)))
        qaug.append(jnp.concatenate([qth * (hd ** -0.5 * LOG2E), bias_rows, mask_rows, aux_pad],
                                    axis=0).astype(BF16))

    def keys(i):
        return kaug_ref[pl.ds(pl.multiple_of(i * sup, sup), sup), :]

    def values_t(i, hh):
        return jnp.concatenate([vt_ref[i * MB_SUPER + u, hh * hd:(hh + 1) * hd, :]
                                for u in range(MB_SUPER)], axis=1)

    grp = own // MB_SUPER
    rel = (lax.broadcasted_iota(jnp.int32, (sup, bs), 0) - (own - grp * MB_SUPER) * bs)
    qi = lax.broadcasted_iota(jnp.int32, (sup, bs), 1)
    future = (rel > qi) & (rel < bs)
    qaug2 = jnp.concatenate(qaug, axis=1)
    s2 = _dot(keys(grp), qaug2)
    init = []
    for hh in range(2):
        s = jnp.where(future, NEG_BIG, s2[:, hh * bs:(hh + 1) * bs])
        m0 = jnp.max(s, axis=0, keepdims=True)
        pexp = jnp.exp2(s - m0)
        l0 = jnp.sum(pexp, axis=0, keepdims=True)
        acc0 = _dot(values_t(grp, hh), pexp.astype(BF16))
        init += [m0, l0, acc0]

    def body(i, carry):
        s2 = _dot(keys(i), qaug2)
        new = []
        for hh in range(2):
            m_i, l_i, acc = carry[3 * hh:3 * hh + 3]
            s = s2[:, hh * bs:(hh + 1) * bs]
            m_new = jnp.maximum(m_i, jnp.max(s, axis=0, keepdims=True))
            alpha = jnp.exp2(m_i - m_new)
            pexp = jnp.exp2(s - m_new)
            l_new = l_i * alpha + jnp.sum(pexp, axis=0, keepdims=True)
            acc_new = acc * alpha + _dot(values_t(i, hh), pexp.astype(BF16))
            new += [m_new, l_new, acc_new]
        return tuple(new)

    fin = lax.fori_loop(0, grp, body, tuple(init))
    out_t = jnp.concatenate([fin[2] / fin[1], fin[5] / fin[4]], axis=0)
    o_ref[0] = out_t.T


def _moba(qt, k, vt):
    bsz, t_len, _ = k.shape
    bs = MB_BLOCK
    n_blk = t_len // bs
    assert n_blk % MB_SUPER == 0 and MB_AUX_MASK + n_blk <= LANES
    return pl.pallas_call(
        functools.partial(_moba_kernel, n_blk=n_blk),
        grid=(bsz, MB_PAIRS, n_blk),
        in_specs=[
            pl.BlockSpec((1, LANES, bs), lambda b, p, i: (b * n_blk + i, p, 0)),
            pl.BlockSpec((1, t_len, LANES), lambda b, p, i: (b, 0, p)),
            pl.BlockSpec((n_blk, LANES, bs), lambda b, p, i: (b, p, 0)),
        ],
        out_specs=pl.BlockSpec((1, bs, LANES), lambda b, p, i: (b, i, p)),
        out_shape=jax.ShapeDtypeStruct((bsz, t_len, MB_WIDTH), F32),
        scratch_shapes=[
            pltpu.VMEM((t_len, 2 * LANES), BF16),
            pltpu.VMEM((LANES, LANES), F32),
        ],
        compiler_params=pltpu.CompilerParams(
            dimension_semantics=("arbitrary", "arbitrary", "arbitrary"),
            vmem_limit_bytes=VMEM_LIMIT),
        name="moba",
    )(qt, k, vt)


def _mixout_kernel(x_ref, odn_ref, omb_ref, prew_ref, wz_ref, wgd_ref, wgm_ref,
                   wbd_ref, wbm_ref, wo_ref, postw_ref, o_ref):
    x = x_ref[...]
    h = _rms(x, prew_ref[...]).astype(BF16)
    z = _dot(h, wz_ref[...])
    gate_dn = jax.nn.sigmoid(_dot(h, wgd_ref[...]))
    gate_mb = jax.nn.sigmoid(_dot(h, wgm_ref[...]))
    o_dn = odn_ref[...] * (z * jax.nn.sigmoid(z))
    y_dn = _dot(o_dn.astype(BF16), wbd_ref[...])
    y_mb = _dot(omb_ref[...].astype(BF16), wbm_ref[...])
    merged = gate_dn * y_dn + gate_mb * y_mb
    y = _dot(merged.astype(BF16), wo_ref[...])
    o_ref[...] = x + _rms(y, postw_ref[...])


def _mix_out(x, o_dn, o_mb, pre_w, w_z, w_gd, w_gm, w_bd, w_bm, w_o, post_w, tm=512):
    n, d = x.shape
    return pl.pallas_call(
        _mixout_kernel,
        grid=(n // tm,),
        in_specs=[
            pl.BlockSpec((tm, d), lambda i: (i, 0)),
            pl.BlockSpec((tm, o_dn.shape[1]), lambda i: (i, 0)),
            pl.BlockSpec((tm, o_mb.shape[1]), lambda i: (i, 0)),
            _const_spec((1, d)),
            _const_spec(w_z.shape),
            _const_spec(w_gd.shape),
            _const_spec(w_gm.shape),
            _const_spec(w_bd.shape),
            _const_spec(w_bm.shape),
            _const_spec(w_o.shape),
            _const_spec((1, d)),
        ],
        out_specs=pl.BlockSpec((tm, d), lambda i: (i, 0)),
        out_shape=jax.ShapeDtypeStruct((n, d), F32),
        compiler_params=pltpu.CompilerParams(
            dimension_semantics=("arbitrary",), vmem_limit_bytes=VMEM_LIMIT),
        name="mix_out",
    )(x, o_dn, o_mb, pre_w, w_z, w_gd, w_gm, w_bd, w_bm, w_o, post_w)


def _layer(x, ffn1_pre_w, ffn1_w_gate, ffn1_w_up, ffn1_w_down, ffn1_post_w,
           mix_pre_w, w_in, dn_conv_w, dn_a_log, dn_dt_bias, dn_norm_w,
           w_branch_dn, w_branch_mb, w_out, mix_post_w,
           ffn2_pre_w, ffn2_w_gate, ffn2_w_up, ffn2_w_down, ffn2_post_w):
    bsz, t_len, d = x.shape
    n = bsz * t_len
    row = lambda w: w.reshape(1, -1).astype(F32)
    b16 = lambda w: w.astype(BF16)

    x = x.reshape(n, d)
    x = _ffn_block(x, row(ffn1_pre_w), b16(ffn1_w_gate), b16(ffn1_w_up), b16(ffn1_w_down),
                   row(ffn1_post_w))

    o = 0
    w_dn = w_in[:, o:o + 3 * DN_WIDTH]; o += 3 * DN_WIDTH
    w_z = w_in[:, o:o + DN_WIDTH]; o += DN_WIDTH
    w_ba = w_in[:, o:o + 2 * DN_HEADS]; o += 2 * DN_HEADS
    w_q = w_in[:, o:o + MB_WIDTH]; o += MB_WIDTH
    w_k = w_in[:, o:o + MB_WIDTH]; o += MB_WIDTH
    w_v = w_in[:, o:o + MB_WIDTH]; o += MB_WIDTH
    w_gd = w_in[:, o:o + d]; o += d
    w_gm = w_in[:, o:o + d]; o += d
    w_ba = jnp.pad(w_ba, ((0, 0), (0, LANES - 2 * DN_HEADS)))

    def hi_lo(w):
        hi = w.astype(BF16)
        return jnp.stack([hi, (w - hi.astype(F32)).astype(BF16)])

    dn_qkv, ba, mb_k, mb_qt, mb_vt = _in_proj(x, row(mix_pre_w), b16(w_dn), hi_lo(w_ba), hi_lo(w_k),
                                              hi_lo(w_q.T), b16(w_v.T))

    pad_heads = lambda p: jnp.pad(p.astype(F32), (DN_HEADS, LANES - 2 * DN_HEADS)).reshape(1, LANES)
    o_dn = _deltanet(dn_qkv.reshape(bsz, t_len, -1), ba.reshape(bsz, t_len, LANES),
                     dn_conv_w.astype(F32), pad_heads(dn_a_log), pad_heads(dn_dt_bias),
                     row(dn_norm_w))

    o_mb = _moba(mb_qt, mb_k.reshape(bsz, t_len, MB_WIDTH), mb_vt)

    x = _mix_out(x, o_dn.reshape(n, DN_WIDTH), o_mb.reshape(n, MB_WIDTH), row(mix_pre_w),
                 b16(w_z), b16(w_gd), b16(w_gm), b16(w_branch_dn), b16(w_branch_mb), b16(w_out),
                 row(mix_post_w))

    x = _ffn_block(x, row(ffn2_pre_w), b16(ffn2_w_gate), b16(ffn2_w_up), b16(ffn2_w_down),
                   row(ffn2_post_w))
    return x.reshape(bsz, t_len, d)


def kernel(x, ffn1_pre_w, ffn1_w_gate, ffn1_w_up, ffn1_w_down, ffn1_post_w, mix_pre_w, w_in, dn_conv_w, dn_a_log, dn_dt_bias, dn_norm_w, w_branch_dn, w_branch_mb, w_out, mix_post_w, ffn2_pre_w, ffn2_w_gate, ffn2_w_up, ffn2_w_down, ffn2_post_w):
    depth = w_in.shape[0]
    for l in range(depth):
        x = _layer(x, ffn1_pre_w[l], ffn1_w_gate[l], ffn1_w_up[l], ffn1_w_down[l], ffn1_post_w[l],
                   mix_pre_w[l], w_in[l], dn_conv_w[l], dn_a_log[l], dn_dt_bias[l], dn_norm_w[l],
                   w_branch_dn[l], w_branch_mb[l], w_out[l], mix_post_w[l],
                   ffn2_pre_w[l], ffn2_w_gate[l], ffn2_w_up[l], ffn2_w_down[l], ffn2_post_w[l])
    return x
```

```python
import functools
import math

import jax
import jax.numpy as jnp
from jax import lax
from jax.experimental import pallas as pl
from jax.experimental.pallas import tpu as pltpu

F32 = jnp.float32
BF16 = jnp.bfloat16

NORM_EPS = 1e-6
MACARON_WEIGHT = 0.5

DN_HEADS = 4
DN_HEAD_DIM = 128
DN_WIDTH = DN_HEADS * DN_HEAD_DIM
DN_CONV = 4
DN_CHUNK = 64
DN_TILE = 256

MB_HEADS = 8
MB_HEAD_DIM = 64
MB_WIDTH = MB_HEADS * MB_HEAD_DIM
MB_BLOCK = 256
MB_TOPK = 3
ALIBI_MAX_BIAS = 8.0
LANES = 128
MB_PAIRS = MB_WIDTH // LANES
NEG_BIG = -1e30

VMEM_LIMIT = 56 * 1024 * 1024


def _rms(x, w):
    ms = jnp.mean(x * x, axis=-1, keepdims=True)
    return x * lax.rsqrt(ms + NORM_EPS) * w


def _dot(a, b):
    return jnp.dot(a, b, preferred_element_type=F32)


def _dot_nt(a, b):
    return lax.dot_general(a, b, (((1,), (1,)), ((), ())), preferred_element_type=F32)


def _dot_tn(a, b):
    return lax.dot_general(a, b, (((0,), (0,)), ((), ())), preferred_element_type=F32)


def _split2(x):
    hi = x.astype(BF16)
    lo = (x - hi.astype(F32)).astype(BF16)
    return hi, lo


def _const_spec(shape):
    nd = len(shape)
    return pl.BlockSpec(shape, lambda *_: (0,) * nd, pipeline_mode=pl.Buffered(1))


def _ffn_kernel(x_ref, prew_ref, wg_ref, wu_ref, wd_ref, postw_ref, o_ref):
    x = x_ref[...]
    xn = _rms(x, prew_ref[...]).astype(BF16)
    g = _dot(xn, wg_ref[...])
    u = _dot(xn, wu_ref[...])
    a = (g * jax.nn.sigmoid(g) * u).astype(BF16)
    h = _dot(a, wd_ref[...])
    o_ref[...] = x + MACARON_WEIGHT * _rms(h, postw_ref[...])


def _ffn_block(x, pre_w, w_gate, w_up, w_down, post_w, tm=512):
    n, d = x.shape
    dff = w_gate.shape[1]
    return pl.pallas_call(
        _ffn_kernel,
        grid=(n // tm,),
        in_specs=[
            pl.BlockSpec((tm, d), lambda i: (i, 0)),
            _const_spec((1, d)),
            _const_spec((d, dff)),
            _const_spec((d, dff)),
            _const_spec((dff, d)),
            _const_spec((1, d)),
        ],
        out_specs=pl.BlockSpec((tm, d), lambda i: (i, 0)),
        out_shape=jax.ShapeDtypeStruct((n, d), F32),
        compiler_params=pltpu.CompilerParams(
            dimension_semantics=("arbitrary",), vmem_limit_bytes=VMEM_LIMIT),
        name="ffn_block",
    )(x, pre_w, w_gate, w_up, w_down, post_w)


def _inproj_kernel(x_ref, prew_ref, wdn_ref, wba_ref, wk_ref, wqt_ref, wvt_ref,
                   dn_ref, ba_ref, k_ref, qt_ref, vt_ref):
    bs = MB_BLOCK
    h = _rms(x_ref[...], prew_ref[...])
    h_hi, h_lo = _split2(h)
    dn_ref[...] = _dot(h_hi, wdn_ref[...])
    ba_ref[...] = _dot(h_hi, wba_ref[0]) + _dot(h_lo, wba_ref[0]) + _dot(h_hi, wba_ref[1])
    k_ref[...] = _dot(h_hi, wk_ref[0]) + _dot(h_lo, wk_ref[0]) + _dot(h_hi, wk_ref[1])
    qt = _dot_nt(wqt_ref[0], h_hi) + _dot_nt(wqt_ref[0], h_lo) + _dot_nt(wqt_ref[1], h_hi)
    vt = _dot_nt(wvt_ref[...], h_hi).astype(BF16)
    for i in range(qt_ref.shape[0]):
        qt_ref[i] = qt[:, i * bs:(i + 1) * bs]
        vt_ref[i] = vt[:, i * bs:(i + 1) * bs]


def _in_proj(x, pre_w, w_dn, w_ba, w_k, w_qt, w_vt, tm=512):
    n, d = x.shape
    bs = MB_BLOCK
    return pl.pallas_call(
        _inproj_kernel,
        grid=(n // tm,),
        in_specs=[
            pl.BlockSpec((tm, d), lambda i: (i, 0)),
            _const_spec((1, d)),
            _const_spec(w_dn.shape),
            _const_spec(w_ba.shape),
            _const_spec(w_k.shape),
            _const_spec(w_qt.shape),
            _const_spec(w_vt.shape),
        ],
        out_specs=[
            pl.BlockSpec((tm, w_dn.shape[1]), lambda i: (i, 0)),
            pl.BlockSpec((tm, LANES), lambda i: (i, 0)),
            pl.BlockSpec((tm, MB_WIDTH), lambda i: (i, 0)),
            pl.BlockSpec((tm // bs, MB_WIDTH, bs), lambda i: (i, 0, 0)),
            pl.BlockSpec((tm // bs, MB_WIDTH, bs), lambda i: (i, 0, 0)),
        ],
        out_shape=[
            jax.ShapeDtypeStruct((n, w_dn.shape[1]), F32),
            jax.ShapeDtypeStruct((n, LANES), F32),
            jax.ShapeDtypeStruct((n, MB_WIDTH), F32),
            jax.ShapeDtypeStruct((n // bs, MB_WIDTH, bs), F32),
            jax.ShapeDtypeStruct((n // bs, MB_WIDTH, bs), BF16),
        ],
        compiler_params=pltpu.CompilerParams(
            dimension_semantics=("arbitrary",), vmem_limit_bytes=VMEM_LIMIT),
        name="in_proj",
    )(x, pre_w, w_dn, w_ba, w_k, w_qt, w_vt)


def _dn_kernel(qkv_ref, ba_ref, convw_ref, alog_ref, dtb_ref, normw_ref, o_ref,
               xbuf_ref, state_ref):
    tt = DN_TILE
    c = DN_CHUNK
    dk = DN_HEAD_DIM
    t = pl.program_id(1)

    @pl.when(t == 0)
    def _():
        xbuf_ref[0:8, :] = jnp.zeros((8, 3 * DN_WIDTH), F32)
        state_ref[...] = jnp.zeros_like(state_ref)

    x = qkv_ref[0]
    xbuf_ref[8:8 + tt, :] = x
    cw = convw_ref[...]
    y = x * cw[DN_CONV - 1:DN_CONV, :]
    for s in range(1, DN_CONV):
        y = y + xbuf_ref[8 - s:8 - s + tt, :] * cw[DN_CONV - 1 - s:DN_CONV - s, :]
    xbuf_ref[0:8, :] = x[tt - 8:tt, :]
    y = y * jax.nn.sigmoid(y)

    ba = ba_ref[0]
    beta_all = jax.nn.sigmoid(ba)
    g_all = -jnp.exp(alog_ref[...]) * jax.nn.softplus(ba + dtb_ref[...])

    ri = lax.broadcasted_iota(jnp.int32, (tt, tt), 0)
    ci = lax.broadcasted_iota(jnp.int32, (tt, tt), 1)
    same_chunk = (ri // c) == (ci // c)
    incl = same_chunk & (ri >= ci)
    eye = ri == ci
    tril = jnp.where(incl, 1.0, 0.0).astype(BF16)
    ones_bd = jnp.where(same_chunk, 1.0, 0.0).astype(BF16)

    g1 = g_all.astype(BF16)
    r1 = g_all - g1.astype(F32)
    g2 = r1.astype(BF16)
    g3 = (r1 - g2.astype(F32)).astype(BF16)
    gcs_all = _dot(tril, g1) + _dot(tril, g2) + _dot(tril, g3)
    gtot_all = _dot(ones_bd, g1) + _dot(ones_bd, g2) + _dot(ones_bd, g3)

    heads = range(DN_HEADS)
    lmat, attn16, rhs, qd, kd, gtot = [], [], [], [], [], []
    for h in heads:
        qr = y[:, h * dk:(h + 1) * dk]
        kr = y[:, DN_WIDTH + h * dk:DN_WIDTH + (h + 1) * dk]
        v = y[:, 2 * DN_WIDTH + h * dk:2 * DN_WIDTH + (h + 1) * dk]
        q = qr * lax.rsqrt(jnp.sum(qr * qr, axis=-1, keepdims=True) + NORM_EPS) * (dk ** -0.5)
        k = kr * lax.rsqrt(jnp.sum(kr * kr, axis=-1, keepdims=True) + NORM_EPS)
        beta = beta_all[:, h:h + 1]
        gcs = gcs_all[:, DN_HEADS + h:DN_HEADS + h + 1]
        gtot.append(gtot_all[:, DN_HEADS + h:DN_HEADS + h + 1])
        eg = jnp.exp(gcs)

        g_row = jnp.sum(jnp.where(eye, gcs, 0.0), axis=0, keepdims=True)
        decay = jnp.exp(jnp.where(incl, gcs - g_row, NEG_BIG))

        kb = k * beta
        k16 = k.astype(BF16)
        lmat.append(jnp.where(eye, 0.0, _dot_nt(kb.astype(BF16), k16) * decay))
        attn16.append((_dot_nt(q.astype(BF16), k16) * decay).astype(BF16))
        rhs.append(jnp.concatenate([v * beta, kb * eg], axis=1).astype(BF16))
        qd.append((q * eg).astype(BF16))
        kd.append((k * jnp.exp(gtot[h] - gcs)).astype(BF16))

    xinv = [jnp.where(eye, 1.0, -lmat[h]) for h in heads]
    m = [lmat[h].astype(BF16) for h in heads]
    power = 2
    while power < c:
        m = [_dot(m[h], m[h]).astype(BF16) for h in heads]
        xinv = [xinv[h] + _dot(xinv[h].astype(BF16), m[h]) for h in heads]
        power *= 2

    uw = [_dot(xinv[h].astype(BF16), rhs[h]) for h in heads]
    u = [uw[h][:, :dk] for h in heads]
    w16 = [uw[h][:, dk:].astype(BF16) for h in heads]

    s = [state_ref[h] for h in heads]
    outs = [[] for _ in heads]
    for ch in range(tt // c):
        lo, hi = ch * c, (ch + 1) * c
        for h in heads:
            s16 = s[h].astype(BF16)
            v_new = u[h][lo:hi] - _dot(w16[h][lo:hi], s16)
            vn16 = v_new.astype(BF16)
            outs[h].append(_dot(qd[h][lo:hi], s16) + _dot(attn16[h][lo:hi, lo:hi], vn16))
            s[h] = s[h] * jnp.exp(gtot[h][lo:lo + 1, :]) + _dot_tn(kd[h][lo:hi], vn16)
    for h in heads:
        state_ref[h] = s[h]
        o = jnp.concatenate(outs[h], axis=0)
        o_ref[0, :, h * dk:(h + 1) * dk] = _rms(o, normw_ref[...])


def _deltanet(qkv, ba, conv_w, alog_row, dtb_row, norm_w):
    bsz, t_len, width = qkv.shape
    tt = DN_TILE
    return pl.pallas_call(
        _dn_kernel,
        grid=(bsz, t_len // tt),
        in_specs=[
            pl.BlockSpec((1, tt, width), lambda b, t: (b, t, 0)),
            pl.BlockSpec((1, tt, LANES), lambda b, t: (b, t, 0)),
            _const_spec(conv_w.shape),
            _const_spec((1, LANES)),
            _const_spec((1, LANES)),
            _const_spec((1, DN_HEAD_DIM)),
        ],
        out_specs=pl.BlockSpec((1, tt, DN_WIDTH), lambda b, t: (b, t, 0)),
        out_shape=jax.ShapeDtypeStruct((bsz, t_len, DN_WIDTH), F32),
        scratch_shapes=[
            pltpu.VMEM((8 + tt, width), F32),
            pltpu.VMEM((DN_HEADS, DN_HEAD_DIM, DN_HEAD_DIM), F32),
        ],
        compiler_params=pltpu.CompilerParams(
            dimension_semantics=("arbitrary", "arbitrary"), vmem_limit_bytes=VMEM_LIMIT),
        name="deltanet",
    )(qkv, ba, conv_w, alog_row, dtb_row, norm_w)


SUBLANES = 8
MB_SUPER = 4
MB_AUX_MASK = 16
ALIBI_STEP = int(ALIBI_MAX_BIAS) // MB_HEADS
assert ALIBI_STEP * MB_HEADS == ALIBI_MAX_BIAS
LOG2E = math.log2(math.e)
LOG2E_PIECES = (1.4453125, -0.00262451171875, 7.063150405883789e-06, -1.05355866253376e-08)


def _moba_kernel(qt_ref, k_ref, vt_ref, o_ref, kaug_ref, kmean_ref, sa_ref, sb_ref, *, n_blk):
    bs = MB_BLOCK
    hd = MB_HEAD_DIM
    sup = MB_SUPER * bs
    nbp = -(-n_blk // SUBLANES) * SUBLANES
    p = pl.program_id(1)
    own = pl.program_id(2)
    lane = lax.broadcasted_iota(jnp.int32, (bs, LANES), 1)
    row = lax.broadcasted_iota(jnp.int32, (bs, LANES), 0)

    @pl.when(own == 0)
    def _():
        kmean_ref[...] = jnp.zeros_like(kmean_ref)

        def build(j, carry):
            off = pl.multiple_of(j * bs, bs)
            kblk = k_ref[0, pl.ds(off, bs), :]
            kmean_ref[pl.ds(j, 1), :] = jnp.mean(kblk, axis=0, keepdims=True)
            kaug_ref[pl.ds(off, bs), 0:LANES] = kblk.astype(BF16)
            kstart = jnp.full((bs, LANES), j * bs, jnp.int32).astype(F32)
            aux = jnp.where(lane < 2, 1.0,
                            jnp.where(lane < 6, row.astype(F32),
                                      jnp.where(lane < 10, kstart,
                                                jnp.where(lane == MB_AUX_MASK + j, 1.0, 0.0))))
            kaug_ref[pl.ds(off, bs), LANES:2 * LANES] = aux.astype(BF16)
            return carry

        lax.fori_loop(0, n_blk, build, 0)

    qt = qt_ref[0]
    km_hi, km_lo = _split2(kmean_ref[...])

    chan = lax.broadcasted_iota(jnp.int32, (LANES, bs), 0)
    blk = lax.broadcasted_iota(jnp.int32, (nbp, bs), 0)
    blk_f = blk.astype(F32)
    aux_row = lax.broadcasted_iota(jnp.int32, (MB_AUX_MASK, bs), 0)
    qpos = (lax.broadcasted_iota(jnp.int32, (MB_AUX_MASK, bs), 1) + own * bs).astype(F32)
    aux_pad = jnp.zeros((LANES - MB_AUX_MASK - nbp, bs), F32)

    qaug = []
    for hh in range(2):
        qth = jnp.where((chan >= hh * hd) & (chan < (hh + 1) * hd), qt, 0.0)
        slope_bits = (127 - ALIBI_STEP * (2 * p + hh + 1)) << 23
        slope = lax.bitcast_convert_type(jnp.full((MB_AUX_MASK, bs), slope_bits, jnp.int32), F32)

        q_hi, q_lo = _split2(qth)
        gate = (_dot(km_hi, q_hi) + _dot(km_hi, q_lo) + _dot(km_lo, q_hi))[:nbp]
        gate = jnp.where(blk < own, gate, -jnp.inf)
        sel = jnp.zeros((nbp, bs), F32)
        for _ in range(MB_TOPK):
            mx = jnp.max(gate, axis=0, keepdims=True)
            first = jnp.min(jnp.where(gate == mx, blk_f, float(nbp)), axis=0, keepdims=True)
            hit = blk_f == first
            sel = jnp.where(hit, 1.0, sel)
            gate = jnp.where(hit, -jnp.inf, gate)
        keep = jnp.where(blk < own, sel, jnp.where(blk == own, 1.0, 0.0))
        mask_rows = jnp.where(keep > 0.5, 0.0, NEG_BIG)
        qconst = -(slope * LOG2E) * qpos
        qconst_hi = qconst.astype(BF16).astype(F32)
        piece_id = (aux_row + 2) & 3
        piece = jnp.where(piece_id == 0, LOG2E_PIECES[0],
                          jnp.where(piece_id == 1, LOG2E_PIECES[1],
                                    jnp.where(piece_id == 2, LOG2E_PIECES[2], LOG2E_PIECES[3])))
        bias_rows = jnp.where(aux_row == 0, qconst_hi,
                              jnp.where(aux_row == 1, qconst - qconst_hi,
                                        jnp.where(aux_row < 10, slope * piece, 0.0)))
        qaug.append(jnp.concatenate([qth * (hd ** -0.5 * LOG2E), bias_rows, mask_rows, aux_pad],
                                    axis=0).astype(BF16))

    def keys(i):
        return kaug_ref[pl.ds(pl.multiple_of(i * sup, sup), sup), :]

    def values_t(i, hh):
        return jnp.concatenate([vt_ref[i * MB_SUPER + u, hh * hd:(hh + 1) * hd, :]
                                for u in range(MB_SUPER)], axis=1)

    n_grp = n_blk // MB_SUPER
    grp = own // MB_SUPER
    qaug2 = jnp.concatenate(qaug, axis=1)

    def group_at(t):
        g = jnp.where(t == 0, grp, jnp.where(t > grp, grp + 1, t - 1))
        return jnp.minimum(g, n_grp - 1)

    def scores(g):
        return _dot(keys(g), qaug2)

    def softmax_step(s_ref, g, carry):
        hs = range(2)
        m_i = [carry[3 * hh] for hh in hs]
        s = [s_ref[:, hh * bs:(hh + 1) * bs] for hh in hs]
        m_new = [jnp.maximum(m_i[hh], jnp.max(s[hh], axis=0, keepdims=True)) for hh in hs]
        alpha = [jnp.exp2(m_i[hh] - m_new[hh]) for hh in hs]
        pexp = [jnp.exp2(s[hh] - m_new[hh]) for hh in hs]
        l_new = [carry[3 * hh + 1] * alpha[hh] + jnp.sum(pexp[hh], axis=0, keepdims=True)
                 for hh in hs]
        acc_new = [carry[3 * hh + 2] * alpha[hh] + _dot(values_t(g, hh), pexp[hh].astype(BF16))
                   for hh in hs]
        return (m_new[0], l_new[0], acc_new[0], m_new[1], l_new[1], acc_new[1])

    rel = (lax.broadcasted_iota(jnp.int32, (sup, 2 * bs), 0) - (own - grp * MB_SUPER) * bs)
    qi = lax.broadcasted_iota(jnp.int32, (sup, 2 * bs), 1) & (bs - 1)
    future = (rel > qi) & (rel < bs)
    sa_ref[...] = jnp.where(future, NEG_BIG, scores(grp))

    def pair(u, carry):
        t = 2 * u
        sb_ref[...] = scores(group_at(t + 1))
        carry = softmax_step(sa_ref, group_at(t), carry)
        sa_ref[...] = scores(group_at(t + 2))
        return softmax_step(sb_ref, group_at(t + 1), carry)

    stat0 = jnp.full((1, bs), -jnp.inf, F32)
    zero = jnp.zeros((1, bs), F32)
    acc0 = jnp.zeros((hd, bs), F32)
    fin = lax.fori_loop(0, grp // 2 + 1, pair, (stat0, zero, acc0, stat0, zero, acc0))
    out_t = jnp.concatenate([fin[2] / fin[1], fin[5] / fin[4]], axis=0)
    o_ref[0] = out_t.T


def _moba(qt, k, vt):
    bsz, t_len, _ = k.shape
    bs = MB_BLOCK
    n_blk = t_len // bs
    assert n_blk % (2 * MB_SUPER) == 0 and MB_AUX_MASK + n_blk <= LANES
    return pl.pallas_call(
        functools.partial(_moba_kernel, n_blk=n_blk),
        grid=(bsz, MB_PAIRS, n_blk),
        in_specs=[
            pl.BlockSpec((1, LANES, bs), lambda b, p, i: (b * n_blk + i, p, 0)),
            pl.BlockSpec((1, t_len, LANES), lambda b, p, i: (b, 0, p)),
            pl.BlockSpec((n_blk, LANES, bs), lambda b, p, i: (b, p, 0)),
        ],
        out_specs=pl.BlockSpec((1, bs, LANES), lambda b, p, i: (b, i, p)),
        out_shape=jax.ShapeDtypeStruct((bsz, t_len, MB_WIDTH), F32),
        scratch_shapes=[
            pltpu.VMEM((t_len, 2 * LANES), BF16),
            pltpu.VMEM((LANES, LANES), F32),
            pltpu.VMEM((MB_SUPER * bs, 2 * bs), F32),
            pltpu.VMEM((MB_SUPER * bs, 2 * bs), F32),
        ],
        compiler_params=pltpu.CompilerParams(
            dimension_semantics=("arbitrary", "arbitrary", "arbitrary"),
            vmem_limit_bytes=VMEM_LIMIT),
        name="moba",
    )(qt, k, vt)


def _mixout_kernel(x_ref, odn_ref, omb_ref, prew_ref, wz_ref, wgd_ref, wgm_ref,
                   wbd_ref, wbm_ref, wo_ref, postw_ref, o_ref):
    x = x_ref[...]
    h = _rms(x, prew_ref[...]).astype(BF16)
    z = _dot(h, wz_ref[...])
    gate_dn = jax.nn.sigmoid(_dot(h, wgd_ref[...]))
    gate_mb = jax.nn.sigmoid(_dot(h, wgm_ref[...]))
    o_dn = odn_ref[...] * (z * jax.nn.sigmoid(z))
    y_dn = _dot(o_dn.astype(BF16), wbd_ref[...])
    y_mb = _dot(omb_ref[...].astype(BF16), wbm_ref[...])
    merged = gate_dn * y_dn + gate_mb * y_mb
    y = _dot(merged.astype(BF16), wo_ref[...])
    o_ref[...] = x + _rms(y, postw_ref[...])


def _mix_out(x, o_dn, o_mb, pre_w, w_z, w_gd, w_gm, w_bd, w_bm, w_o, post_w, tm=512):
    n, d = x.shape
    return pl.pallas_call(
        _mixout_kernel,
        grid=(n // tm,),
        in_specs=[
            pl.BlockSpec((tm, d), lambda i: (i, 0)),
            pl.BlockSpec((tm, o_dn.shape[1]), lambda i: (i, 0)),
            pl.BlockSpec((tm, o_mb.shape[1]), lambda i: (i, 0)),
            _const_spec((1, d)),
            _const_spec(w_z.shape),
            _const_spec(w_gd.shape),
            _const_spec(w_gm.shape),
            _const_spec(w_bd.shape),
            _const_spec(w_bm.shape),
            _const_spec(w_o.shape),
            _const_spec((1, d)),
        ],
        out_specs=pl.BlockSpec((tm, d), lambda i: (i, 0)),
        out_shape=jax.ShapeDtypeStruct((n, d), F32),
        compiler_params=pltpu.CompilerParams(
            dimension_semantics=("arbitrary",), vmem_limit_bytes=VMEM_LIMIT),
        name="mix_out",
    )(x, o_dn, o_mb, pre_w, w_z, w_gd, w_gm, w_bd, w_bm, w_o, post_w)


def _layer(x, ffn1_pre_w, ffn1_w_gate, ffn1_w_up, ffn1_w_down, ffn1_post_w,
           mix_pre_w, w_in, dn_conv_w, dn_a_log, dn_dt_bias, dn_norm_w,
           w_branch_dn, w_branch_mb, w_out, mix_post_w,
           ffn2_pre_w, ffn2_w_gate, ffn2_w_up, ffn2_w_down, ffn2_post_w):
    bsz, t_len, d = x.shape
    n = bsz * t_len
    row = lambda w: w.reshape(1, -1).astype(F32)
    b16 = lambda w: w.astype(BF16)

    x = x.reshape(n, d)
    x = _ffn_block(x, row(ffn1_pre_w), b16(ffn1_w_gate), b16(ffn1_w_up), b16(ffn1_w_down),
                   row(ffn1_post_w))

    o = 0
    w_dn = w_in[:, o:o + 3 * DN_WIDTH]; o += 3 * DN_WIDTH
    w_z = w_in[:, o:o + DN_WIDTH]; o += DN_WIDTH
    w_ba = w_in[:, o:o + 2 * DN_HEADS]; o += 2 * DN_HEADS
    w_q = w_in[:, o:o + MB_WIDTH]; o += MB_WIDTH
    w_k = w_in[:, o:o + MB_WIDTH]; o += MB_WIDTH
    w_v = w_in[:, o:o + MB_WIDTH]; o += MB_WIDTH
    w_gd = w_in[:, o:o + d]; o += d
    w_gm = w_in[:, o:o + d]; o += d
    w_ba = jnp.pad(w_ba, ((0, 0), (0, LANES - 2 * DN_HEADS)))

    def hi_lo(w):
        hi = w.astype(BF16)
        return jnp.stack([hi, (w - hi.astype(F32)).astype(BF16)])

    dn_qkv, ba, mb_k, mb_qt, mb_vt = _in_proj(x, row(mix_pre_w), b16(w_dn), hi_lo(w_ba), hi_lo(w_k),
                                              hi_lo(w_q.T), b16(w_v.T))

    pad_heads = lambda p: jnp.pad(p.astype(F32), (DN_HEADS, LANES - 2 * DN_HEADS)).reshape(1, LANES)
    o_dn = _deltanet(dn_qkv.reshape(bsz, t_len, -1), ba.reshape(bsz, t_len, LANES),
                     dn_conv_w.astype(F32), pad_heads(dn_a_log), pad_heads(dn_dt_bias),
                     row(dn_norm_w))

    o_mb = _moba(mb_qt, mb_k.reshape(bsz, t_len, MB_WIDTH), mb_vt)

    x = _mix_out(x, o_dn.reshape(n, DN_WIDTH), o_mb.reshape(n, MB_WIDTH), row(mix_pre_w),
                 b16(w_z), b16(w_gd), b16(w_gm), b16(w_branch_dn), b16(w_branch_mb), b16(w_out),
                 row(mix_post_w))

    x = _ffn_block(x, row(ffn2_pre_w), b16(ffn2_w_gate), b16(ffn2_w_up), b16(ffn2_w_down),
                   row(ffn2_post_w))
    return x.reshape(bsz, t_len, d)


def kernel(x, ffn1_pre_w, ffn1_w_gate, ffn1_w_up, ffn1_w_down, ffn1_post_w, mix_pre_w, w_in, dn_conv_w, dn_a_log, dn_dt_bias, dn_norm_w, w_branch_dn, w_branch_mb, w_out, mix_post_w, ffn2_pre_w, ffn2_w_gate, ffn2_w_up, ffn2_w_down, ffn2_post_w):
    depth = w_in.shape[0]
    for l in range(depth):
        x = _layer(x, ffn1_pre_w[l], ffn1_w_gate[l], ffn1_w_up[l], ffn1_w_down[l], ffn1_post_w[l],
                   mix_pre_w[l], w_in[l], dn_conv_w[l], dn_a_log[l], dn_dt_bias[l], dn_norm_w[l],
                   w_branch_dn[l], w_branch_mb[l], w_out[l], mix_post_w[l],
                   ffn2_pre_w[l], ffn2_w_gate[l], ffn2_w_up[l], ffn2_w_down[l], ffn2_post_w[l])
    return x
```

```python
import functools
import math

import jax
import jax.numpy as jnp
from jax import lax
from jax.experimental import pallas as pl
from jax.experimental.pallas import tpu as pltpu

F32 = jnp.float32
BF16 = jnp.bfloat16

NORM_EPS = 1e-6
MACARON_WEIGHT = 0.5

DN_HEADS = 4
DN_HEAD_DIM = 128
DN_WIDTH = DN_HEADS * DN_HEAD_DIM
DN_CONV = 4
DN_CHUNK = 64
DN_TILE = 256

MB_HEADS = 8
MB_HEAD_DIM = 64
MB_WIDTH = MB_HEADS * MB_HEAD_DIM
MB_BLOCK = 256
MB_TOPK = 3
ALIBI_MAX_BIAS = 8.0
LANES = 128
MB_PAIRS = MB_WIDTH // LANES
NEG_BIG = -1e30

VMEM_LIMIT = 56 * 1024 * 1024


def _rms(x, w):
    ms = jnp.mean(x * x, axis=-1, keepdims=True)
    return x * lax.rsqrt(ms + NORM_EPS) * w


def _dot(a, b):
    return jnp.dot(a, b, preferred_element_type=F32)


def _dot_nt(a, b):
    return lax.dot_general(a, b, (((1,), (1,)), ((), ())), preferred_element_type=F32)


def _dot_tn(a, b):
    return lax.dot_general(a, b, (((0,), (0,)), ((), ())), preferred_element_type=F32)


def _split2(x):
    hi = x.astype(BF16)
    lo = (x - hi.astype(F32)).astype(BF16)
    return hi, lo


def _const_spec(shape):
    nd = len(shape)
    return pl.BlockSpec(shape, lambda *_: (0,) * nd, pipeline_mode=pl.Buffered(1))


def _ffn_kernel(x_ref, prew_ref, wg_ref, wu_ref, wd_ref, postw_ref, o_ref):
    x = x_ref[...]
    xn = _rms(x, prew_ref[...]).astype(BF16)
    g = _dot(xn, wg_ref[...])
    u = _dot(xn, wu_ref[...])
    a = (g * jax.nn.sigmoid(g) * u).astype(BF16)
    h = _dot(a, wd_ref[...])
    o_ref[...] = x + MACARON_WEIGHT * _rms(h, postw_ref[...])


def _ffn_block(x, pre_w, w_gate, w_up, w_down, post_w, tm=512):
    n, d = x.shape
    dff = w_gate.shape[1]
    return pl.pallas_call(
        _ffn_kernel,
        grid=(n // tm,),
        in_specs=[
            pl.BlockSpec((tm, d), lambda i: (i, 0)),
            _const_spec((1, d)),
            _const_spec((d, dff)),
            _const_spec((d, dff)),
            _const_spec((dff, d)),
            _const_spec((1, d)),
        ],
        out_specs=pl.BlockSpec((tm, d), lambda i: (i, 0)),
        out_shape=jax.ShapeDtypeStruct((n, d), F32),
        compiler_params=pltpu.CompilerParams(
            dimension_semantics=("arbitrary",), vmem_limit_bytes=VMEM_LIMIT),
        name="ffn_block",
    )(x, pre_w, w_gate, w_up, w_down, post_w)


def _inproj_kernel(x_ref, prew_ref, wdn_ref, wba_ref, wk_ref, wqt_ref, wvt_ref,
                   dn_ref, ba_ref, k_ref, qt_ref, vt_ref):
    bs = MB_BLOCK
    h = _rms(x_ref[...], prew_ref[...])
    h_hi, h_lo = _split2(h)
    dn_ref[...] = _dot(h_hi, wdn_ref[...])
    ba_ref[...] = _dot(h_hi, wba_ref[0]) + _dot(h_lo, wba_ref[0]) + _dot(h_hi, wba_ref[1])
    k_ref[...] = _dot(h_hi, wk_ref[0]) + _dot(h_lo, wk_ref[0]) + _dot(h_hi, wk_ref[1])
    qt = _dot_nt(wqt_ref[0], h_hi) + _dot_nt(wqt_ref[0], h_lo) + _dot_nt(wqt_ref[1], h_hi)
    vt = _dot_nt(wvt_ref[...], h_hi).astype(BF16)
    for i in range(qt_ref.shape[0]):
        qt_ref[i] = qt[:, i * bs:(i + 1) * bs]
        vt_ref[i] = vt[:, i * bs:(i + 1) * bs]


def _in_proj(x, pre_w, w_dn, w_ba, w_k, w_qt, w_vt, tm=512):
    n, d = x.shape
    bs = MB_BLOCK
    return pl.pallas_call(
        _inproj_kernel,
        grid=(n // tm,),
        in_specs=[
            pl.BlockSpec((tm, d), lambda i: (i, 0)),
            _const_spec((1, d)),
            _const_spec(w_dn.shape),
            _const_spec(w_ba.shape),
            _const_spec(w_k.shape),
            _const_spec(w_qt.shape),
            _const_spec(w_vt.shape),
        ],
        out_specs=[
            pl.BlockSpec((tm, w_dn.shape[1]), lambda i: (i, 0)),
            pl.BlockSpec((tm, LANES), lambda i: (i, 0)),
            pl.BlockSpec((tm, MB_WIDTH), lambda i: (i, 0)),
            pl.BlockSpec((tm // bs, MB_WIDTH, bs), lambda i: (i, 0, 0)),
            pl.BlockSpec((tm // bs, MB_WIDTH, bs), lambda i: (i, 0, 0)),
        ],
        out_shape=[
            jax.ShapeDtypeStruct((n, w_dn.shape[1]), F32),
            jax.ShapeDtypeStruct((n, LANES), F32),
            jax.ShapeDtypeStruct((n, MB_WIDTH), F32),
            jax.ShapeDtypeStruct((n // bs, MB_WIDTH, bs), F32),
            jax.ShapeDtypeStruct((n // bs, MB_WIDTH, bs), BF16),
        ],
        compiler_params=pltpu.CompilerParams(
            dimension_semantics=("arbitrary",), vmem_limit_bytes=VMEM_LIMIT),
        name="in_proj",
    )(x, pre_w, w_dn, w_ba, w_k, w_qt, w_vt)


def _dn_kernel(qkv_ref, ba_ref, convw_ref, alog_ref, dtb_ref, normw_ref, o_ref,
               xbuf_ref, state_ref):
    tt = DN_TILE
    c = DN_CHUNK
    dk = DN_HEAD_DIM
    t = pl.program_id(1)

    @pl.when(t == 0)
    def _():
        xbuf_ref[0:8, :] = jnp.zeros((8, 3 * DN_WIDTH), F32)
        state_ref[...] = jnp.zeros_like(state_ref)

    x = qkv_ref[0]
    xbuf_ref[8:8 + tt, :] = x
    cw = convw_ref[...]
    y = x * cw[DN_CONV - 1:DN_CONV, :]
    for s in range(1, DN_CONV):
        y = y + xbuf_ref[8 - s:8 - s + tt, :] * cw[DN_CONV - 1 - s:DN_CONV - s, :]
    xbuf_ref[0:8, :] = x[tt - 8:tt, :]
    y = y * jax.nn.sigmoid(y)

    ba = ba_ref[0]
    beta_all = jax.nn.sigmoid(ba)
    g_all = -jnp.exp(alog_ref[...]) * jax.nn.softplus(ba + dtb_ref[...])

    ri = lax.broadcasted_iota(jnp.int32, (tt, tt), 0)
    ci = lax.broadcasted_iota(jnp.int32, (tt, tt), 1)
    same_chunk = (ri // c) == (ci // c)
    incl = same_chunk & (ri >= ci)
    eye = ri == ci
    tril = jnp.where(incl, 1.0, 0.0).astype(BF16)
    ones_bd = jnp.where(same_chunk, 1.0, 0.0).astype(BF16)

    g1 = g_all.astype(BF16)
    r1 = g_all - g1.astype(F32)
    g2 = r1.astype(BF16)
    g3 = (r1 - g2.astype(F32)).astype(BF16)
    gcs_all = _dot(tril, g1) + _dot(tril, g2) + _dot(tril, g3)
    gtot_all = _dot(ones_bd, g1) + _dot(ones_bd, g2) + _dot(ones_bd, g3)

    heads = range(DN_HEADS)
    lmat, attn16, rhs, qd, kd, gtot = [], [], [], [], [], []
    for h in heads:
        qr = y[:, h * dk:(h + 1) * dk]
        kr = y[:, DN_WIDTH + h * dk:DN_WIDTH + (h + 1) * dk]
        v = y[:, 2 * DN_WIDTH + h * dk:2 * DN_WIDTH + (h + 1) * dk]
        q = qr * lax.rsqrt(jnp.sum(qr * qr, axis=-1, keepdims=True) + NORM_EPS) * (dk ** -0.5)
        k = kr * lax.rsqrt(jnp.sum(kr * kr, axis=-1, keepdims=True) + NORM_EPS)
        beta = beta_all[:, h:h + 1]
        gcs = gcs_all[:, DN_HEADS + h:DN_HEADS + h + 1]
        gtot.append(gtot_all[:, DN_HEADS + h:DN_HEADS + h + 1])
        eg = jnp.exp(gcs)

        g_row = jnp.sum(jnp.where(eye, gcs, 0.0), axis=0, keepdims=True)
        decay = jnp.exp(jnp.where(incl, gcs - g_row, NEG_BIG))

        kb = k * beta
        k16 = k.astype(BF16)
        lmat.append(jnp.where(eye, 0.0, _dot_nt(kb.astype(BF16), k16) * decay))
        attn16.append((_dot_nt(q.astype(BF16), k16) * decay).astype(BF16))
        rhs.append(jnp.concatenate([v * beta, kb * eg], axis=1).astype(BF16))
        qd.append((q * eg).astype(BF16))
        kd.append((k * jnp.exp(gtot[h] - gcs)).astype(BF16))

    xinv = [jnp.where(eye, 1.0, -lmat[h]) for h in heads]
    m = [lmat[h].astype(BF16) for h in heads]
    power = 2
    while power < c:
        m = [_dot(m[h], m[h]).astype(BF16) for h in heads]
        xinv = [xinv[h] + _dot(xinv[h].astype(BF16), m[h]) for h in heads]
        power *= 2

    uw = [_dot(xinv[h].astype(BF16), rhs[h]) for h in heads]
    u = [uw[h][:, :dk] for h in heads]
    w16 = [uw[h][:, dk:].astype(BF16) for h in heads]

    s = [state_ref[h] for h in heads]
    outs = [[] for _ in heads]
    for ch in range(tt // c):
        lo, hi = ch * c, (ch + 1) * c
        for h in heads:
            s16 = s[h].astype(BF16)
            v_new = u[h][lo:hi] - _dot(w16[h][lo:hi], s16)
            vn16 = v_new.astype(BF16)
            outs[h].append(_dot(qd[h][lo:hi], s16) + _dot(attn16[h][lo:hi, lo:hi], vn16))
            s[h] = s[h] * jnp.exp(gtot[h][lo:lo + 1, :]) + _dot_tn(kd[h][lo:hi], vn16)
    for h in heads:
        state_ref[h] = s[h]
        o = jnp.concatenate(outs[h], axis=0)
        o_ref[0, :, h * dk:(h + 1) * dk] = _rms(o, normw_ref[...])


def _deltanet(qkv, ba, conv_w, alog_row, dtb_row, norm_w):
    bsz, t_len, width = qkv.shape
    tt = DN_TILE
    return pl.pallas_call(
        _dn_kernel,
        grid=(bsz, t_len // tt),
        in_specs=[
            pl.BlockSpec((1, tt, width), lambda b, t: (b, t, 0)),
            pl.BlockSpec((1, tt, LANES), lambda b, t: (b, t, 0)),
            _const_spec(conv_w.shape),
            _const_spec((1, LANES)),
            _const_spec((1, LANES)),
            _const_spec((1, DN_HEAD_DIM)),
        ],
        out_specs=pl.BlockSpec((1, tt, DN_WIDTH), lambda b, t: (b, t, 0)),
        out_shape=jax.ShapeDtypeStruct((bsz, t_len, DN_WIDTH), F32),
        scratch_shapes=[
            pltpu.VMEM((8 + tt, width), F32),
            pltpu.VMEM((DN_HEADS, DN_HEAD_DIM, DN_HEAD_DIM), F32),
        ],
        compiler_params=pltpu.CompilerParams(
            dimension_semantics=("arbitrary", "arbitrary"), vmem_limit_bytes=VMEM_LIMIT),
        name="deltanet",
    )(qkv, ba, conv_w, alog_row, dtb_row, norm_w)


SUBLANES = 8
MB_SUPER = 4
MB_AUX_MASK = 16
MB_SUM_ROWS = 16
ALIBI_STEP = int(ALIBI_MAX_BIAS) // MB_HEADS
assert ALIBI_STEP * MB_HEADS == ALIBI_MAX_BIAS
LOG2E = math.log2(math.e)
LOG2E_PIECES = (1.4453125, -0.00262451171875, 7.063150405883789e-06, -1.05355866253376e-08)


def _moba_kernel(qt_ref, k_ref, vt_ref, o_ref, kaug_ref, kmean_ref, sa_ref, sb_ref, *, n_blk):
    bs = MB_BLOCK
    hd = MB_HEAD_DIM
    sup = MB_SUPER * bs
    nbp = -(-n_blk // SUBLANES) * SUBLANES
    p = pl.program_id(1)
    own = pl.program_id(2)
    lane = lax.broadcasted_iota(jnp.int32, (bs, LANES), 1)
    row = lax.broadcasted_iota(jnp.int32, (bs, LANES), 0)

    @pl.when(own == 0)
    def _():
        kmean_ref[...] = jnp.zeros_like(kmean_ref)

        def build(j, carry):
            off = pl.multiple_of(j * bs, bs)
            kblk = k_ref[0, pl.ds(off, bs), :]
            kmean_ref[pl.ds(j, 1), :] = jnp.mean(kblk, axis=0, keepdims=True)
            kaug_ref[pl.ds(off, bs), 0:LANES] = kblk.astype(BF16)
            kstart = jnp.full((bs, LANES), j * bs, jnp.int32).astype(F32)
            aux = jnp.where(lane < 2, 1.0,
                            jnp.where(lane < 6, row.astype(F32),
                                      jnp.where(lane < 10, kstart,
                                                jnp.where(lane == MB_AUX_MASK + j, 1.0, 0.0))))
            kaug_ref[pl.ds(off, bs), LANES:2 * LANES] = aux.astype(BF16)
            return carry

        lax.fori_loop(0, n_blk, build, 0)

    qt = qt_ref[0]
    km_hi, km_lo = _split2(kmean_ref[...])

    chan = lax.broadcasted_iota(jnp.int32, (LANES, bs), 0)
    blk = lax.broadcasted_iota(jnp.int32, (nbp, bs), 0)
    blk_f = blk.astype(F32)
    aux_row = lax.broadcasted_iota(jnp.int32, (MB_AUX_MASK, bs), 0)
    qpos = (lax.broadcasted_iota(jnp.int32, (MB_AUX_MASK, bs), 1) + own * bs).astype(F32)
    aux_pad = jnp.zeros((LANES - MB_AUX_MASK - nbp, bs), F32)

    qaug = []
    for hh in range(2):
        qth = jnp.where((chan >= hh * hd) & (chan < (hh + 1) * hd), qt, 0.0)
        slope_bits = (127 - ALIBI_STEP * (2 * p + hh + 1)) << 23
        slope = lax.bitcast_convert_type(jnp.full((MB_AUX_MASK, bs), slope_bits, jnp.int32), F32)

        q_hi, q_lo = _split2(qth)
        gate = (_dot(km_hi, q_hi) + _dot(km_hi, q_lo) + _dot(km_lo, q_hi))[:nbp]
        gate = jnp.where(blk < own, gate, -jnp.inf)
        sel = jnp.zeros((nbp, bs), F32)
        for _ in range(MB_TOPK):
            mx = jnp.max(gate, axis=0, keepdims=True)
            first = jnp.min(jnp.where(gate == mx, blk_f, float(nbp)), axis=0, keepdims=True)
            hit = blk_f == first
            sel = jnp.where(hit, 1.0, sel)
            gate = jnp.where(hit, -jnp.inf, gate)
        keep = jnp.where(blk < own, sel, jnp.where(blk == own, 1.0, 0.0))
        mask_rows = jnp.where(keep > 0.5, 0.0, NEG_BIG)
        qconst = -(slope * LOG2E) * qpos
        qconst_hi = qconst.astype(BF16).astype(F32)
        piece_id = (aux_row + 2) & 3
        piece = jnp.where(piece_id == 0, LOG2E_PIECES[0],
                          jnp.where(piece_id == 1, LOG2E_PIECES[1],
                                    jnp.where(piece_id == 2, LOG2E_PIECES[2], LOG2E_PIECES[3])))
        bias_rows = jnp.where(aux_row == 0, qconst_hi,
                              jnp.where(aux_row == 1, qconst - qconst_hi,
                                        jnp.where(aux_row < 10, slope * piece, 0.0)))
        qaug.append(jnp.concatenate([qth * (hd ** -0.5 * LOG2E), bias_rows, mask_rows, aux_pad],
                                    axis=0).astype(BF16))

    def keys(i):
        return kaug_ref[pl.ds(pl.multiple_of(i * sup, sup), sup), :]

    def values_t(i, hh):
        return jnp.concatenate([vt_ref[i * MB_SUPER + u, hh * hd:(hh + 1) * hd, :]
                                for u in range(MB_SUPER)], axis=1)

    n_grp = n_blk // MB_SUPER
    grp = own // MB_SUPER
    qaug2 = jnp.concatenate(qaug, axis=1)

    def group_at(t):
        g = jnp.where(t == 0, grp, jnp.where(t > grp, grp + 1, t - 1))
        return jnp.minimum(g, n_grp - 1)

    def scores(g):
        return _dot(keys(g), qaug2)

    ones_rows = jnp.ones((MB_SUM_ROWS, sup), BF16)

    def produce(s_ref, s2):
        s_ref[...] = s2
        return jnp.max(s2, axis=0, keepdims=True)

    def softmax_step(s_ref, smax, g, carry):
        hs = range(2)
        m_i = [carry[2 * hh] for hh in hs]
        m_new = [jnp.maximum(m_i[hh], smax[:, hh * bs:(hh + 1) * bs]) for hh in hs]
        alpha = [jnp.exp2(m_i[hh] - m_new[hh]) for hh in hs]
        pexp = [jnp.exp2((s_ref[:, hh * bs:(hh + 1) * bs] - m_new[hh]).astype(BF16)) for hh in hs]
        acc_new = [carry[2 * hh + 1] * alpha[hh]
                   + _dot(jnp.concatenate([values_t(g, hh), ones_rows], axis=0), pexp[hh])
                   for hh in hs]
        return (m_new[0], acc_new[0], m_new[1], acc_new[1])

    rel = (lax.broadcasted_iota(jnp.int32, (sup, 2 * bs), 0) - (own - grp * MB_SUPER) * bs)
    qi = lax.broadcasted_iota(jnp.int32, (sup, 2 * bs), 1) & (bs - 1)
    future = (rel > qi) & (rel < bs)
    smax_a0 = produce(sa_ref, jnp.where(future, NEG_BIG, scores(grp)))

    def pair(u, carry):
        t = 2 * u
        smax_a, carry = carry[0], carry[1:]
        smax_b = produce(sb_ref, scores(group_at(t + 1)))
        carry = softmax_step(sa_ref, smax_a, group_at(t), carry)
        smax_a = produce(sa_ref, scores(group_at(t + 2)))
        return (smax_a,) + softmax_step(sb_ref, smax_b, group_at(t + 1), carry)

    stat0 = jnp.full((1, bs), -jnp.inf, F32)
    acc0 = jnp.zeros((hd + MB_SUM_ROWS, bs), F32)
    fin = lax.fori_loop(0, grp // 2 + 1, pair, (smax_a0, stat0, acc0, stat0, acc0))
    out_t = jnp.concatenate([fin[2][:hd] / fin[2][hd:hd + 1], fin[4][:hd] / fin[4][hd:hd + 1]],
                            axis=0)
    o_ref[0] = out_t.T


def _moba(qt, k, vt):
    bsz, t_len, _ = k.shape
    bs = MB_BLOCK
    n_blk = t_len // bs
    assert n_blk % (2 * MB_SUPER) == 0 and MB_AUX_MASK + n_blk <= LANES
    return pl.pallas_call(
        functools.partial(_moba_kernel, n_blk=n_blk),
        grid=(bsz, MB_PAIRS, n_blk),
        in_specs=[
            pl.BlockSpec((1, LANES, bs), lambda b, p, i: (b * n_blk + i, p, 0)),
            pl.BlockSpec((1, t_len, LANES), lambda b, p, i: (b, 0, p)),
            pl.BlockSpec((n_blk, LANES, bs), lambda b, p, i: (b, p, 0)),
        ],
        out_specs=pl.BlockSpec((1, bs, LANES), lambda b, p, i: (b, i, p)),
        out_shape=jax.ShapeDtypeStruct((bsz, t_len, MB_WIDTH), F32),
        scratch_shapes=[
            pltpu.VMEM((t_len, 2 * LANES), BF16),
            pltpu.VMEM((LANES, LANES), F32),
            pltpu.VMEM((MB_SUPER * bs, 2 * bs), F32),
            pltpu.VMEM((MB_SUPER * bs, 2 * bs), F32),
        ],
        compiler_params=pltpu.CompilerParams(
            dimension_semantics=("arbitrary", "arbitrary", "arbitrary"),
            vmem_limit_bytes=VMEM_LIMIT),
        name="moba",
    )(qt, k, vt)


def _mixout_kernel(x_ref, odn_ref, omb_ref, prew_ref, wz_ref, wgd_ref, wgm_ref,
                   wbd_ref, wbm_ref, wo_ref, postw_ref, o_ref):
    x = x_ref[...]
    h = _rms(x, prew_ref[...]).astype(BF16)
    z = _dot(h, wz_ref[...])
    gate_dn = jax.nn.sigmoid(_dot(h, wgd_ref[...]))
    gate_mb = jax.nn.sigmoid(_dot(h, wgm_ref[...]))
    o_dn = odn_ref[...] * (z * jax.nn.sigmoid(z))
    y_dn = _dot(o_dn.astype(BF16), wbd_ref[...])
    y_mb = _dot(omb_ref[...].astype(BF16), wbm_ref[...])
    merged = gate_dn * y_dn + gate_mb * y_mb
    y = _dot(merged.astype(BF16), wo_ref[...])
    o_ref[...] = x + _rms(y, postw_ref[...])


def _mix_out(x, o_dn, o_mb, pre_w, w_z, w_gd, w_gm, w_bd, w_bm, w_o, post_w, tm=512):
    n, d = x.shape
    return pl.pallas_call(
        _mixout_kernel,
        grid=(n // tm,),
        in_specs=[
            pl.BlockSpec((tm, d), lambda i: (i, 0)),
            pl.BlockSpec((tm, o_dn.shape[1]), lambda i: (i, 0)),
            pl.BlockSpec((tm, o_mb.shape[1]), lambda i: (i, 0)),
            _const_spec((1, d)),
            _const_spec(w_z.shape),
            _const_spec(w_gd.shape),
            _const_spec(w_gm.shape),
            _const_spec(w_bd.shape),
            _const_spec(w_bm.shape),
            _const_spec(w_o.shape),
            _const_spec((1, d)),
        ],
        out_specs=pl.BlockSpec((tm, d), lambda i: (i, 0)),
        out_shape=jax.ShapeDtypeStruct((n, d), F32),
        compiler_params=pltpu.CompilerParams(
            dimension_semantics=("arbitrary",), vmem_limit_bytes=VMEM_LIMIT),
        name="mix_out",
    )(x, o_dn, o_mb, pre_w, w_z, w_gd, w_gm, w_bd, w_bm, w_o, post_w)


def _layer(x, ffn1_pre_w, ffn1_w_gate, ffn1_w_up, ffn1_w_down, ffn1_post_w,
           mix_pre_w, w_in, dn_conv_w, dn_a_log, dn_dt_bias, dn_norm_w,
           w_branch_dn, w_branch_mb, w_out, mix_post_w,
           ffn2_pre_w, ffn2_w_gate, ffn2_w_up, ffn2_w_down, ffn2_post_w):
    bsz, t_len, d = x.shape
    n = bsz * t_len
    row = lambda w: w.reshape(1, -1).astype(F32)
    b16 = lambda w: w.astype(BF16)

    x = x.reshape(n, d)
    x = _ffn_block(x, row(ffn1_pre_w), b16(ffn1_w_gate), b16(ffn1_w_up), b16(ffn1_w_down),
                   row(ffn1_post_w))

    o = 0
    w_dn = w_in[:, o:o + 3 * DN_WIDTH]; o += 3 * DN_WIDTH
    w_z = w_in[:, o:o + DN_WIDTH]; o += DN_WIDTH
    w_ba = w_in[:, o:o + 2 * DN_HEADS]; o += 2 * DN_HEADS
    w_q = w_in[:, o:o + MB_WIDTH]; o += MB_WIDTH
    w_k = w_in[:, o:o + MB_WIDTH]; o += MB_WIDTH
    w_v = w_in[:, o:o + MB_WIDTH]; o += MB_WIDTH
    w_gd = w_in[:, o:o + d]; o += d
    w_gm = w_in[:, o:o + d]; o += d
    w_ba = jnp.pad(w_ba, ((0, 0), (0, LANES - 2 * DN_HEADS)))

    def hi_lo(w):
        hi = w.astype(BF16)
        return jnp.stack([hi, (w - hi.astype(F32)).astype(BF16)])

    dn_qkv, ba, mb_k, mb_qt, mb_vt = _in_proj(x, row(mix_pre_w), b16(w_dn), hi_lo(w_ba), hi_lo(w_k),
                                              hi_lo(w_q.T), b16(w_v.T))

    pad_heads = lambda p: jnp.pad(p.astype(F32), (DN_HEADS, LANES - 2 * DN_HEADS)).reshape(1, LANES)
    o_dn = _deltanet(dn_qkv.reshape(bsz, t_len, -1), ba.reshape(bsz, t_len, LANES),
                     dn_conv_w.astype(F32), pad_heads(dn_a_log), pad_heads(dn_dt_bias),
                     row(dn_norm_w))

    o_mb = _moba(mb_qt, mb_k.reshape(bsz, t_len, MB_WIDTH), mb_vt)

    x = _mix_out(x, o_dn.reshape(n, DN_WIDTH), o_mb.reshape(n, MB_WIDTH), row(mix_pre_w),
                 b16(w_z), b16(w_gd), b16(w_gm), b16(w_branch_dn), b16(w_branch_mb), b16(w_out),
                 row(mix_post_w))

    x = _ffn_block(x, row(ffn2_pre_w), b16(ffn2_w_gate), b16(ffn2_w_up), b16(ffn2_w_down),
                   row(ffn2_post_w))
    return x.reshape(bsz, t_len, d)


def kernel(x, ffn1_pre_w, ffn1_w_gate, ffn1_w_up, ffn1_w_down, ffn1_post_w, mix_pre_w, w_in, dn_conv_w, dn_a_log, dn_dt_bias, dn_norm_w, w_branch_dn, w_branch_mb, w_out, mix_post_w, ffn2_pre_w, ffn2_w_gate, ffn2_w_up, ffn2_w_down, ffn2_post_w):
    depth = w_in.shape[0]
    for l in range(depth):
        x = _layer(x, ffn1_pre_w[l], ffn1_w_gate[l], ffn1_w_up[l], ffn1_w_down[l], ffn1_post_w[l],
                   mix_pre_w[l], w_in[l], dn_conv_w[l], dn_a_log[l], dn_dt_bias[l], dn_norm_w[l],
                   w_branch_dn[l], w_branch_mb[l], w_out[l], mix_post_w[l],
                   ffn2_pre_w[l], ffn2_w_gate[l], ffn2_w_up[l], ffn2_w_down[l], ffn2_post_w[l])
    return x
```

```python
import functools
import math

import jax
import jax.numpy as jnp
from jax import lax
from jax.experimental import pallas as pl
from jax.experimental.pallas import tpu as pltpu

F32 = jnp.float32
BF16 = jnp.bfloat16

NORM_EPS = 1e-6
MACARON_WEIGHT = 0.5

DN_HEADS = 4
DN_HEAD_DIM = 128
DN_WIDTH = DN_HEADS * DN_HEAD_DIM
DN_CONV = 4
DN_CHUNK = 64
DN_TILE = 256

MB_HEADS = 8
MB_HEAD_DIM = 64
MB_WIDTH = MB_HEADS * MB_HEAD_DIM
MB_BLOCK = 256
MB_TOPK = 3
ALIBI_MAX_BIAS = 8.0
LANES = 128
MB_PAIRS = MB_WIDTH // LANES
NEG_BIG = -1e30

VMEM_LIMIT = 56 * 1024 * 1024


def _rms(x, w):
    ms = jnp.mean(x * x, axis=-1, keepdims=True)
    return x * lax.rsqrt(ms + NORM_EPS) * w


def _dot(a, b):
    return jnp.dot(a, b, preferred_element_type=F32)


def _dot_nt(a, b):
    return lax.dot_general(a, b, (((1,), (1,)), ((), ())), preferred_element_type=F32)


def _dot_tn(a, b):
    return lax.dot_general(a, b, (((0,), (0,)), ((), ())), preferred_element_type=F32)


def _split2(x):
    hi = x.astype(BF16)
    lo = (x - hi.astype(F32)).astype(BF16)
    return hi, lo


def _const_spec(shape):
    nd = len(shape)
    return pl.BlockSpec(shape, lambda *_: (0,) * nd, pipeline_mode=pl.Buffered(1))


def _ffn_kernel(x_ref, prew_ref, wg_ref, wu_ref, wd_ref, postw_ref, o_ref):
    x = x_ref[...]
    xn = _rms(x, prew_ref[...]).astype(BF16)
    g = _dot(xn, wg_ref[...])
    u = _dot(xn, wu_ref[...])
    a = (g * jax.nn.sigmoid(g) * u).astype(BF16)
    h = _dot(a, wd_ref[...])
    o_ref[...] = x + MACARON_WEIGHT * _rms(h, postw_ref[...])


def _ffn_block(x, pre_w, w_gate, w_up, w_down, post_w, tm=512):
    n, d = x.shape
    dff = w_gate.shape[1]
    return pl.pallas_call(
        _ffn_kernel,
        grid=(n // tm,),
        in_specs=[
            pl.BlockSpec((tm, d), lambda i: (i, 0)),
            _const_spec((1, d)),
            _const_spec((d, dff)),
            _const_spec((d, dff)),
            _const_spec((dff, d)),
            _const_spec((1, d)),
        ],
        out_specs=pl.BlockSpec((tm, d), lambda i: (i, 0)),
        out_shape=jax.ShapeDtypeStruct((n, d), F32),
        compiler_params=pltpu.CompilerParams(
            dimension_semantics=("arbitrary",), vmem_limit_bytes=VMEM_LIMIT),
        name="ffn_block",
    )(x, pre_w, w_gate, w_up, w_down, post_w)


def _inproj_kernel(x_ref, prew_ref, wdn_ref, wba_ref, wk_ref, wqt_ref, wvt_ref,
                   dn_ref, ba_ref, k_ref, qt_ref, vt_ref):
    bs = MB_BLOCK
    h = _rms(x_ref[...], prew_ref[...])
    h_hi, h_lo = _split2(h)
    dn_ref[...] = _dot(h_hi, wdn_ref[...])
    ba_ref[...] = _dot(h_hi, wba_ref[0]) + _dot(h_lo, wba_ref[0]) + _dot(h_hi, wba_ref[1])
    k_ref[...] = _dot(h_hi, wk_ref[0]) + _dot(h_lo, wk_ref[0]) + _dot(h_hi, wk_ref[1])
    qt = _dot_nt(wqt_ref[0], h_hi) + _dot_nt(wqt_ref[0], h_lo) + _dot_nt(wqt_ref[1], h_hi)
    vt = _dot_nt(wvt_ref[...], h_hi).astype(BF16)
    for i in range(qt_ref.shape[0]):
        qt_ref[i] = qt[:, i * bs:(i + 1) * bs]
        vt_ref[i] = vt[:, i * bs:(i + 1) * bs]


def _in_proj(x, pre_w, w_dn, w_ba, w_k, w_qt, w_vt, tm=512):
    n, d = x.shape
    bs = MB_BLOCK
    return pl.pallas_call(
        _inproj_kernel,
        grid=(n // tm,),
        in_specs=[
            pl.BlockSpec((tm, d), lambda i: (i, 0)),
            _const_spec((1, d)),
            _const_spec(w_dn.shape),
            _const_spec(w_ba.shape),
            _const_spec(w_k.shape),
            _const_spec(w_qt.shape),
            _const_spec(w_vt.shape),
        ],
        out_specs=[
            pl.BlockSpec((tm, w_dn.shape[1]), lambda i: (i, 0)),
            pl.BlockSpec((tm, LANES), lambda i: (i, 0)),
            pl.BlockSpec((tm, MB_WIDTH), lambda i: (i, 0)),
            pl.BlockSpec((tm // bs, MB_WIDTH, bs), lambda i: (i, 0, 0)),
            pl.BlockSpec((tm // bs, MB_WIDTH, bs), lambda i: (i, 0, 0)),
        ],
        out_shape=[
            jax.ShapeDtypeStruct((n, w_dn.shape[1]), F32),
            jax.ShapeDtypeStruct((n, LANES), F32),
            jax.ShapeDtypeStruct((n, MB_WIDTH), F32),
            jax.ShapeDtypeStruct((n // bs, MB_WIDTH, bs), F32),
            jax.ShapeDtypeStruct((n // bs, MB_WIDTH, bs), BF16),
        ],
        compiler_params=pltpu.CompilerParams(
            dimension_semantics=("arbitrary",), vmem_limit_bytes=VMEM_LIMIT),
        name="in_proj",
    )(x, pre_w, w_dn, w_ba, w_k, w_qt, w_vt)


def _dn_kernel(qkv_ref, ba_ref, convw_ref, alog_ref, dtb_ref, normw_ref, o_ref,
               xbuf_ref, state_ref):
    tt = DN_TILE
    c = DN_CHUNK
    dk = DN_HEAD_DIM
    t = pl.program_id(1)

    @pl.when(t == 0)
    def _():
        xbuf_ref[0:8, :] = jnp.zeros((8, 3 * DN_WIDTH), F32)
        state_ref[...] = jnp.zeros_like(state_ref)

    x = qkv_ref[0]
    xbuf_ref[8:8 + tt, :] = x
    cw = convw_ref[...]
    y = x * cw[DN_CONV - 1:DN_CONV, :]
    for s in range(1, DN_CONV):
        y = y + xbuf_ref[8 - s:8 - s + tt, :] * cw[DN_CONV - 1 - s:DN_CONV - s, :]
    xbuf_ref[0:8, :] = x[tt - 8:tt, :]
    y = y * jax.nn.sigmoid(y)

    ba = ba_ref[0]
    beta_all = jax.nn.sigmoid(ba)
    g_all = -jnp.exp(alog_ref[...]) * jax.nn.softplus(ba + dtb_ref[...])

    ri = lax.broadcasted_iota(jnp.int32, (tt, tt), 0)
    ci = lax.broadcasted_iota(jnp.int32, (tt, tt), 1)
    same_chunk = (ri // c) == (ci // c)
    incl = same_chunk & (ri >= ci)
    eye = ri == ci
    tril = jnp.where(incl, 1.0, 0.0).astype(BF16)
    ones_bd = jnp.where(same_chunk, 1.0, 0.0).astype(BF16)

    g1 = g_all.astype(BF16)
    r1 = g_all - g1.astype(F32)
    g2 = r1.astype(BF16)
    g3 = (r1 - g2.astype(F32)).astype(BF16)
    gcs_all = _dot(tril, g1) + _dot(tril, g2) + _dot(tril, g3)
    gtot_all = _dot(ones_bd, g1) + _dot(ones_bd, g2) + _dot(ones_bd, g3)

    heads = range(DN_HEADS)
    lmat, attn16, rhs, qd, kd, gtot = [], [], [], [], [], []
    for h in heads:
        qr = y[:, h * dk:(h + 1) * dk]
        kr = y[:, DN_WIDTH + h * dk:DN_WIDTH + (h + 1) * dk]
        v = y[:, 2 * DN_WIDTH + h * dk:2 * DN_WIDTH + (h + 1) * dk]
        q = qr * lax.rsqrt(jnp.sum(qr * qr, axis=-1, keepdims=True) + NORM_EPS) * (dk ** -0.5)
        k = kr * lax.rsqrt(jnp.sum(kr * kr, axis=-1, keepdims=True) + NORM_EPS)
        beta = beta_all[:, h:h + 1]
        gcs = gcs_all[:, DN_HEADS + h:DN_HEADS + h + 1]
        gtot.append(gtot_all[:, DN_HEADS + h:DN_HEADS + h + 1])
        eg = jnp.exp(gcs)

        g_row = jnp.sum(jnp.where(eye, gcs, 0.0), axis=0, keepdims=True)
        decay = jnp.exp(jnp.where(incl, gcs - g_row, NEG_BIG))

        kb = k * beta
        k16 = k.astype(BF16)
        lmat.append(jnp.where(eye, 0.0, _dot_nt(kb.astype(BF16), k16) * decay))
        attn16.append((_dot_nt(q.astype(BF16), k16) * decay).astype(BF16))
        rhs.append(jnp.concatenate([v * beta, kb * eg], axis=1).astype(BF16))
        qd.append(q * eg)
        kd.append((k * jnp.exp(gtot[h] - gcs)).astype(BF16))

    xinv = [jnp.where(eye, 1.0, -lmat[h]) for h in heads]
    m = [lmat[h].astype(BF16) for h in heads]
    power = 2
    while power < c:
        m = [_dot(m[h], m[h]).astype(BF16) for h in heads]
        xinv = [xinv[h] + _dot(xinv[h].astype(BF16), m[h]) for h in heads]
        power *= 2

    uw16 = [_dot(xinv[h].astype(BF16), rhs[h]).astype(BF16) for h in heads]
    au_aw = [_dot(attn16[h], uw16[h]) for h in heads]
    au = [au_aw[h][:, :dk] for h in heads]
    e16 = [(qd[h] - au_aw[h][:, dk:]).astype(BF16) for h in heads]

    s = [state_ref[h] for h in heads]
    outs = [[] for _ in heads]
    for ch in range(tt // c):
        lo, hi = ch * c, (ch + 1) * c
        bc = [_dot_tn(kd[h][lo:hi], uw16[h][lo:hi]) for h in heads]
        for h in heads:
            s16 = s[h].astype(BF16)
            outs[h].append(_dot(e16[h][lo:hi], s16) + au[h][lo:hi])
            s[h] = (s[h] * jnp.exp(gtot[h][lo:lo + 1, :]) + bc[h][:, :dk]
                    - _dot(bc[h][:, dk:].astype(BF16), s16))
    for h in heads:
        state_ref[h] = s[h]
        o = jnp.concatenate(outs[h], axis=0)
        o_ref[0, :, h * dk:(h + 1) * dk] = _rms(o, normw_ref[...])


def _deltanet(qkv, ba, conv_w, alog_row, dtb_row, norm_w):
    bsz, t_len, width = qkv.shape
    tt = DN_TILE
    return pl.pallas_call(
        _dn_kernel,
        grid=(bsz, t_len // tt),
        in_specs=[
            pl.BlockSpec((1, tt, width), lambda b, t: (b, t, 0)),
            pl.BlockSpec((1, tt, LANES), lambda b, t: (b, t, 0)),
            _const_spec(conv_w.shape),
            _const_spec((1, LANES)),
            _const_spec((1, LANES)),
            _const_spec((1, DN_HEAD_DIM)),
        ],
        out_specs=pl.BlockSpec((1, tt, DN_WIDTH), lambda b, t: (b, t, 0)),
        out_shape=jax.ShapeDtypeStruct((bsz, t_len, DN_WIDTH), F32),
        scratch_shapes=[
            pltpu.VMEM((8 + tt, width), F32),
            pltpu.VMEM((DN_HEADS, DN_HEAD_DIM, DN_HEAD_DIM), F32),
        ],
        compiler_params=pltpu.CompilerParams(
            dimension_semantics=("arbitrary", "arbitrary"), vmem_limit_bytes=VMEM_LIMIT),
        name="deltanet",
    )(qkv, ba, conv_w, alog_row, dtb_row, norm_w)


SUBLANES = 8
MB_SUPER = 4
MB_AUX_MASK = 16
MB_SUM_ROWS = 16
ALIBI_STEP = int(ALIBI_MAX_BIAS) // MB_HEADS
assert ALIBI_STEP * MB_HEADS == ALIBI_MAX_BIAS
LOG2E = math.log2(math.e)
LOG2E_PIECES = (1.4453125, -0.00262451171875, 7.063150405883789e-06, -1.05355866253376e-08)


def _moba_kernel(qt_ref, k_ref, vt_ref, o_ref, kaug_ref, kmean_ref, sa_ref, sb_ref, *, n_blk):
    bs = MB_BLOCK
    hd = MB_HEAD_DIM
    sup = MB_SUPER * bs
    nbp = -(-n_blk // SUBLANES) * SUBLANES
    p = pl.program_id(1)
    own = pl.program_id(2)
    lane = lax.broadcasted_iota(jnp.int32, (bs, LANES), 1)
    row = lax.broadcasted_iota(jnp.int32, (bs, LANES), 0)

    @pl.when(own == 0)
    def _():
        kmean_ref[...] = jnp.zeros_like(kmean_ref)

        def build(j, carry):
            off = pl.multiple_of(j * bs, bs)
            kblk = k_ref[0, pl.ds(off, bs), :]
            kmean_ref[pl.ds(j, 1), :] = jnp.mean(kblk, axis=0, keepdims=True)
            kaug_ref[pl.ds(off, bs), 0:LANES] = kblk.astype(BF16)
            kstart = jnp.full((bs, LANES), j * bs, jnp.int32).astype(F32)
            aux = jnp.where(lane < 2, 1.0,
                            jnp.where(lane < 6, row.astype(F32),
                                      jnp.where(lane < 10, kstart,
                                                jnp.where(lane == MB_AUX_MASK + j, 1.0, 0.0))))
            kaug_ref[pl.ds(off, bs), LANES:2 * LANES] = aux.astype(BF16)
            return carry

        lax.fori_loop(0, n_blk, build, 0)

    qt = qt_ref[0]
    km_hi, km_lo = _split2(kmean_ref[...])

    chan = lax.broadcasted_iota(jnp.int32, (LANES, bs), 0)
    blk = lax.broadcasted_iota(jnp.int32, (nbp, bs), 0)
    blk_f = blk.astype(F32)
    aux_row = lax.broadcasted_iota(jnp.int32, (MB_AUX_MASK, bs), 0)
    qpos = (lax.broadcasted_iota(jnp.int32, (MB_AUX_MASK, bs), 1) + own * bs).astype(F32)
    aux_pad = jnp.zeros((LANES - MB_AUX_MASK - nbp, bs), F32)

    qaug = []
    for hh in range(2):
        qth = jnp.where((chan >= hh * hd) & (chan < (hh + 1) * hd), qt, 0.0)
        slope_bits = (127 - ALIBI_STEP * (2 * p + hh + 1)) << 23
        slope = lax.bitcast_convert_type(jnp.full((MB_AUX_MASK, bs), slope_bits, jnp.int32), F32)

        q_hi, q_lo = _split2(qth)
        gate = (_dot(km_hi, q_hi) + _dot(km_hi, q_lo) + _dot(km_lo, q_hi))[:nbp]
        gate = jnp.where(blk < own, gate, -jnp.inf)
        sel = jnp.zeros((nbp, bs), F32)
        for _ in range(MB_TOPK):
            mx = jnp.max(gate, axis=0, keepdims=True)
            first = jnp.min(jnp.where(gate == mx, blk_f, float(nbp)), axis=0, keepdims=True)
            hit = blk_f == first
            sel = jnp.where(hit, 1.0, sel)
            gate = jnp.where(hit, -jnp.inf, gate)
        keep = jnp.where(blk < own, sel, jnp.where(blk == own, 1.0, 0.0))
        mask_rows = jnp.where(keep > 0.5, 0.0, NEG_BIG)
        qconst = -(slope * LOG2E) * qpos
        qconst_hi = qconst.astype(BF16).astype(F32)
        piece_id = (aux_row + 2) & 3
        piece = jnp.where(piece_id == 0, LOG2E_PIECES[0],
                          jnp.where(piece_id == 1, LOG2E_PIECES[1],
                                    jnp.where(piece_id == 2, LOG2E_PIECES[2], LOG2E_PIECES[3])))
        bias_rows = jnp.where(aux_row == 0, qconst_hi,
                              jnp.where(aux_row == 1, qconst - qconst_hi,
                                        jnp.where(aux_row < 10, slope * piece, 0.0)))
        qaug.append(jnp.concatenate([qth * (hd ** -0.5 * LOG2E), bias_rows, mask_rows, aux_pad],
                                    axis=0).astype(BF16))

    def keys(i):
        return kaug_ref[pl.ds(pl.multiple_of(i * sup, sup), sup), :]

    def values_t(i, hh):
        return jnp.concatenate([vt_ref[i * MB_SUPER + u, hh * hd:(hh + 1) * hd, :]
                                for u in range(MB_SUPER)], axis=1)

    n_grp = n_blk // MB_SUPER
    grp = own // MB_SUPER
    qaug2 = jnp.concatenate(qaug, axis=1)

    def group_at(t):
        g = jnp.where(t == 0, grp, jnp.where(t > grp, grp + 1, t - 1))
        return jnp.minimum(g, n_grp - 1)

    def scores(g):
        return _dot(keys(g), qaug2)

    ones_rows = jnp.ones((MB_SUM_ROWS, sup), BF16)

    def produce(s_ref, s2):
        s_ref[...] = s2
        return jnp.max(s2, axis=0, keepdims=True)

    def softmax_step(s_ref, smax, g, carry):
        hs = range(2)
        m_i = [carry[2 * hh] for hh in hs]
        m_new = [jnp.maximum(m_i[hh], smax[:, hh * bs:(hh + 1) * bs]) for hh in hs]
        alpha = [jnp.exp2(m_i[hh] - m_new[hh]) for hh in hs]
        pexp = [jnp.exp2((s_ref[:, hh * bs:(hh + 1) * bs] - m_new[hh]).astype(BF16)) for hh in hs]
        acc_new = [carry[2 * hh + 1] * alpha[hh]
                   + _dot(jnp.concatenate([values_t(g, hh), ones_rows], axis=0), pexp[hh])
                   for hh in hs]
        return (m_new[0], acc_new[0], m_new[1], acc_new[1])

    rel = (lax.broadcasted_iota(jnp.int32, (sup, 2 * bs), 0) - (own - grp * MB_SUPER) * bs)
    qi = lax.broadcasted_iota(jnp.int32, (sup, 2 * bs), 1) & (bs - 1)
    future = (rel > qi) & (rel < bs)
    smax_a0 = produce(sa_ref, jnp.where(future, NEG_BIG, scores(grp)))

    def pair(u, carry):
        t = 2 * u
        smax_a, carry = carry[0], carry[1:]
        smax_b = produce(sb_ref, scores(group_at(t + 1)))
        carry = softmax_step(sa_ref, smax_a, group_at(t), carry)
        smax_a = produce(sa_ref, scores(group_at(t + 2)))
        return (smax_a,) + softmax_step(sb_ref, smax_b, group_at(t + 1), carry)

    stat0 = jnp.full((1, bs), -jnp.inf, F32)
    acc0 = jnp.zeros((hd + MB_SUM_ROWS, bs), F32)
    fin = lax.fori_loop(0, grp // 2 + 1, pair, (smax_a0, stat0, acc0, stat0, acc0))
    out_t = jnp.concatenate([fin[2][:hd] / fin[2][hd:hd + 1], fin[4][:hd] / fin[4][hd:hd + 1]],
                            axis=0)
    o_ref[0] = out_t.T


def _moba(qt, k, vt):
    bsz, t_len, _ = k.shape
    bs = MB_BLOCK
    n_blk = t_len // bs
    assert n_blk % (2 * MB_SUPER) == 0 and MB_AUX_MASK + n_blk <= LANES
    return pl.pallas_call(
        functools.partial(_moba_kernel, n_blk=n_blk),
        grid=(bsz, MB_PAIRS, n_blk),
        in_specs=[
            pl.BlockSpec((1, LANES, bs), lambda b, p, i: (b * n_blk + i, p, 0)),
            pl.BlockSpec((1, t_len, LANES), lambda b, p, i: (b, 0, p)),
            pl.BlockSpec((n_blk, LANES, bs), lambda b, p, i: (b, p, 0)),
        ],
        out_specs=pl.BlockSpec((1, bs, LANES), lambda b, p, i: (b, i, p)),
        out_shape=jax.ShapeDtypeStruct((bsz, t_len, MB_WIDTH), F32),
        scratch_shapes=[
            pltpu.VMEM((t_len, 2 * LANES), BF16),
            pltpu.VMEM((LANES, LANES), F32),
            pltpu.VMEM((MB_SUPER * bs, 2 * bs), F32),
            pltpu.VMEM((MB_SUPER * bs, 2 * bs), F32),
        ],
        compiler_params=pltpu.CompilerParams(
            dimension_semantics=("arbitrary", "arbitrary", "arbitrary"),
            vmem_limit_bytes=VMEM_LIMIT),
        name="moba",
    )(qt, k, vt)


def _mixout_kernel(x_ref, odn_ref, omb_ref, prew_ref, wz_ref, wgd_ref, wgm_ref,
                   wbd_ref, wbm_ref, wo_ref, postw_ref, o_ref):
    x = x_ref[...]
    h = _rms(x, prew_ref[...]).astype(BF16)
    z = _dot(h, wz_ref[...])
    gate_dn = jax.nn.sigmoid(_dot(h, wgd_ref[...]))
    gate_mb = jax.nn.sigmoid(_dot(h, wgm_ref[...]))
    o_dn = odn_ref[...] * (z * jax.nn.sigmoid(z))
    y_dn = _dot(o_dn.astype(BF16), wbd_ref[...])
    y_mb = _dot(omb_ref[...].astype(BF16), wbm_ref[...])
    merged = gate_dn * y_dn + gate_mb * y_mb
    y = _dot(merged.astype(BF16), wo_ref[...])
    o_ref[...] = x + _rms(y, postw_ref[...])


def _mix_out(x, o_dn, o_mb, pre_w, w_z, w_gd, w_gm, w_bd, w_bm, w_o, post_w, tm=512):
    n, d = x.shape
    return pl.pallas_call(
        _mixout_kernel,
        grid=(n // tm,),
        in_specs=[
            pl.BlockSpec((tm, d), lambda i: (i, 0)),
            pl.BlockSpec((tm, o_dn.shape[1]), lambda i: (i, 0)),
            pl.BlockSpec((tm, o_mb.shape[1]), lambda i: (i, 0)),
            _const_spec((1, d)),
            _const_spec(w_z.shape),
            _const_spec(w_gd.shape),
            _const_spec(w_gm.shape),
            _const_spec(w_bd.shape),
            _const_spec(w_bm.shape),
            _const_spec(w_o.shape),
            _const_spec((1, d)),
        ],
        out_specs=pl.BlockSpec((tm, d), lambda i: (i, 0)),
        out_shape=jax.ShapeDtypeStruct((n, d), F32),
        compiler_params=pltpu.CompilerParams(
            dimension_semantics=("arbitrary",), vmem_limit_bytes=VMEM_LIMIT),
        name="mix_out",
    )(x, o_dn, o_mb, pre_w, w_z, w_gd, w_gm, w_bd, w_bm, w_o, post_w)


def _layer(x, ffn1_pre_w, ffn1_w_gate, ffn1_w_up, ffn1_w_down, ffn1_post_w,
           mix_pre_w, w_in, dn_conv_w, dn_a_log, dn_dt_bias, dn_norm_w,
           w_branch_dn, w_branch_mb, w_out, mix_post_w,
           ffn2_pre_w, ffn2_w_gate, ffn2_w_up, ffn2_w_down, ffn2_post_w):
    bsz, t_len, d = x.shape
    n = bsz * t_len
    row = lambda w: w.reshape(1, -1).astype(F32)
    b16 = lambda w: w.astype(BF16)

    x = x.reshape(n, d)
    x = _ffn_block(x, row(ffn1_pre_w), b16(ffn1_w_gate), b16(ffn1_w_up), b16(ffn1_w_down),
                   row(ffn1_post_w))

    o = 0
    w_dn = w_in[:, o:o + 3 * DN_WIDTH]; o += 3 * DN_WIDTH
    w_z = w_in[:, o:o + DN_WIDTH]; o += DN_WIDTH
    w_ba = w_in[:, o:o + 2 * DN_HEADS]; o += 2 * DN_HEADS
    w_q = w_in[:, o:o + MB_WIDTH]; o += MB_WIDTH
    w_k = w_in[:, o:o + MB_WIDTH]; o += MB_WIDTH
    w_v = w_in[:, o:o + MB_WIDTH]; o += MB_WIDTH
    w_gd = w_in[:, o:o + d]; o += d
    w_gm = w_in[:, o:o + d]; o += d
    w_ba = jnp.pad(w_ba, ((0, 0), (0, LANES - 2 * DN_HEADS)))

    def hi_lo(w):
        hi = w.astype(BF16)
        return jnp.stack([hi, (w - hi.astype(F32)).astype(BF16)])

    dn_qkv, ba, mb_k, mb_qt, mb_vt = _in_proj(x, row(mix_pre_w), b16(w_dn), hi_lo(w_ba), hi_lo(w_k),
                                              hi_lo(w_q.T), b16(w_v.T))

    pad_heads = lambda p: jnp.pad(p.astype(F32), (DN_HEADS, LANES - 2 * DN_HEADS)).reshape(1, LANES)
    o_dn = _deltanet(dn_qkv.reshape(bsz, t_len, -1), ba.reshape(bsz, t_len, LANES),
                     dn_conv_w.astype(F32), pad_heads(dn_a_log), pad_heads(dn_dt_bias),
                     row(dn_norm_w))

    o_mb = _moba(mb_qt, mb_k.reshape(bsz, t_len, MB_WIDTH), mb_vt)

    x = _mix_out(x, o_dn.reshape(n, DN_WIDTH), o_mb.reshape(n, MB_WIDTH), row(mix_pre_w),
                 b16(w_z), b16(w_gd), b16(w_gm), b16(w_branch_dn), b16(w_branch_mb), b16(w_out),
                 row(mix_post_w))

    x = _ffn_block(x, row(ffn2_pre_w), b16(ffn2_w_gate), b16(ffn2_w_up), b16(ffn2_w_down),
                   row(ffn2_post_w))
    return x.reshape(bsz, t_len, d)


def kernel(x, ffn1_pre_w, ffn1_w_gate, ffn1_w_up, ffn1_w_down, ffn1_post_w, mix_pre_w, w_in, dn_conv_w, dn_a_log, dn_dt_bias, dn_norm_w, w_branch_dn, w_branch_mb, w_out, mix_post_w, ffn2_pre_w, ffn2_w_gate, ffn2_w_up, ffn2_w_down, ffn2_post_w):
    depth = w_in.shape[0]
    for l in range(depth):
        x = _layer(x, ffn1_pre_w[l], ffn1_w_gate[l], ffn1_w_up[l], ffn1_w_down[l], ffn1_post_w[l],
                   mix_pre_w[l], w_in[l], dn_conv_w[l], dn_a_log[l], dn_dt_bias[l], dn_norm_w[l],
                   w_branch_dn[l], w_branch_mb[l], w_out[l], mix_post_w[l],
                   ffn2_pre_w[l], ffn2_w_gate[l], ffn2_w_up[l], ffn2_w_down[l], ffn2_post_w[l])
    return x
```

```python
import functools
import math

import jax
import jax.numpy as jnp
from jax import lax
from jax.experimental import pallas as pl
from jax.experimental.pallas import tpu as pltpu

F32 = jnp.float32
BF16 = jnp.bfloat16

NORM_EPS = 1e-6
MACARON_WEIGHT = 0.5

DN_HEADS = 4
DN_HEAD_DIM = 128
DN_WIDTH = DN_HEADS * DN_HEAD_DIM
DN_CONV = 4
DN_CHUNK = 64
DN_TILE = 256

MB_HEADS = 8
MB_HEAD_DIM = 64
MB_WIDTH = MB_HEADS * MB_HEAD_DIM
MB_BLOCK = 256
MB_TOPK = 3
ALIBI_MAX_BIAS = 8.0
LANES = 128
MB_PAIRS = MB_WIDTH // LANES
NEG_BIG = -1e30

VMEM_LIMIT = 56 * 1024 * 1024


def _rms(x, w):
    ms = jnp.mean(x * x, axis=-1, keepdims=True)
    return x * lax.rsqrt(ms + NORM_EPS) * w


def _dot(a, b):
    return jnp.dot(a, b, preferred_element_type=F32)


def _dot_nt(a, b):
    return lax.dot_general(a, b, (((1,), (1,)), ((), ())), preferred_element_type=F32)


def _dot_tn(a, b):
    return lax.dot_general(a, b, (((0,), (0,)), ((), ())), preferred_element_type=F32)


def _split2(x):
    hi = x.astype(BF16)
    lo = (x - hi.astype(F32)).astype(BF16)
    return hi, lo


def _const_spec(shape):
    nd = len(shape)
    return pl.BlockSpec(shape, lambda *_: (0,) * nd, pipeline_mode=pl.Buffered(1))


def _ffn_kernel(x_ref, prew_ref, wg_ref, wu_ref, wd_ref, postw_ref, o_ref):
    x = x_ref[...]
    xn = _rms(x, prew_ref[...]).astype(BF16)
    g = _dot(xn, wg_ref[...])
    u = _dot(xn, wu_ref[...])
    a = (g * jax.nn.sigmoid(g) * u).astype(BF16)
    h = _dot(a, wd_ref[...])
    o_ref[...] = x + MACARON_WEIGHT * _rms(h, postw_ref[...])


def _ffn_block(x, pre_w, w_gate, w_up, w_down, post_w, tm=512):
    n, d = x.shape
    dff = w_gate.shape[1]
    return pl.pallas_call(
        _ffn_kernel,
        grid=(n // tm,),
        in_specs=[
            pl.BlockSpec((tm, d), lambda i: (i, 0)),
            _const_spec((1, d)),
            _const_spec((d, dff)),
            _const_spec((d, dff)),
            _const_spec((dff, d)),
            _const_spec((1, d)),
        ],
        out_specs=pl.BlockSpec((tm, d), lambda i: (i, 0)),
        out_shape=jax.ShapeDtypeStruct((n, d), F32),
        compiler_params=pltpu.CompilerParams(
            dimension_semantics=("arbitrary",), vmem_limit_bytes=VMEM_LIMIT),
        name="ffn_block",
    )(x, pre_w, w_gate, w_up, w_down, post_w)


def _inproj_kernel(x_ref, prew_ref, wdn_ref, wba_ref, wk_ref, wqt_ref, wvt_ref,
                   dn_ref, ba_ref, k_ref, qt_ref, vt_ref):
    bs = MB_BLOCK
    h = _rms(x_ref[...], prew_ref[...])
    h_hi, h_lo = _split2(h)
    dn_ref[...] = _dot(h_hi, wdn_ref[...])
    ba_ref[...] = _dot(h_hi, wba_ref[0]) + _dot(h_lo, wba_ref[0]) + _dot(h_hi, wba_ref[1])
    k_ref[...] = _dot(h_hi, wk_ref[0]) + _dot(h_lo, wk_ref[0]) + _dot(h_hi, wk_ref[1])
    qt = _dot_nt(wqt_ref[0], h_hi) + _dot_nt(wqt_ref[0], h_lo) + _dot_nt(wqt_ref[1], h_hi)
    vt = _dot_nt(wvt_ref[...], h_hi).astype(BF16)
    for i in range(qt_ref.shape[0]):
        qt_ref[i] = qt[:, i * bs:(i + 1) * bs]
        vt_ref[i] = vt[:, i * bs:(i + 1) * bs]


def _in_proj(x, pre_w, w_dn, w_ba, w_k, w_qt, w_vt, tm=512):
    n, d = x.shape
    bs = MB_BLOCK
    return pl.pallas_call(
        _inproj_kernel,
        grid=(n // tm,),
        in_specs=[
            pl.BlockSpec((tm, d), lambda i: (i, 0)),
            _const_spec((1, d)),
            _const_spec(w_dn.shape),
            _const_spec(w_ba.shape),
            _const_spec(w_k.shape),
            _const_spec(w_qt.shape),
            _const_spec(w_vt.shape),
        ],
        out_specs=[
            pl.BlockSpec((tm, w_dn.shape[1]), lambda i: (i, 0)),
            pl.BlockSpec((tm, LANES), lambda i: (i, 0)),
            pl.BlockSpec((tm, MB_WIDTH), lambda i: (i, 0)),
            pl.BlockSpec((tm // bs, MB_WIDTH, bs), lambda i: (i, 0, 0)),
            pl.BlockSpec((tm // bs, MB_WIDTH, bs), lambda i: (i, 0, 0)),
        ],
        out_shape=[
            jax.ShapeDtypeStruct((n, w_dn.shape[1]), F32),
            jax.ShapeDtypeStruct((n, LANES), F32),
            jax.ShapeDtypeStruct((n, MB_WIDTH), F32),
            jax.ShapeDtypeStruct((n // bs, MB_WIDTH, bs), F32),
            jax.ShapeDtypeStruct((n // bs, MB_WIDTH, bs), BF16),
        ],
        compiler_params=pltpu.CompilerParams(
            dimension_semantics=("arbitrary",), vmem_limit_bytes=VMEM_LIMIT),
        name="in_proj",
    )(x, pre_w, w_dn, w_ba, w_k, w_qt, w_vt)


def _dn_kernel(qkv_ref, ba_ref, convw_ref, alog_ref, dtb_ref, normw_ref, o_ref,
               xbuf_ref, state_ref):
    tt = DN_TILE
    c = DN_CHUNK
    dk = DN_HEAD_DIM
    t = pl.program_id(1)

    @pl.when(t == 0)
    def _():
        xbuf_ref[0:8, :] = jnp.zeros((8, 3 * DN_WIDTH), F32)
        state_ref[...] = jnp.zeros_like(state_ref)

    x = qkv_ref[0]
    xbuf_ref[8:8 + tt, :] = x
    cw = convw_ref[...]
    y = x * cw[DN_CONV - 1:DN_CONV, :]
    for s in range(1, DN_CONV):
        y = y + xbuf_ref[8 - s:8 - s + tt, :] * cw[DN_CONV - 1 - s:DN_CONV - s, :]
    xbuf_ref[0:8, :] = x[tt - 8:tt, :]
    y = y * jax.nn.sigmoid(y)

    ba = ba_ref[0]
    beta_all = jax.nn.sigmoid(ba)
    g_all = -jnp.exp(alog_ref[...]) * jax.nn.softplus(ba + dtb_ref[...])

    ri = lax.broadcasted_iota(jnp.int32, (tt, tt), 0)
    ci = lax.broadcasted_iota(jnp.int32, (tt, tt), 1)
    same_chunk = (ri // c) == (ci // c)
    incl = same_chunk & (ri >= ci)
    eye = ri == ci
    tril = jnp.where(incl, 1.0, 0.0).astype(BF16)
    ones_bd = jnp.where(same_chunk, 1.0, 0.0).astype(BF16)

    g1 = g_all.astype(BF16)
    r1 = g_all - g1.astype(F32)
    g2 = r1.astype(BF16)
    g3 = (r1 - g2.astype(F32)).astype(BF16)
    gcs_all = _dot(tril, g1) + _dot(tril, g2) + _dot(tril, g3)
    gtot_all = _dot(ones_bd, g1) + _dot(ones_bd, g2) + _dot(ones_bd, g3)

    heads = range(DN_HEADS)
    lmat, attn16, rhs, qd, kd, gtot = [], [], [], [], [], []
    for h in heads:
        qr = y[:, h * dk:(h + 1) * dk]
        kr = y[:, DN_WIDTH + h * dk:DN_WIDTH + (h + 1) * dk]
        v = y[:, 2 * DN_WIDTH + h * dk:2 * DN_WIDTH + (h + 1) * dk]
        q = qr * lax.rsqrt(jnp.sum(qr * qr, axis=-1, keepdims=True) + NORM_EPS) * (dk ** -0.5)
        k = kr * lax.rsqrt(jnp.sum(kr * kr, axis=-1, keepdims=True) + NORM_EPS)
        beta = beta_all[:, h:h + 1]
        gcs = gcs_all[:, DN_HEADS + h:DN_HEADS + h + 1]
        gtot.append(gtot_all[:, DN_HEADS + h:DN_HEADS + h + 1])
        eg = jnp.exp(gcs)

        g_row = jnp.sum(jnp.where(eye, gcs, 0.0), axis=0, keepdims=True)
        decay = jnp.exp(jnp.where(incl, gcs - g_row, NEG_BIG))

        kb = k * beta
        k16 = k.astype(BF16)
        lmat.append(jnp.where(eye, 0.0, _dot_nt(kb.astype(BF16), k16) * decay))
        attn16.append((_dot_nt(q.astype(BF16), k16) * decay).astype(BF16))
        rhs.append(jnp.concatenate([v * beta, kb * eg], axis=1).astype(BF16))
        qd.append(q * eg)
        kd.append((k * jnp.exp(gtot[h] - gcs)).astype(BF16))

    xinv = [jnp.where(eye, 1.0, -lmat[h]) for h in heads]
    m = [lmat[h].astype(BF16) for h in heads]
    power = 2
    while power < c:
        m = [_dot(m[h], m[h]).astype(BF16) for h in heads]
        xinv = [xinv[h] + _dot(xinv[h].astype(BF16), m[h]) for h in heads]
        power *= 2

    uw16 = [_dot(xinv[h].astype(BF16), rhs[h]).astype(BF16) for h in heads]
    au_aw = [_dot(attn16[h], uw16[h]) for h in heads]
    au = [au_aw[h][:, :dk] for h in heads]
    e16 = [(qd[h] - au_aw[h][:, dk:]).astype(BF16) for h in heads]

    s = [state_ref[h] for h in heads]
    outs = [[] for _ in heads]
    for ch in range(tt // c):
        lo, hi = ch * c, (ch + 1) * c
        bc = [_dot_tn(kd[h][lo:hi], uw16[h][lo:hi]) for h in heads]
        for h in heads:
            s16 = s[h].astype(BF16)
            outs[h].append(_dot(e16[h][lo:hi], s16) + au[h][lo:hi])
            s[h] = (s[h] * jnp.exp(gtot[h][lo:lo + 1, :]) + bc[h][:, :dk]
                    - _dot(bc[h][:, dk:].astype(BF16), s16))
    for h in heads:
        state_ref[h] = s[h]
        o = jnp.concatenate(outs[h], axis=0)
        o_ref[0, :, h * dk:(h + 1) * dk] = _rms(o, normw_ref[...])


def _deltanet(qkv, ba, conv_w, alog_row, dtb_row, norm_w):
    bsz, t_len, width = qkv.shape
    tt = DN_TILE
    return pl.pallas_call(
        _dn_kernel,
        grid=(bsz, t_len // tt),
        in_specs=[
            pl.BlockSpec((1, tt, width), lambda b, t: (b, t, 0)),
            pl.BlockSpec((1, tt, LANES), lambda b, t: (b, t, 0)),
            _const_spec(conv_w.shape),
            _const_spec((1, LANES)),
            _const_spec((1, LANES)),
            _const_spec((1, DN_HEAD_DIM)),
        ],
        out_specs=pl.BlockSpec((1, tt, DN_WIDTH), lambda b, t: (b, t, 0)),
        out_shape=jax.ShapeDtypeStruct((bsz, t_len, DN_WIDTH), F32),
        scratch_shapes=[
            pltpu.VMEM((8 + tt, width), F32),
            pltpu.VMEM((DN_HEADS, DN_HEAD_DIM, DN_HEAD_DIM), F32),
        ],
        compiler_params=pltpu.CompilerParams(
            dimension_semantics=("arbitrary", "arbitrary"), vmem_limit_bytes=VMEM_LIMIT),
        name="deltanet",
    )(qkv, ba, conv_w, alog_row, dtb_row, norm_w)


SUBLANES = 8
MB_SUPER = 2
MB_AUX_MASK = 16
MB_SUM_ROWS = 16
ALIBI_STEP = int(ALIBI_MAX_BIAS) // MB_HEADS
assert ALIBI_STEP * MB_HEADS == ALIBI_MAX_BIAS
LOG2E = math.log2(math.e)
LOG2E_PIECES = (1.4453125, -0.00262451171875, 7.063150405883789e-06, -1.05355866253376e-08)


def _moba_kernel(qt_ref, k_ref, vt_ref, o_ref, kaug_ref, kmean_ref, sa_ref, sb_ref, *, n_blk):
    bs = MB_BLOCK
    hd = MB_HEAD_DIM
    sup = MB_SUPER * bs
    nbp = -(-n_blk // SUBLANES) * SUBLANES
    p = pl.program_id(1)
    own = pl.program_id(2)
    lane = lax.broadcasted_iota(jnp.int32, (bs, LANES), 1)
    row = lax.broadcasted_iota(jnp.int32, (bs, LANES), 0)

    @pl.when(own == 0)
    def _():
        kmean_ref[...] = jnp.zeros_like(kmean_ref)

        def build(j, carry):
            off = pl.multiple_of(j * bs, bs)
            kblk = k_ref[0, pl.ds(off, bs), :]
            kmean_ref[pl.ds(j, 1), :] = jnp.mean(kblk, axis=0, keepdims=True)
            kaug_ref[pl.ds(off, bs), 0:LANES] = kblk.astype(BF16)
            kstart = jnp.full((bs, LANES), j * bs, jnp.int32).astype(F32)
            aux = jnp.where(lane < 2, 1.0,
                            jnp.where(lane < 6, row.astype(F32),
                                      jnp.where(lane < 10, kstart,
                                                jnp.where(lane == MB_AUX_MASK + j, 1.0, 0.0))))
            kaug_ref[pl.ds(off, bs), LANES:2 * LANES] = aux.astype(BF16)
            return carry

        lax.fori_loop(0, n_blk, build, 0)

    qt = qt_ref[0]
    km_hi, km_lo = _split2(kmean_ref[...])

    chan = lax.broadcasted_iota(jnp.int32, (LANES, bs), 0)
    blk = lax.broadcasted_iota(jnp.int32, (nbp, bs), 0)
    blk_f = blk.astype(F32)
    aux_row = lax.broadcasted_iota(jnp.int32, (MB_AUX_MASK, bs), 0)
    qpos = (lax.broadcasted_iota(jnp.int32, (MB_AUX_MASK, bs), 1) + own * bs).astype(F32)
    aux_pad = jnp.zeros((LANES - MB_AUX_MASK - nbp, bs), F32)

    qaug = []
    for hh in range(2):
        qth = jnp.where((chan >= hh * hd) & (chan < (hh + 1) * hd), qt, 0.0)
        slope_bits = (127 - ALIBI_STEP * (2 * p + hh + 1)) << 23
        slope = lax.bitcast_convert_type(jnp.full((MB_AUX_MASK, bs), slope_bits, jnp.int32), F32)

        q_hi, q_lo = _split2(qth)
        gate = (_dot(km_hi, q_hi) + _dot(km_hi, q_lo) + _dot(km_lo, q_hi))[:nbp]
        gate = jnp.where(blk < own, gate, -jnp.inf)
        sel = jnp.zeros((nbp, bs), F32)
        for _ in range(MB_TOPK):
            mx = jnp.max(gate, axis=0, keepdims=True)
            first = jnp.min(jnp.where(gate == mx, blk_f, float(nbp)), axis=0, keepdims=True)
            hit = blk_f == first
            sel = jnp.where(hit, 1.0, sel)
            gate = jnp.where(hit, -jnp.inf, gate)
        keep = jnp.where(blk < own, sel, jnp.where(blk == own, 1.0, 0.0))
        mask_rows = jnp.where(keep > 0.5, 0.0, NEG_BIG)
        qconst = -(slope * LOG2E) * qpos
        qconst_hi = qconst.astype(BF16).astype(F32)
        piece_id = (aux_row + 2) & 3
        piece = jnp.where(piece_id == 0, LOG2E_PIECES[0],
                          jnp.where(piece_id == 1, LOG2E_PIECES[1],
                                    jnp.where(piece_id == 2, LOG2E_PIECES[2], LOG2E_PIECES[3])))
        bias_rows = jnp.where(aux_row == 0, qconst_hi,
                              jnp.where(aux_row == 1, qconst - qconst_hi,
                                        jnp.where(aux_row < 10, slope * piece, 0.0)))
        qaug.append(jnp.concatenate([qth * (hd ** -0.5 * LOG2E), bias_rows, mask_rows, aux_pad],
                                    axis=0).astype(BF16))

    def keys(i):
        return kaug_ref[pl.ds(pl.multiple_of(i * sup, sup), sup), :]

    def values_t(i, hh):
        return jnp.concatenate([vt_ref[i * MB_SUPER + u, hh * hd:(hh + 1) * hd, :]
                                for u in range(MB_SUPER)], axis=1)

    n_grp = n_blk // MB_SUPER
    grp = own // MB_SUPER
    qaug2 = jnp.concatenate(qaug, axis=1)

    def group_at(t):
        g = jnp.where(t == 0, grp, jnp.where(t > grp, grp + 1, t - 1))
        return jnp.minimum(g, n_grp - 1)

    def scores(g):
        return _dot(keys(g), qaug2)

    ones_rows = jnp.ones((MB_SUM_ROWS, sup), BF16)

    def produce(s_ref, s2):
        s_ref[...] = s2
        return jnp.max(s2, axis=0, keepdims=True)

    def softmax_step(s_ref, smax, g, carry):
        hs = range(2)
        m_i = [carry[2 * hh] for hh in hs]
        m_new = [jnp.maximum(m_i[hh], smax[:, hh * bs:(hh + 1) * bs]) for hh in hs]
        alpha = [jnp.exp2(m_i[hh] - m_new[hh]) for hh in hs]
        pexp = [jnp.exp2((s_ref[:, hh * bs:(hh + 1) * bs] - m_new[hh]).astype(BF16)) for hh in hs]
        acc_new = [carry[2 * hh + 1] * alpha[hh]
                   + _dot(jnp.concatenate([values_t(g, hh), ones_rows], axis=0), pexp[hh])
                   for hh in hs]
        return (m_new[0], acc_new[0], m_new[1], acc_new[1])

    rel = (lax.broadcasted_iota(jnp.int32, (sup, 2 * bs), 0) - (own - grp * MB_SUPER) * bs)
    qi = lax.broadcasted_iota(jnp.int32, (sup, 2 * bs), 1) & (bs - 1)
    future = (rel > qi) & (rel < bs)
    smax_a0 = produce(sa_ref, jnp.where(future, NEG_BIG, scores(grp)))

    def pair(u, carry):
        t = 2 * u
        smax_a, carry = carry[0], carry[1:]
        smax_b = produce(sb_ref, scores(group_at(t + 1)))
        carry = softmax_step(sa_ref, smax_a, group_at(t), carry)
        smax_a = produce(sa_ref, scores(group_at(t + 2)))
        return (smax_a,) + softmax_step(sb_ref, smax_b, group_at(t + 1), carry)

    stat0 = jnp.full((1, bs), -jnp.inf, F32)
    acc0 = jnp.zeros((hd + MB_SUM_ROWS, bs), F32)
    fin = lax.fori_loop(0, grp // 2 + 1, pair, (smax_a0, stat0, acc0, stat0, acc0))
    out_t = jnp.concatenate([fin[2][:hd] / fin[2][hd:hd + 1], fin[4][:hd] / fin[4][hd:hd + 1]],
                            axis=0)
    o_ref[0] = out_t.T


def _moba(qt, k, vt):
    bsz, t_len, _ = k.shape
    bs = MB_BLOCK
    n_blk = t_len // bs
    assert n_blk % (2 * MB_SUPER) == 0 and MB_AUX_MASK + n_blk <= LANES
    return pl.pallas_call(
        functools.partial(_moba_kernel, n_blk=n_blk),
        grid=(bsz, MB_PAIRS, n_blk),
        in_specs=[
            pl.BlockSpec((1, LANES, bs), lambda b, p, i: (b * n_blk + i, p, 0)),
            pl.BlockSpec((1, t_len, LANES), lambda b, p, i: (b, 0, p)),
            pl.BlockSpec((n_blk, LANES, bs), lambda b, p, i: (b, p, 0)),
        ],
        out_specs=pl.BlockSpec((1, bs, LANES), lambda b, p, i: (b, i, p)),
        out_shape=jax.ShapeDtypeStruct((bsz, t_len, MB_WIDTH), F32),
        scratch_shapes=[
            pltpu.VMEM((t_len, 2 * LANES), BF16),
            pltpu.VMEM((LANES, LANES), F32),
            pltpu.VMEM((MB_SUPER * bs, 2 * bs), F32),
            pltpu.VMEM((MB_SUPER * bs, 2 * bs), F32),
        ],
        compiler_params=pltpu.CompilerParams(
            dimension_semantics=("arbitrary", "arbitrary", "arbitrary"),
            vmem_limit_bytes=VMEM_LIMIT),
        name="moba",
    )(qt, k, vt)


def _mixout_kernel(x_ref, odn_ref, omb_ref, prew_ref, wz_ref, wgd_ref, wgm_ref,
                   wbd_ref, wbm_ref, wo_ref, postw_ref, o_ref):
    x = x_ref[...]
    h = _rms(x, prew_ref[...]).astype(BF16)
    z = _dot(h, wz_ref[...])
    gate_dn = jax.nn.sigmoid(_dot(h, wgd_ref[...]))
    gate_mb = jax.nn.sigmoid(_dot(h, wgm_ref[...]))
    o_dn = odn_ref[...] * (z * jax.nn.sigmoid(z))
    y_dn = _dot(o_dn.astype(BF16), wbd_ref[...])
    y_mb = _dot(omb_ref[...].astype(BF16), wbm_ref[...])
    merged = gate_dn * y_dn + gate_mb * y_mb
    y = _dot(merged.astype(BF16), wo_ref[...])
    o_ref[...] = x + _rms(y, postw_ref[...])


def _mix_out(x, o_dn, o_mb, pre_w, w_z, w_gd, w_gm, w_bd, w_bm, w_o, post_w, tm=512):
    n, d = x.shape
    return pl.pallas_call(
        _mixout_kernel,
        grid=(n // tm,),
        in_specs=[
            pl.BlockSpec((tm, d), lambda i: (i, 0)),
            pl.BlockSpec((tm, o_dn.shape[1]), lambda i: (i, 0)),
            pl.BlockSpec((tm, o_mb.shape[1]), lambda i: (i, 0)),
            _const_spec((1, d)),
            _const_spec(w_z.shape),
            _const_spec(w_gd.shape),
            _const_spec(w_gm.shape),
            _const_spec(w_bd.shape),
            _const_spec(w_bm.shape),
            _const_spec(w_o.shape),
            _const_spec((1, d)),
        ],
        out_specs=pl.BlockSpec((tm, d), lambda i: (i, 0)),
        out_shape=jax.ShapeDtypeStruct((n, d), F32),
        compiler_params=pltpu.CompilerParams(
            dimension_semantics=("arbitrary",), vmem_limit_bytes=VMEM_LIMIT),
        name="mix_out",
    )(x, o_dn, o_mb, pre_w, w_z, w_gd, w_gm, w_bd, w_bm, w_o, post_w)


def _layer(x, ffn1_pre_w, ffn1_w_gate, ffn1_w_up, ffn1_w_down, ffn1_post_w,
           mix_pre_w, w_in, dn_conv_w, dn_a_log, dn_dt_bias, dn_norm_w,
           w_branch_dn, w_branch_mb, w_out, mix_post_w,
           ffn2_pre_w, ffn2_w_gate, ffn2_w_up, ffn2_w_down, ffn2_post_w):
    bsz, t_len, d = x.shape
    n = bsz * t_len
    row = lambda w: w.reshape(1, -1).astype(F32)
    b16 = lambda w: w.astype(BF16)

    x = x.reshape(n, d)
    x = _ffn_block(x, row(ffn1_pre_w), b16(ffn1_w_gate), b16(ffn1_w_up), b16(ffn1_w_down),
                   row(ffn1_post_w))

    o = 0
    w_dn = w_in[:, o:o + 3 * DN_WIDTH]; o += 3 * DN_WIDTH
    w_z = w_in[:, o:o + DN_WIDTH]; o += DN_WIDTH
    w_ba = w_in[:, o:o + 2 * DN_HEADS]; o += 2 * DN_HEADS
    w_q = w_in[:, o:o + MB_WIDTH]; o += MB_WIDTH
    w_k = w_in[:, o:o + MB_WIDTH]; o += MB_WIDTH
    w_v = w_in[:, o:o + MB_WIDTH]; o += MB_WIDTH
    w_gd = w_in[:, o:o + d]; o += d
    w_gm = w_in[:, o:o + d]; o += d
    w_ba = jnp.pad(w_ba, ((0, 0), (0, LANES - 2 * DN_HEADS)))

    def hi_lo(w):
        hi = w.astype(BF16)
        return jnp.stack([hi, (w - hi.astype(F32)).astype(BF16)])

    dn_qkv, ba, mb_k, mb_qt, mb_vt = _in_proj(x, row(mix_pre_w), b16(w_dn), hi_lo(w_ba), hi_lo(w_k),
                                              hi_lo(w_q.T), b16(w_v.T))

    pad_heads = lambda p: jnp.pad(p.astype(F32), (DN_HEADS, LANES - 2 * DN_HEADS)).reshape(1, LANES)
    o_dn = _deltanet(dn_qkv.reshape(bsz, t_len, -1), ba.reshape(bsz, t_len, LANES),
                     dn_conv_w.astype(F32), pad_heads(dn_a_log), pad_heads(dn_dt_bias),
                     row(dn_norm_w))

    o_mb = _moba(mb_qt, mb_k.reshape(bsz, t_len, MB_WIDTH), mb_vt)

    x = _mix_out(x, o_dn.reshape(n, DN_WIDTH), o_mb.reshape(n, MB_WIDTH), row(mix_pre_w),
                 b16(w_z), b16(w_gd), b16(w_gm), b16(w_branch_dn), b16(w_branch_mb), b16(w_out),
                 row(mix_post_w))

    x = _ffn_block(x, row(ffn2_pre_w), b16(ffn2_w_gate), b16(ffn2_w_up), b16(ffn2_w_down),
                   row(ffn2_post_w))
    return x.reshape(bsz, t_len, d)


def kernel(x, ffn1_pre_w, ffn1_w_gate, ffn1_w_up, ffn1_w_down, ffn1_post_w, mix_pre_w, w_in, dn_conv_w, dn_a_log, dn_dt_bias, dn_norm_w, w_branch_dn, w_branch_mb, w_out, mix_post_w, ffn2_pre_w, ffn2_w_gate, ffn2_w_up, ffn2_w_down, ffn2_post_w):
    depth = w_in.shape[0]
    for l in range(depth):
        x = _layer(x, ffn1_pre_w[l], ffn1_w_gate[l], ffn1_w_up[l], ffn1_w_down[l], ffn1_post_w[l],
                   mix_pre_w[l], w_in[l], dn_conv_w[l], dn_a_log[l], dn_dt_bias[l], dn_norm_w[l],
                   w_branch_dn[l], w_branch_mb[l], w_out[l], mix_post_w[l],
                   ffn2_pre_w[l], ffn2_w_gate[l], ffn2_w_up[l], ffn2_w_down[l], ffn2_post_w[l])
    return x
```

```python
import functools
import math

import jax
import jax.numpy as jnp
from jax import lax
from jax.experimental import pallas as pl
from jax.experimental.pallas import tpu as pltpu

F32 = jnp.float32
BF16 = jnp.bfloat16

NORM_EPS = 1e-6
MACARON_WEIGHT = 0.5

DN_HEADS = 4
DN_HEAD_DIM = 128
DN_WIDTH = DN_HEADS * DN_HEAD_DIM
DN_CONV = 4
DN_CHUNK = 64
DN_TILE = 256

MB_HEADS = 8
MB_HEAD_DIM = 64
MB_WIDTH = MB_HEADS * MB_HEAD_DIM
MB_BLOCK = 256
MB_TOPK = 3
ALIBI_MAX_BIAS = 8.0
LANES = 128
SUBLANES = 8
MB_PAIRS = MB_WIDTH // LANES
NEG_BIG = -1e30

VMEM_LIMIT = 56 * 1024 * 1024


def _rms(x, w):
    ms = jnp.mean(x * x, axis=-1, keepdims=True)
    return x * lax.rsqrt(ms + NORM_EPS) * w


def _dot(a, b):
    return jnp.dot(a, b, preferred_element_type=F32)


def _dot_nt(a, b):
    return lax.dot_general(a, b, (((1,), (1,)), ((), ())), preferred_element_type=F32)


def _dot_tn(a, b):
    return lax.dot_general(a, b, (((0,), (0,)), ((), ())), preferred_element_type=F32)


def _split2(x):
    hi = x.astype(BF16)
    lo = (x - hi.astype(F32)).astype(BF16)
    return hi, lo


def _const_spec(shape):
    nd = len(shape)
    return pl.BlockSpec(shape, lambda *_: (0,) * nd, pipeline_mode=pl.Buffered(1))


def _ffn_kernel(x_ref, prew_ref, wg_ref, wu_ref, wd_ref, postw_ref, o_ref):
    x = x_ref[...]
    xn = _rms(x, prew_ref[...]).astype(BF16)
    g = _dot(xn, wg_ref[...])
    u = _dot(xn, wu_ref[...])
    a = (g * jax.nn.sigmoid(g) * u).astype(BF16)
    h = _dot(a, wd_ref[...])
    o_ref[...] = x + MACARON_WEIGHT * _rms(h, postw_ref[...])


def _ffn_block(x, pre_w, w_gate, w_up, w_down, post_w, tm=512):
    n, d = x.shape
    dff = w_gate.shape[1]
    return pl.pallas_call(
        _ffn_kernel,
        grid=(n // tm,),
        in_specs=[
            pl.BlockSpec((tm, d), lambda i: (i, 0)),
            _const_spec((1, d)),
            _const_spec((d, dff)),
            _const_spec((d, dff)),
            _const_spec((dff, d)),
            _const_spec((1, d)),
        ],
        out_specs=pl.BlockSpec((tm, d), lambda i: (i, 0)),
        out_shape=jax.ShapeDtypeStruct((n, d), F32),
        compiler_params=pltpu.CompilerParams(
            dimension_semantics=("arbitrary",), vmem_limit_bytes=VMEM_LIMIT),
        name="ffn_block",
    )(x, pre_w, w_gate, w_up, w_down, post_w)


def _inproj_kernel(x_ref, prew_ref, wdn_ref, wba_ref, wk_ref, wqt_ref, wvt_ref,
                   dn_ref, ba_ref, k_ref, kmean_ref, qt_ref, vt_ref):
    bs = MB_BLOCK
    nb = qt_ref.shape[0]
    h = _rms(x_ref[...], prew_ref[...])
    h_hi, h_lo = _split2(h)
    dn_ref[...] = _dot(h_hi, wdn_ref[...])
    ba_ref[...] = _dot(h_hi, wba_ref[0]) + _dot(h_lo, wba_ref[0]) + _dot(h_hi, wba_ref[1])
    k_ref[...] = _dot(h_hi, wk_ref[0]).astype(BF16)
    hbar = jnp.concatenate([jnp.mean(h[i * bs:(i + 1) * bs], axis=0, keepdims=True)
                            for i in range(nb)]
                           + [jnp.zeros((SUBLANES - nb, h.shape[1]), F32)], axis=0)
    hb_hi, hb_lo = _split2(hbar)
    kmean = _dot(hb_hi, wk_ref[0]) + _dot(hb_lo, wk_ref[0]) + _dot(hb_hi, wk_ref[1])
    kmean_ref[0] = kmean[:nb]
    qt = _dot_nt(wqt_ref[0], h_hi) + _dot_nt(wqt_ref[0], h_lo) + _dot_nt(wqt_ref[1], h_hi)
    vt = _dot_nt(wvt_ref[...], h_hi).astype(BF16)
    for i in range(nb):
        qt_ref[i] = qt[:, i * bs:(i + 1) * bs]
        vt_ref[i] = vt[:, i * bs:(i + 1) * bs]


def _in_proj(x, pre_w, w_dn, w_ba, w_k, w_qt, w_vt, tm=512):
    n, d = x.shape
    bs = MB_BLOCK
    return pl.pallas_call(
        _inproj_kernel,
        grid=(n // tm,),
        in_specs=[
            pl.BlockSpec((tm, d), lambda i: (i, 0)),
            _const_spec((1, d)),
            _const_spec(w_dn.shape),
            _const_spec(w_ba.shape),
            _const_spec(w_k.shape),
            _const_spec(w_qt.shape),
            _const_spec(w_vt.shape),
        ],
        out_specs=[
            pl.BlockSpec((tm, w_dn.shape[1]), lambda i: (i, 0)),
            pl.BlockSpec((tm, LANES), lambda i: (i, 0)),
            pl.BlockSpec((tm, MB_WIDTH), lambda i: (i, 0)),
            pl.BlockSpec((1, tm // bs, MB_WIDTH), lambda i: (i, 0, 0)),
            pl.BlockSpec((tm // bs, MB_WIDTH, bs), lambda i: (i, 0, 0)),
            pl.BlockSpec((tm // bs, MB_WIDTH, bs), lambda i: (i, 0, 0)),
        ],
        out_shape=[
            jax.ShapeDtypeStruct((n, w_dn.shape[1]), F32),
            jax.ShapeDtypeStruct((n, LANES), F32),
            jax.ShapeDtypeStruct((n, MB_WIDTH), BF16),
            jax.ShapeDtypeStruct((n // tm, tm // bs, MB_WIDTH), F32),
            jax.ShapeDtypeStruct((n // bs, MB_WIDTH, bs), F32),
            jax.ShapeDtypeStruct((n // bs, MB_WIDTH, bs), BF16),
        ],
        compiler_params=pltpu.CompilerParams(
            dimension_semantics=("arbitrary",), vmem_limit_bytes=VMEM_LIMIT),
        name="in_proj",
    )(x, pre_w, w_dn, w_ba, w_k, w_qt, w_vt)


def _dn_kernel(qkv_ref, ba_ref, convw_ref, alog_ref, dtb_ref, normw_ref, o_ref,
               xbuf_ref, state_ref):
    tt = DN_TILE
    c = DN_CHUNK
    dk = DN_HEAD_DIM
    t = pl.program_id(1)

    @pl.when(t == 0)
    def _():
        xbuf_ref[0:8, :] = jnp.zeros((8, 3 * DN_WIDTH), F32)
        state_ref[...] = jnp.zeros_like(state_ref)

    x = qkv_ref[0]
    xbuf_ref[8:8 + tt, :] = x
    cw = convw_ref[...]
    y = x * cw[DN_CONV - 1:DN_CONV, :]
    for s in range(1, DN_CONV):
        y = y + xbuf_ref[8 - s:8 - s + tt, :] * cw[DN_CONV - 1 - s:DN_CONV - s, :]
    xbuf_ref[0:8, :] = x[tt - 8:tt, :]
    y = y * jax.nn.sigmoid(y)

    ba = ba_ref[0]
    beta_all = jax.nn.sigmoid(ba)
    g_all = -jnp.exp(alog_ref[...]) * jax.nn.softplus(ba + dtb_ref[...])

    ri = lax.broadcasted_iota(jnp.int32, (tt, tt), 0)
    ci = lax.broadcasted_iota(jnp.int32, (tt, tt), 1)
    same_chunk = (ri // c) == (ci // c)
    incl = same_chunk & (ri >= ci)
    eye = ri == ci
    tril = jnp.where(incl, 1.0, 0.0).astype(BF16)
    ones_bd = jnp.where(same_chunk, 1.0, 0.0).astype(BF16)

    g1 = g_all.astype(BF16)
    r1 = g_all - g1.astype(F32)
    g2 = r1.astype(BF16)
    g3 = (r1 - g2.astype(F32)).astype(BF16)
    gcs_all = _dot(tril, g1) + _dot(tril, g2) + _dot(tril, g3)
    gtot_all = _dot(ones_bd, g1) + _dot(ones_bd, g2) + _dot(ones_bd, g3)

    heads = range(DN_HEADS)
    lmat, attn16, rhs, qd, kd, gtot = [], [], [], [], [], []
    for h in heads:
        qr = y[:, h * dk:(h + 1) * dk]
        kr = y[:, DN_WIDTH + h * dk:DN_WIDTH + (h + 1) * dk]
        v = y[:, 2 * DN_WIDTH + h * dk:2 * DN_WIDTH + (h + 1) * dk]
        q = qr * lax.rsqrt(jnp.sum(qr * qr, axis=-1, keepdims=True) + NORM_EPS) * (dk ** -0.5)
        k = kr * lax.rsqrt(jnp.sum(kr * kr, axis=-1, keepdims=True) + NORM_EPS)
        beta = beta_all[:, h:h + 1]
        gcs = gcs_all[:, DN_HEADS + h:DN_HEADS + h + 1]
        gtot.append(gtot_all[:, DN_HEADS + h:DN_HEADS + h + 1])
        eg = jnp.exp(gcs)

        g_row = jnp.sum(jnp.where(eye, gcs, 0.0), axis=0, keepdims=True)
        decay = jnp.exp(jnp.where(incl, gcs - g_row, NEG_BIG))

        kb = k * beta
        k16 = k.astype(BF16)
        lmat.append(jnp.where(eye, 0.0, _dot_nt(kb.astype(BF16), k16) * decay))
        attn16.append((_dot_nt(q.astype(BF16), k16) * decay).astype(BF16))
        rhs.append(jnp.concatenate([v * beta, kb * eg], axis=1).astype(BF16))
        qd.append(q * eg)
        kd.append((k * jnp.exp(gtot[h] - gcs)).astype(BF16))

    xinv = [jnp.where(eye, 1.0, -lmat[h]) for h in heads]
    m = [lmat[h].astype(BF16) for h in heads]
    power = 2
    while power < c:
        m = [_dot(m[h], m[h]).astype(BF16) for h in heads]
        xinv = [xinv[h] + _dot(xinv[h].astype(BF16), m[h]) for h in heads]
        power *= 2

    uw16 = [_dot(xinv[h].astype(BF16), rhs[h]).astype(BF16) for h in heads]
    au_aw = [_dot(attn16[h], uw16[h]) for h in heads]
    au = [au_aw[h][:, :dk] for h in heads]
    e16 = [(qd[h] - au_aw[h][:, dk:]).astype(BF16) for h in heads]

    s = [state_ref[h] for h in heads]
    outs = [[] for _ in heads]
    for ch in range(tt // c):
        lo, hi = ch * c, (ch + 1) * c
        bc = [_dot_tn(kd[h][lo:hi], uw16[h][lo:hi]) for h in heads]
        for h in heads:
            s16 = s[h].astype(BF16)
            outs[h].append(_dot(e16[h][lo:hi], s16) + au[h][lo:hi])
            s[h] = (s[h] * jnp.exp(gtot[h][lo:lo + 1, :]) + bc[h][:, :dk]
                    - _dot(bc[h][:, dk:].astype(BF16), s16))
    for h in heads:
        state_ref[h] = s[h]
        o = jnp.concatenate(outs[h], axis=0)
        o_ref[0, :, h * dk:(h + 1) * dk] = _rms(o, normw_ref[...])


def _deltanet(qkv, ba, conv_w, alog_row, dtb_row, norm_w):
    bsz, t_len, width = qkv.shape
    tt = DN_TILE
    return pl.pallas_call(
        _dn_kernel,
        grid=(bsz, t_len // tt),
        in_specs=[
            pl.BlockSpec((1, tt, width), lambda b, t: (b, t, 0)),
            pl.BlockSpec((1, tt, LANES), lambda b, t: (b, t, 0)),
            _const_spec(conv_w.shape),
            _const_spec((1, LANES)),
            _const_spec((1, LANES)),
            _const_spec((1, DN_HEAD_DIM)),
        ],
        out_specs=pl.BlockSpec((1, tt, DN_WIDTH), lambda b, t: (b, t, 0)),
        out_shape=jax.ShapeDtypeStruct((bsz, t_len, DN_WIDTH), F32),
        scratch_shapes=[
            pltpu.VMEM((8 + tt, width), F32),
            pltpu.VMEM((DN_HEADS, DN_HEAD_DIM, DN_HEAD_DIM), F32),
        ],
        compiler_params=pltpu.CompilerParams(
            dimension_semantics=("arbitrary", "arbitrary"), vmem_limit_bytes=VMEM_LIMIT),
        name="deltanet",
    )(qkv, ba, conv_w, alog_row, dtb_row, norm_w)


MB_SUPER = 2
MB_AUX_MASK = 16
MB_SUM_ROWS = 16
ALIBI_STEP = int(ALIBI_MAX_BIAS) // MB_HEADS
assert ALIBI_STEP * MB_HEADS == ALIBI_MAX_BIAS
LOG2E = math.log2(math.e)
LOG2E_PIECES = (1.4453125, -0.00262451171875, 7.063150405883789e-06, -1.05355866253376e-08)


def _moba_kernel(qt_ref, k_ref, kmean_ref, vt_ref, o_ref, kaug_ref, sa_ref, sb_ref, *, n_blk):
    bs = MB_BLOCK
    hd = MB_HEAD_DIM
    sup = MB_SUPER * bs
    nbp = -(-n_blk // SUBLANES) * SUBLANES
    p = pl.program_id(1)
    own = pl.program_id(2)
    lane = lax.broadcasted_iota(jnp.int32, (bs, LANES), 1)
    row = lax.broadcasted_iota(jnp.int32, (bs, LANES), 0)

    @pl.when(own == 0)
    def _():
        def build(j, carry):
            off = pl.multiple_of(j * bs, bs)
            kaug_ref[pl.ds(off, bs), 0:LANES] = k_ref[0, pl.ds(off, bs), :]
            kstart = jnp.full((bs, LANES), j * bs, jnp.int32).astype(F32)
            aux = jnp.where(lane < 2, 1.0,
                            jnp.where(lane < 6, row.astype(F32),
                                      jnp.where(lane < 10, kstart,
                                                jnp.where(lane == MB_AUX_MASK + j, 1.0, 0.0))))
            kaug_ref[pl.ds(off, bs), LANES:2 * LANES] = aux.astype(BF16)
            return carry

        lax.fori_loop(0, n_blk, build, 0)

    qt = qt_ref[0]
    km_hi, km_lo = _split2(kmean_ref[0])

    chan = lax.broadcasted_iota(jnp.int32, (LANES, bs), 0)
    blk = lax.broadcasted_iota(jnp.int32, (nbp, bs), 0)
    blk_f = blk.astype(F32)
    aux_row = lax.broadcasted_iota(jnp.int32, (MB_AUX_MASK, bs), 0)
    qpos = (lax.broadcasted_iota(jnp.int32, (MB_AUX_MASK, bs), 1) + own * bs).astype(F32)
    aux_pad = jnp.zeros((LANES - MB_AUX_MASK - nbp, bs), F32)

    qaug = []
    for hh in range(2):
        qth = jnp.where((chan >= hh * hd) & (chan < (hh + 1) * hd), qt, 0.0)
        slope_bits = (127 - ALIBI_STEP * (2 * p + hh + 1)) << 23
        slope = lax.bitcast_convert_type(jnp.full((MB_AUX_MASK, bs), slope_bits, jnp.int32), F32)

        q_hi, q_lo = _split2(qth)
        gate = _dot(km_hi, q_hi) + _dot(km_hi, q_lo) + _dot(km_lo, q_hi)
        gate = jnp.where(blk < own, gate, -jnp.inf)
        sel = jnp.zeros((nbp, bs), F32)
        for _ in range(MB_TOPK):
            mx = jnp.max(gate, axis=0, keepdims=True)
            first = jnp.min(jnp.where(gate == mx, blk_f, float(nbp)), axis=0, keepdims=True)
            hit = blk_f == first
            sel = jnp.where(hit, 1.0, sel)
            gate = jnp.where(hit, -jnp.inf, gate)
        keep = jnp.where(blk < own, sel, jnp.where(blk == own, 1.0, 0.0))
        mask_rows = jnp.where(keep > 0.5, 0.0, NEG_BIG)
        qconst = -(slope * LOG2E) * qpos
        qconst_hi = qconst.astype(BF16).astype(F32)
        piece_id = (aux_row + 2) & 3
        piece = jnp.where(piece_id == 0, LOG2E_PIECES[0],
                          jnp.where(piece_id == 1, LOG2E_PIECES[1],
                                    jnp.where(piece_id == 2, LOG2E_PIECES[2], LOG2E_PIECES[3])))
        bias_rows = jnp.where(aux_row == 0, qconst_hi,
                              jnp.where(aux_row == 1, qconst - qconst_hi,
                                        jnp.where(aux_row < 10, slope * piece, 0.0)))
        qaug.append(jnp.concatenate([qth * (hd ** -0.5 * LOG2E), bias_rows, mask_rows, aux_pad],
                                    axis=0).astype(BF16))

    def keys(i):
        return kaug_ref[pl.ds(pl.multiple_of(i * sup, sup), sup), :]

    def values_t(i, hh):
        return jnp.concatenate([vt_ref[i * MB_SUPER + u, hh * hd:(hh + 1) * hd, :]
                                for u in range(MB_SUPER)], axis=1)

    n_grp = n_blk // MB_SUPER
    grp = own // MB_SUPER
    qaug2 = jnp.concatenate(qaug, axis=1)

    def group_at(t):
        g = jnp.where(t == 0, grp, jnp.where(t > grp, grp + 1, t - 1))
        return jnp.minimum(g, n_grp - 1)

    def scores(g):
        return _dot(keys(g), qaug2)

    ones_rows = jnp.ones((MB_SUM_ROWS, sup), BF16)

    def produce(s_ref, s2):
        s_ref[...] = s2
        return jnp.max(s2, axis=0, keepdims=True)

    def softmax_step(s_ref, smax, g, carry):
        hs = range(2)
        m_i = [carry[2 * hh] for hh in hs]
        m_new = [jnp.maximum(m_i[hh], smax[:, hh * bs:(hh + 1) * bs]) for hh in hs]
        alpha = [jnp.exp2(m_i[hh] - m_new[hh]) for hh in hs]
        pexp = [jnp.exp2((s_ref[:, hh * bs:(hh + 1) * bs] - m_new[hh]).astype(BF16)) for hh in hs]
        acc_new = [carry[2 * hh + 1] * alpha[hh]
                   + _dot(jnp.concatenate([values_t(g, hh), ones_rows], axis=0), pexp[hh])
                   for hh in hs]
        return (m_new[0], acc_new[0], m_new[1], acc_new[1])

    rel = (lax.broadcasted_iota(jnp.int32, (sup, 2 * bs), 0) - (own - grp * MB_SUPER) * bs)
    qi = lax.broadcasted_iota(jnp.int32, (sup, 2 * bs), 1) & (bs - 1)
    future = (rel > qi) & (rel < bs)
    smax_a0 = produce(sa_ref, jnp.where(future, NEG_BIG, scores(grp)))

    def pair(u, carry):
        t = 2 * u
        smax_a, carry = carry[0], carry[1:]
        smax_b = produce(sb_ref, scores(group_at(t + 1)))
        carry = softmax_step(sa_ref, smax_a, group_at(t), carry)
        smax_a = produce(sa_ref, scores(group_at(t + 2)))
        return (smax_a,) + softmax_step(sb_ref, smax_b, group_at(t + 1), carry)

    stat0 = jnp.full((1, bs), -jnp.inf, F32)
    acc0 = jnp.zeros((hd + MB_SUM_ROWS, bs), F32)
    fin = lax.fori_loop(0, grp // 2 + 1, pair, (smax_a0, stat0, acc0, stat0, acc0))
    out_t = jnp.concatenate([fin[2][:hd] / fin[2][hd:hd + 1], fin[4][:hd] / fin[4][hd:hd + 1]],
                            axis=0)
    o_ref[0] = out_t.T


def _moba(qt, k, kmean, vt):
    bsz, t_len, _ = k.shape
    bs = MB_BLOCK
    n_blk = t_len // bs
    assert n_blk % (2 * MB_SUPER) == 0 and MB_AUX_MASK + n_blk <= LANES
    assert n_blk % SUBLANES == 0
    return pl.pallas_call(
        functools.partial(_moba_kernel, n_blk=n_blk),
        grid=(bsz, MB_PAIRS, n_blk),
        in_specs=[
            pl.BlockSpec((1, LANES, bs), lambda b, p, i: (b * n_blk + i, p, 0)),
            pl.BlockSpec((1, t_len, LANES), lambda b, p, i: (b, 0, p)),
            pl.BlockSpec((1, n_blk, LANES), lambda b, p, i: (b, 0, p)),
            pl.BlockSpec((n_blk, LANES, bs), lambda b, p, i: (b, p, 0)),
        ],
        out_specs=pl.BlockSpec((1, bs, LANES), lambda b, p, i: (b, i, p)),
        out_shape=jax.ShapeDtypeStruct((bsz, t_len, MB_WIDTH), F32),
        scratch_shapes=[
            pltpu.VMEM((t_len, 2 * LANES), BF16),
            pltpu.VMEM((MB_SUPER * bs, 2 * bs), F32),
            pltpu.VMEM((MB_SUPER * bs, 2 * bs), F32),
        ],
        compiler_params=pltpu.CompilerParams(
            dimension_semantics=("arbitrary", "arbitrary", "arbitrary"),
            vmem_limit_bytes=VMEM_LIMIT),
        name="moba",
    )(qt, k, kmean, vt)


def _mixout_kernel(x_ref, odn_ref, omb_ref, prew_ref, wz_ref, wgd_ref, wgm_ref,
                   wbd_ref, wbm_ref, wo_ref, postw_ref, o_ref):
    x = x_ref[...]
    h = _rms(x, prew_ref[...]).astype(BF16)
    z = _dot(h, wz_ref[...])
    gate_dn = jax.nn.sigmoid(_dot(h, wgd_ref[...]))
    gate_mb = jax.nn.sigmoid(_dot(h, wgm_ref[...]))
    o_dn = odn_ref[...] * (z * jax.nn.sigmoid(z))
    y_dn = _dot(o_dn.astype(BF16), wbd_ref[...])
    y_mb = _dot(omb_ref[...].astype(BF16), wbm_ref[...])
    merged = gate_dn * y_dn + gate_mb * y_mb
    y = _dot(merged.astype(BF16), wo_ref[...])
    o_ref[...] = x + _rms(y, postw_ref[...])


def _mix_out(x, o_dn, o_mb, pre_w, w_z, w_gd, w_gm, w_bd, w_bm, w_o, post_w, tm=512):
    n, d = x.shape
    return pl.pallas_call(
        _mixout_kernel,
        grid=(n // tm,),
        in_specs=[
            pl.BlockSpec((tm, d), lambda i: (i, 0)),
            pl.BlockSpec((tm, o_dn.shape[1]), lambda i: (i, 0)),
            pl.BlockSpec((tm, o_mb.shape[1]), lambda i: (i, 0)),
            _const_spec((1, d)),
            _const_spec(w_z.shape),
            _const_spec(w_gd.shape),
            _const_spec(w_gm.shape),
            _const_spec(w_bd.shape),
            _const_spec(w_bm.shape),
            _const_spec(w_o.shape),
            _const_spec((1, d)),
        ],
        out_specs=pl.BlockSpec((tm, d), lambda i: (i, 0)),
        out_shape=jax.ShapeDtypeStruct((n, d), F32),
        compiler_params=pltpu.CompilerParams(
            dimension_semantics=("arbitrary",), vmem_limit_bytes=VMEM_LIMIT),
        name="mix_out",
    )(x, o_dn, o_mb, pre_w, w_z, w_gd, w_gm, w_bd, w_bm, w_o, post_w)


def _layer(x, ffn1_pre_w, ffn1_w_gate, ffn1_w_up, ffn1_w_down, ffn1_post_w,
           mix_pre_w, w_in, dn_conv_w, dn_a_log, dn_dt_bias, dn_norm_w,
           w_branch_dn, w_branch_mb, w_out, mix_post_w,
           ffn2_pre_w, ffn2_w_gate, ffn2_w_up, ffn2_w_down, ffn2_post_w):
    bsz, t_len, d = x.shape
    n = bsz * t_len
    row = lambda w: w.reshape(1, -1).astype(F32)
    b16 = lambda w: w.astype(BF16)

    x = x.reshape(n, d)
    x = _ffn_block(x, row(ffn1_pre_w), b16(ffn1_w_gate), b16(ffn1_w_up), b16(ffn1_w_down),
                   row(ffn1_post_w))

    o = 0
    w_dn = w_in[:, o:o + 3 * DN_WIDTH]; o += 3 * DN_WIDTH
    w_z = w_in[:, o:o + DN_WIDTH]; o += DN_WIDTH
    w_ba = w_in[:, o:o + 2 * DN_HEADS]; o += 2 * DN_HEADS
    w_q = w_in[:, o:o + MB_WIDTH]; o += MB_WIDTH
    w_k = w_in[:, o:o + MB_WIDTH]; o += MB_WIDTH
    w_v = w_in[:, o:o + MB_WIDTH]; o += MB_WIDTH
    w_gd = w_in[:, o:o + d]; o += d
    w_gm = w_in[:, o:o + d]; o += d
    w_ba = jnp.pad(w_ba, ((0, 0), (0, LANES - 2 * DN_HEADS)))

    def hi_lo(w):
        hi = w.astype(BF16)
        return jnp.stack([hi, (w - hi.astype(F32)).astype(BF16)])

    dn_qkv, ba, mb_k, mb_kmean, mb_qt, mb_vt = _in_proj(
        x, row(mix_pre_w), b16(w_dn), hi_lo(w_ba), hi_lo(w_k), hi_lo(w_q.T), b16(w_v.T))

    pad_heads = lambda p: jnp.pad(p.astype(F32), (DN_HEADS, LANES - 2 * DN_HEADS)).reshape(1, LANES)
    o_dn = _deltanet(dn_qkv.reshape(bsz, t_len, -1), ba.reshape(bsz, t_len, LANES),
                     dn_conv_w.astype(F32), pad_heads(dn_a_log), pad_heads(dn_dt_bias),
                     row(dn_norm_w))

    o_mb = _moba(mb_qt, mb_k.reshape(bsz, t_len, MB_WIDTH),
                 mb_kmean.reshape(bsz, t_len // MB_BLOCK, MB_WIDTH), mb_vt)

    x = _mix_out(x, o_dn.reshape(n, DN_WIDTH), o_mb.reshape(n, MB_WIDTH), row(mix_pre_w),
                 b16(w_z), b16(w_gd), b16(w_gm), b16(w_branch_dn), b16(w_branch_mb), b16(w_out),
                 row(mix_post_w))

    x = _ffn_block(x, row(ffn2_pre_w), b16(ffn2_w_gate), b16(ffn2_w_up), b16(ffn2_w_down),
                   row(ffn2_post_w))
    return x.reshape(bsz, t_len, d)


def kernel(x, ffn1_pre_w, ffn1_w_gate, ffn1_w_up, ffn1_w_down, ffn1_post_w, mix_pre_w, w_in, dn_conv_w, dn_a_log, dn_dt_bias, dn_norm_w, w_branch_dn, w_branch_mb, w_out, mix_post_w, ffn2_pre_w, ffn2_w_gate, ffn2_w_up, ffn2_w_down, ffn2_post_w):
    depth = w_in.shape[0]
    for l in range(depth):
        x = _layer(x, ffn1_pre_w[l], ffn1_w_gate[l], ffn1_w_up[l], ffn1_w_down[l], ffn1_post_w[l],
                   mix_pre_w[l], w_in[l], dn_conv_w[l], dn_a_log[l], dn_dt_bias[l], dn_norm_w[l],
                   w_branch_dn[l], w_branch_mb[l], w_out[l], mix_post_w[l],
                   ffn2_pre_w[l], ffn2_w_gate[l], ffn2_w_up[l], ffn2_w_down[l], ffn2_post_w[l])
    return x
```

```python
import functools
import math

import jax
import jax.numpy as jnp
from jax import lax
from jax.experimental import pallas as pl
from jax.experimental.pallas import tpu as pltpu

F32 = jnp.float32
BF16 = jnp.bfloat16

NORM_EPS = 1e-6
MACARON_WEIGHT = 0.5

DN_HEADS = 4
DN_HEAD_DIM = 128
DN_WIDTH = DN_HEADS * DN_HEAD_DIM
DN_CONV = 4
DN_CHUNK = 64
DN_TILE = 256

MB_HEADS = 8
MB_HEAD_DIM = 64
MB_WIDTH = MB_HEADS * MB_HEAD_DIM
MB_BLOCK = 256
MB_TOPK = 3
ALIBI_MAX_BIAS = 8.0
LANES = 128
SUBLANES = 8
MB_PAIRS = MB_WIDTH // LANES
NEG_BIG = -1e30

VMEM_LIMIT = 56 * 1024 * 1024


def _rms(x, w):
    ms = jnp.mean(x * x, axis=-1, keepdims=True)
    return x * lax.rsqrt(ms + NORM_EPS) * w


def _dot(a, b):
    return jnp.dot(a, b, preferred_element_type=F32)


def _dot_nt(a, b):
    return lax.dot_general(a, b, (((1,), (1,)), ((), ())), preferred_element_type=F32)


def _dot_tn(a, b):
    return lax.dot_general(a, b, (((0,), (0,)), ((), ())), preferred_element_type=F32)


def _split2(x):
    hi = x.astype(BF16)
    lo = (x - hi.astype(F32)).astype(BF16)
    return hi, lo


def _const_spec(shape):
    nd = len(shape)
    return pl.BlockSpec(shape, lambda *_: (0,) * nd, pipeline_mode=pl.Buffered(1))


def _ffn_kernel(x_ref, prew_ref, wg_ref, wu_ref, wd_ref, postw_ref, o_ref):
    x = x_ref[...]
    xn = _rms(x, prew_ref[...]).astype(BF16)
    g = _dot(xn, wg_ref[...])
    u = _dot(xn, wu_ref[...])
    a = (g * jax.nn.sigmoid(g) * u).astype(BF16)
    h = _dot(a, wd_ref[...])
    o_ref[...] = x + MACARON_WEIGHT * _rms(h, postw_ref[...])


def _ffn_block(x, pre_w, w_gate, w_up, w_down, post_w, tm=512):
    n, d = x.shape
    dff = w_gate.shape[1]
    return pl.pallas_call(
        _ffn_kernel,
        grid=(n // tm,),
        in_specs=[
            pl.BlockSpec((tm, d), lambda i: (i, 0)),
            _const_spec((1, d)),
            _const_spec((d, dff)),
            _const_spec((d, dff)),
            _const_spec((dff, d)),
            _const_spec((1, d)),
        ],
        out_specs=pl.BlockSpec((tm, d), lambda i: (i, 0)),
        out_shape=jax.ShapeDtypeStruct((n, d), F32),
        compiler_params=pltpu.CompilerParams(
            dimension_semantics=("arbitrary",), vmem_limit_bytes=VMEM_LIMIT),
        name="ffn_block",
    )(x, pre_w, w_gate, w_up, w_down, post_w)


def _inproj_kernel(x_ref, prew_ref, wdn_ref, wba_ref, wk_ref, wqt_ref, wvt_ref,
                   dn_ref, ba_ref, k_ref, kmean_ref, qt_ref, vt_ref):
    bs = MB_BLOCK
    nb = qt_ref.shape[0]
    h = _rms(x_ref[...], prew_ref[...])
    h_hi, h_lo = _split2(h)
    dn_ref[...] = _dot(h_hi, wdn_ref[...])
    ba_ref[...] = _dot(h_hi, wba_ref[0]) + _dot(h_lo, wba_ref[0]) + _dot(h_hi, wba_ref[1])
    k_ref[...] = _dot(h_hi, wk_ref[0]).astype(BF16)
    hbar = jnp.concatenate([jnp.mean(h[i * bs:(i + 1) * bs], axis=0, keepdims=True)
                            for i in range(nb)]
                           + [jnp.zeros((SUBLANES - nb, h.shape[1]), F32)], axis=0)
    hb_hi, hb_lo = _split2(hbar)
    kmean = _dot(hb_hi, wk_ref[0]) + _dot(hb_lo, wk_ref[0]) + _dot(hb_hi, wk_ref[1])
    kmean_ref[0] = kmean[:nb]
    qt = _dot_nt(wqt_ref[0], h_hi) + _dot_nt(wqt_ref[0], h_lo) + _dot_nt(wqt_ref[1], h_hi)
    vt = _dot_nt(wvt_ref[...], h_hi).astype(BF16)
    for i in range(nb):
        qt_ref[i] = qt[:, i * bs:(i + 1) * bs]
        vt_ref[i] = vt[:, i * bs:(i + 1) * bs]


def _in_proj(x, pre_w, w_dn, w_ba, w_k, w_qt, w_vt, tm=512):
    n, d = x.shape
    bs = MB_BLOCK
    return pl.pallas_call(
        _inproj_kernel,
        grid=(n // tm,),
        in_specs=[
            pl.BlockSpec((tm, d), lambda i: (i, 0)),
            _const_spec((1, d)),
            _const_spec(w_dn.shape),
            _const_spec(w_ba.shape),
            _const_spec(w_k.shape),
            _const_spec(w_qt.shape),
            _const_spec(w_vt.shape),
        ],
        out_specs=[
            pl.BlockSpec((tm, w_dn.shape[1]), lambda i: (i, 0)),
            pl.BlockSpec((tm, LANES), lambda i: (i, 0)),
            pl.BlockSpec((tm, MB_WIDTH), lambda i: (i, 0)),
            pl.BlockSpec((1, tm // bs, MB_WIDTH), lambda i: (i, 0, 0)),
            pl.BlockSpec((tm // bs, MB_WIDTH, bs), lambda i: (i, 0, 0)),
            pl.BlockSpec((tm // bs, MB_WIDTH, bs), lambda i: (i, 0, 0)),
        ],
        out_shape=[
            jax.ShapeDtypeStruct((n, w_dn.shape[1]), F32),
            jax.ShapeDtypeStruct((n, LANES), F32),
            jax.ShapeDtypeStruct((n, MB_WIDTH), BF16),
            jax.ShapeDtypeStruct((n // tm, tm // bs, MB_WIDTH), F32),
            jax.ShapeDtypeStruct((n // bs, MB_WIDTH, bs), F32),
            jax.ShapeDtypeStruct((n // bs, MB_WIDTH, bs), BF16),
        ],
        compiler_params=pltpu.CompilerParams(
            dimension_semantics=("arbitrary",), vmem_limit_bytes=VMEM_LIMIT),
        name="in_proj",
    )(x, pre_w, w_dn, w_ba, w_k, w_qt, w_vt)


def _dn_kernel(qkv_ref, ba_ref, convw_ref, alog_ref, dtb_ref, normw_ref, o_ref,
               xbuf_ref, state_ref):
    tt = DN_TILE
    c = DN_CHUNK
    dk = DN_HEAD_DIM
    t = pl.program_id(1)

    @pl.when(t == 0)
    def _():
        xbuf_ref[0:8, :] = jnp.zeros((8, 3 * DN_WIDTH), F32)
        state_ref[...] = jnp.zeros_like(state_ref)

    x = qkv_ref[0]
    xbuf_ref[8:8 + tt, :] = x
    cw = convw_ref[...]
    y = x * cw[DN_CONV - 1:DN_CONV, :]
    for s in range(1, DN_CONV):
        y = y + xbuf_ref[8 - s:8 - s + tt, :] * cw[DN_CONV - 1 - s:DN_CONV - s, :]
    xbuf_ref[0:8, :] = x[tt - 8:tt, :]
    y = y * jax.nn.sigmoid(y)

    ba = ba_ref[0]
    beta_all = jax.nn.sigmoid(ba)
    g_all = -jnp.exp(alog_ref[...]) * jax.nn.softplus(ba + dtb_ref[...])

    ri = lax.broadcasted_iota(jnp.int32, (tt, tt), 0)
    ci = lax.broadcasted_iota(jnp.int32, (tt, tt), 1)
    same_chunk = (ri // c) == (ci // c)
    incl = same_chunk & (ri >= ci)
    eye = ri == ci
    tril = jnp.where(incl, 1.0, 0.0).astype(BF16)
    ones_bd = jnp.where(same_chunk, 1.0, 0.0).astype(BF16)

    g1 = g_all.astype(BF16)
    r1 = g_all - g1.astype(F32)
    g2 = r1.astype(BF16)
    g3 = (r1 - g2.astype(F32)).astype(BF16)
    gcs_all = _dot(tril, g1) + _dot(tril, g2) + _dot(tril, g3)
    gtot_all = _dot(ones_bd, g1) + _dot(ones_bd, g2) + _dot(ones_bd, g3)

    heads = range(DN_HEADS)
    lmat, attn16, rhs, qd, kd, gtot = [], [], [], [], [], []
    for h in heads:
        qr = y[:, h * dk:(h + 1) * dk]
        kr = y[:, DN_WIDTH + h * dk:DN_WIDTH + (h + 1) * dk]
        v = y[:, 2 * DN_WIDTH + h * dk:2 * DN_WIDTH + (h + 1) * dk]
        q = qr * lax.rsqrt(jnp.sum(qr * qr, axis=-1, keepdims=True) + NORM_EPS) * (dk ** -0.5)
        k = kr * lax.rsqrt(jnp.sum(kr * kr, axis=-1, keepdims=True) + NORM_EPS)
        beta = beta_all[:, h:h + 1]
        gcs = gcs_all[:, DN_HEADS + h:DN_HEADS + h + 1]
        gtot.append(gtot_all[:, DN_HEADS + h:DN_HEADS + h + 1])
        eg = jnp.exp(gcs)

        g_row = jnp.sum(jnp.where(eye, gcs, 0.0), axis=0, keepdims=True)
        decay = jnp.exp(jnp.where(incl, gcs - g_row, NEG_BIG))

        kb = k * beta
        k16 = k.astype(BF16)
        lmat.append(jnp.where(eye, 0.0, _dot_nt(kb.astype(BF16), k16) * decay))
        attn16.append((_dot_nt(q.astype(BF16), k16) * decay).astype(BF16))
        rhs.append(jnp.concatenate([v * beta, kb * eg], axis=1).astype(BF16))
        qd.append(q * eg)
        kd.append((k * jnp.exp(gtot[h] - gcs)).astype(BF16))

    xinv = [jnp.where(eye, 1.0, -lmat[h]) for h in heads]
    m = [lmat[h].astype(BF16) for h in heads]
    power = 2
    while power < c:
        m = [_dot(m[h], m[h]).astype(BF16) for h in heads]
        xinv = [xinv[h] + _dot(xinv[h].astype(BF16), m[h]) for h in heads]
        power *= 2

    uw16 = [_dot(xinv[h].astype(BF16), rhs[h]).astype(BF16) for h in heads]
    au_aw = [_dot(attn16[h], uw16[h]) for h in heads]
    au = [au_aw[h][:, :dk] for h in heads]
    e16 = [(qd[h] - au_aw[h][:, dk:]).astype(BF16) for h in heads]

    s = [state_ref[h] for h in heads]
    outs = [[] for _ in heads]
    for ch in range(tt // c):
        lo, hi = ch * c, (ch + 1) * c
        bc = [_dot_tn(kd[h][lo:hi], uw16[h][lo:hi]) for h in heads]
        for h in heads:
            s16 = s[h].astype(BF16)
            outs[h].append(_dot(e16[h][lo:hi], s16) + au[h][lo:hi])
            s[h] = (s[h] * jnp.exp(gtot[h][lo:lo + 1, :]) + bc[h][:, :dk]
                    - _dot(bc[h][:, dk:].astype(BF16), s16))
    for h in heads:
        state_ref[h] = s[h]
        o = jnp.concatenate(outs[h], axis=0)
        o_ref[0, :, h * dk:(h + 1) * dk] = _rms(o, normw_ref[...])


def _deltanet(qkv, ba, conv_w, alog_row, dtb_row, norm_w):
    bsz, t_len, width = qkv.shape
    tt = DN_TILE
    return pl.pallas_call(
        _dn_kernel,
        grid=(bsz, t_len // tt),
        in_specs=[
            pl.BlockSpec((1, tt, width), lambda b, t: (b, t, 0)),
            pl.BlockSpec((1, tt, LANES), lambda b, t: (b, t, 0)),
            _const_spec(conv_w.shape),
            _const_spec((1, LANES)),
            _const_spec((1, LANES)),
            _const_spec((1, DN_HEAD_DIM)),
        ],
        out_specs=pl.BlockSpec((1, tt, DN_WIDTH), lambda b, t: (b, t, 0)),
        out_shape=jax.ShapeDtypeStruct((bsz, t_len, DN_WIDTH), F32),
        scratch_shapes=[
            pltpu.VMEM((8 + tt, width), F32),
            pltpu.VMEM((DN_HEADS, DN_HEAD_DIM, DN_HEAD_DIM), F32),
        ],
        compiler_params=pltpu.CompilerParams(
            dimension_semantics=("arbitrary", "arbitrary"), vmem_limit_bytes=VMEM_LIMIT),
        name="deltanet",
    )(qkv, ba, conv_w, alog_row, dtb_row, norm_w)


MB_SUPER = 1
MB_GANG = 2
MB_AUX_MASK = 16
MB_SUM_ROWS = 16
ALIBI_STEP = int(ALIBI_MAX_BIAS) // MB_HEADS
assert ALIBI_STEP * MB_HEADS == ALIBI_MAX_BIAS
LOG2E = math.log2(math.e)
LOG2E_PIECES = (1.4453125, -0.00262451171875, 7.063150405883789e-06, -1.05355866253376e-08)


def _moba_kernel(qt_ref, k_ref, kmean_ref, vt_ref, o_ref, kaug_ref, sa_ref, sb_ref, *, n_blk):
    bs = MB_BLOCK
    hd = MB_HEAD_DIM
    sup = MB_SUPER * bs
    nbp = -(-n_blk // SUBLANES) * SUBLANES
    gang = pl.program_id(1)
    own = pl.program_id(2)
    pairs = range(MB_GANG)
    lane = lax.broadcasted_iota(jnp.int32, (bs, LANES), 1)
    row = lax.broadcasted_iota(jnp.int32, (bs, LANES), 0)

    @pl.when(own == 0)
    def _():
        def build(j, carry):
            off = pl.multiple_of(j * bs, bs)
            kstart = jnp.full((bs, LANES), j * bs, jnp.int32).astype(F32)
            aux = jnp.where(lane < 2, 1.0,
                            jnp.where(lane < 6, row.astype(F32),
                                      jnp.where(lane < 10, kstart,
                                                jnp.where(lane == MB_AUX_MASK + j, 1.0, 0.0))))
            for pp in pairs:
                kaug_ref[pp, pl.ds(off, bs), 0:LANES] = k_ref[0, pl.ds(off, bs),
                                                              pp * LANES:(pp + 1) * LANES]
                kaug_ref[pp, pl.ds(off, bs), LANES:2 * LANES] = aux.astype(BF16)
            return carry

        lax.fori_loop(0, n_blk, build, 0)

    chan = lax.broadcasted_iota(jnp.int32, (LANES, bs), 0)
    blk = lax.broadcasted_iota(jnp.int32, (nbp, bs), 0)
    blk_f = blk.astype(F32)
    aux_row = lax.broadcasted_iota(jnp.int32, (MB_AUX_MASK, bs), 0)
    qpos = (lax.broadcasted_iota(jnp.int32, (MB_AUX_MASK, bs), 1) + own * bs).astype(F32)
    aux_pad = jnp.zeros((LANES - MB_AUX_MASK - nbp, bs), F32)

    qaug = []
    for pp, hh in [(pp, hh) for pp in pairs for hh in range(2)]:
        qt = qt_ref[0, pp * LANES:(pp + 1) * LANES, :]
        km_hi, km_lo = _split2(kmean_ref[0, :, pp * LANES:(pp + 1) * LANES])
        qth = jnp.where((chan >= hh * hd) & (chan < (hh + 1) * hd), qt, 0.0)
        head = 2 * (MB_GANG * gang + pp) + hh
        slope_bits = (127 - ALIBI_STEP * (head + 1)) << 23
        slope = lax.bitcast_convert_type(jnp.full((MB_AUX_MASK, bs), slope_bits, jnp.int32), F32)

        q_hi, q_lo = _split2(qth)
        gate = _dot(km_hi, q_hi) + _dot(km_hi, q_lo) + _dot(km_lo, q_hi)
        gate = jnp.where(blk < own, gate, -jnp.inf)
        sel = jnp.zeros((nbp, bs), F32)
        for _ in range(MB_TOPK):
            mx = jnp.max(gate, axis=0, keepdims=True)
            first = jnp.min(jnp.where(gate == mx, blk_f, float(nbp)), axis=0, keepdims=True)
            hit = blk_f == first
            sel = jnp.where(hit, 1.0, sel)
            gate = jnp.where(hit, -jnp.inf, gate)
        keep = jnp.where(blk < own, sel, jnp.where(blk == own, 1.0, 0.0))
        mask_rows = jnp.where(keep > 0.5, 0.0, NEG_BIG)
        qconst = -(slope * LOG2E) * qpos
        qconst_hi = qconst.astype(BF16).astype(F32)
        piece_id = (aux_row + 2) & 3
        piece = jnp.where(piece_id == 0, LOG2E_PIECES[0],
                          jnp.where(piece_id == 1, LOG2E_PIECES[1],
                                    jnp.where(piece_id == 2, LOG2E_PIECES[2], LOG2E_PIECES[3])))
        bias_rows = jnp.where(aux_row == 0, qconst_hi,
                              jnp.where(aux_row == 1, qconst - qconst_hi,
                                        jnp.where(aux_row < 10, slope * piece, 0.0)))
        qaug.append(jnp.concatenate([qth * (hd ** -0.5 * LOG2E), bias_rows, mask_rows, aux_pad],
                                    axis=0).astype(BF16))

    def keys(pp, i):
        return kaug_ref[pp, pl.ds(pl.multiple_of(i * sup, sup), sup), :]

    def values_t(i, pp, hh):
        lo = pp * LANES + hh * hd
        return jnp.concatenate([vt_ref[i * MB_SUPER + u, lo:lo + hd, :]
                                for u in range(MB_SUPER)], axis=1)

    n_grp = n_blk // MB_SUPER
    grp = own // MB_SUPER
    qaug2 = [jnp.concatenate(qaug[2 * pp:2 * pp + 2], axis=1) for pp in pairs]
    heads = [(pp, hh) for pp in pairs for hh in range(2)]

    def group_at(t):
        g = jnp.where(t == 0, grp, jnp.where(t > grp, grp + 1, t - 1))
        return jnp.minimum(g, n_grp - 1)

    def scores(g):
        return [_dot(keys(pp, g), qaug2[pp]) for pp in pairs]

    ones_rows = jnp.ones((MB_SUM_ROWS, sup), BF16)

    def produce(s_ref, s2):
        for pp in pairs:
            s_ref[pp] = s2[pp]
        return [jnp.max(s2[pp], axis=0, keepdims=True) for pp in pairs]

    def softmax_step(s_ref, smax, g, carry):
        m_i = [carry[2 * n] for n in range(len(heads))]
        m_new = [jnp.maximum(m_i[n], smax[pp][:, hh * bs:(hh + 1) * bs])
                 for n, (pp, hh) in enumerate(heads)]
        alpha = [jnp.exp2(m_i[n] - m_new[n]) for n in range(len(heads))]
        pexp = [jnp.exp2((s_ref[pp, :, hh * bs:(hh + 1) * bs] - m_new[n]).astype(BF16))
                for n, (pp, hh) in enumerate(heads)]
        acc_new = [carry[2 * n + 1] * alpha[n]
                   + _dot(jnp.concatenate([values_t(g, pp, hh), ones_rows], axis=0), pexp[n])
                   for n, (pp, hh) in enumerate(heads)]
        out = []
        for n in range(len(heads)):
            out += [m_new[n], acc_new[n]]
        return tuple(out)

    rel = (lax.broadcasted_iota(jnp.int32, (sup, 2 * bs), 0) - (own - grp * MB_SUPER) * bs)
    qi = lax.broadcasted_iota(jnp.int32, (sup, 2 * bs), 1) & (bs - 1)
    future = (rel > qi) & (rel < bs)
    smax_a0 = produce(sa_ref, [jnp.where(future, NEG_BIG, s2) for s2 in scores(grp)])

    def pair(u, carry):
        t = 2 * u
        smax_a, carry = list(carry[:MB_GANG]), carry[MB_GANG:]
        smax_b = produce(sb_ref, scores(group_at(t + 1)))
        carry = softmax_step(sa_ref, smax_a, group_at(t), carry)
        smax_a = produce(sa_ref, scores(group_at(t + 2)))
        return tuple(smax_a) + softmax_step(sb_ref, smax_b, group_at(t + 1), carry)

    stat0 = jnp.full((1, bs), -jnp.inf, F32)
    acc0 = jnp.zeros((hd + MB_SUM_ROWS, bs), F32)
    fin = lax.fori_loop(0, grp // 2 + 1, pair, tuple(smax_a0) + (stat0, acc0) * len(heads))
    accs = [fin[MB_GANG + 2 * n + 1] for n in range(len(heads))]
    out_t = jnp.concatenate([a[:hd] / a[hd:hd + 1] for a in accs], axis=0)
    o_ref[0] = out_t.T


def _moba(qt, k, kmean, vt):
    bsz, t_len, _ = k.shape
    bs = MB_BLOCK
    n_blk = t_len // bs
    assert n_blk % (2 * MB_SUPER) == 0 and MB_AUX_MASK + n_blk <= LANES
    assert n_blk % SUBLANES == 0 and MB_PAIRS % MB_GANG == 0
    gw = MB_GANG * LANES
    return pl.pallas_call(
        functools.partial(_moba_kernel, n_blk=n_blk),
        grid=(bsz, MB_PAIRS // MB_GANG, n_blk),
        in_specs=[
            pl.BlockSpec((1, gw, bs), lambda b, p, i: (b * n_blk + i, p, 0)),
            pl.BlockSpec((1, t_len, gw), lambda b, p, i: (b, 0, p)),
            pl.BlockSpec((1, n_blk, gw), lambda b, p, i: (b, 0, p)),
            pl.BlockSpec((n_blk, gw, bs), lambda b, p, i: (b, p, 0)),
        ],
        out_specs=pl.BlockSpec((1, bs, gw), lambda b, p, i: (b, i, p)),
        out_shape=jax.ShapeDtypeStruct((bsz, t_len, MB_WIDTH), F32),
        scratch_shapes=[
            pltpu.VMEM((MB_GANG, t_len, 2 * LANES), BF16),
            pltpu.VMEM((MB_GANG, MB_SUPER * bs, 2 * bs), F32),
            pltpu.VMEM((MB_GANG, MB_SUPER * bs, 2 * bs), F32),
        ],
        compiler_params=pltpu.CompilerParams(
            dimension_semantics=("arbitrary", "arbitrary", "arbitrary"),
            vmem_limit_bytes=VMEM_LIMIT),
        name="moba",
    )(qt, k, kmean, vt)


def _mixout_kernel(x_ref, odn_ref, omb_ref, prew_ref, wz_ref, wgd_ref, wgm_ref,
                   wbd_ref, wbm_ref, wo_ref, postw_ref, o_ref):
    x = x_ref[...]
    h = _rms(x, prew_ref[...]).astype(BF16)
    z = _dot(h, wz_ref[...])
    gate_dn = jax.nn.sigmoid(_dot(h, wgd_ref[...]))
    gate_mb = jax.nn.sigmoid(_dot(h, wgm_ref[...]))
    o_dn = odn_ref[...] * (z * jax.nn.sigmoid(z))
    y_dn = _dot(o_dn.astype(BF16), wbd_ref[...])
    y_mb = _dot(omb_ref[...].astype(BF16), wbm_ref[...])
    merged = gate_dn * y_dn + gate_mb * y_mb
    y = _dot(merged.astype(BF16), wo_ref[...])
    o_ref[...] = x + _rms(y, postw_ref[...])


def _mix_out(x, o_dn, o_mb, pre_w, w_z, w_gd, w_gm, w_bd, w_bm, w_o, post_w, tm=512):
    n, d = x.shape
    return pl.pallas_call(
        _mixout_kernel,
        grid=(n // tm,),
        in_specs=[
            pl.BlockSpec((tm, d), lambda i: (i, 0)),
            pl.BlockSpec((tm, o_dn.shape[1]), lambda i: (i, 0)),
            pl.BlockSpec((tm, o_mb.shape[1]), lambda i: (i, 0)),
            _const_spec((1, d)),
            _const_spec(w_z.shape),
            _const_spec(w_gd.shape),
            _const_spec(w_gm.shape),
            _const_spec(w_bd.shape),
            _const_spec(w_bm.shape),
            _const_spec(w_o.shape),
            _const_spec((1, d)),
        ],
        out_specs=pl.BlockSpec((tm, d), lambda i: (i, 0)),
        out_shape=jax.ShapeDtypeStruct((n, d), F32),
        compiler_params=pltpu.CompilerParams(
            dimension_semantics=("arbitrary",), vmem_limit_bytes=VMEM_LIMIT),
        name="mix_out",
    )(x, o_dn, o_mb, pre_w, w_z, w_gd, w_gm, w_bd, w_bm, w_o, post_w)


def _layer(x, ffn1_pre_w, ffn1_w_gate, ffn1_w_up, ffn1_w_down, ffn1_post_w,
           mix_pre_w, w_in, dn_conv_w, dn_a_log, dn_dt_bias, dn_norm_w,
           w_branch_dn, w_branch_mb, w_out, mix_post_w,
           ffn2_pre_w, ffn2_w_gate, ffn2_w_up, ffn2_w_down, ffn2_post_w):
    bsz, t_len, d = x.shape
    n = bsz * t_len
    row = lambda w: w.reshape(1, -1).astype(F32)
    b16 = lambda w: w.astype(BF16)

    x = x.reshape(n, d)
    x = _ffn_block(x, row(ffn1_pre_w), b16(ffn1_w_gate), b16(ffn1_w_up), b16(ffn1_w_down),
                   row(ffn1_post_w))

    o = 0
    w_dn = w_in[:, o:o + 3 * DN_WIDTH]; o += 3 * DN_WIDTH
    w_z = w_in[:, o:o + DN_WIDTH]; o += DN_WIDTH
    w_ba = w_in[:, o:o + 2 * DN_HEADS]; o += 2 * DN_HEADS
    w_q = w_in[:, o:o + MB_WIDTH]; o += MB_WIDTH
    w_k = w_in[:, o:o + MB_WIDTH]; o += MB_WIDTH
    w_v = w_in[:, o:o + MB_WIDTH]; o += MB_WIDTH
    w_gd = w_in[:, o:o + d]; o += d
    w_gm = w_in[:, o:o + d]; o += d
    w_ba = jnp.pad(w_ba, ((0, 0), (0, LANES - 2 * DN_HEADS)))

    def hi_lo(w):
        hi = w.astype(BF16)
        return jnp.stack([hi, (w - hi.astype(F32)).astype(BF16)])

    dn_qkv, ba, mb_k, mb_kmean, mb_qt, mb_vt = _in_proj(
        x, row(mix_pre_w), b16(w_dn), hi_lo(w_ba), hi_lo(w_k), hi_lo(w_q.T), b16(w_v.T))

    pad_heads = lambda p: jnp.pad(p.astype(F32), (DN_HEADS, LANES - 2 * DN_HEADS)).reshape(1, LANES)
    o_dn = _deltanet(dn_qkv.reshape(bsz, t_len, -1), ba.reshape(bsz, t_len, LANES),
                     dn_conv_w.astype(F32), pad_heads(dn_a_log), pad_heads(dn_dt_bias),
                     row(dn_norm_w))

    o_mb = _moba(mb_qt, mb_k.reshape(bsz, t_len, MB_WIDTH),
                 mb_kmean.reshape(bsz, t_len // MB_BLOCK, MB_WIDTH), mb_vt)

    x = _mix_out(x, o_dn.reshape(n, DN_WIDTH), o_mb.reshape(n, MB_WIDTH), row(mix_pre_w),
                 b16(w_z), b16(w_gd), b16(w_gm), b16(w_branch_dn), b16(w_branch_mb), b16(w_out),
                 row(mix_post_w))

    x = _ffn_block(x, row(ffn2_pre_w), b16(ffn2_w_gate), b16(ffn2_w_up), b16(ffn2_w_down),
                   row(ffn2_post_w))
    return x.reshape(bsz, t_len, d)


def kernel(x, ffn1_pre_w, ffn1_w_gate, ffn1_w_up, ffn1_w_down, ffn1_post_w, mix_pre_w, w_in, dn_conv_w, dn_a_log, dn_dt_bias, dn_norm_w, w_branch_dn, w_branch_mb, w_out, mix_post_w, ffn2_pre_w, ffn2_w_gate, ffn2_w_up, ffn2_w_down, ffn2_post_w):
    depth = w_in.shape[0]
    for l in range(depth):
        x = _layer(x, ffn1_pre_w[l], ffn1_w_gate[l], ffn1_w_up[l], ffn1_w_down[l], ffn1_post_w[l],
                   mix_pre_w[l], w_in[l], dn_conv_w[l], dn_a_log[l], dn_dt_bias[l], dn_norm_w[l],
                   w_branch_dn[l], w_branch_mb[l], w_out[l], mix_post_w[l],
                   ffn2_pre_w[l], ffn2_w_gate[l], ffn2_w_up[l], ffn2_w_down[l], ffn2_post_w[l])
    return x
```

```python
import functools
import math

import jax
import jax.numpy as jnp
from jax import lax
from jax.experimental import pallas as pl
from jax.experimental.pallas import tpu as pltpu

F32 = jnp.float32
BF16 = jnp.bfloat16

NORM_EPS = 1e-6
MACARON_WEIGHT = 0.5

DN_HEADS = 4
DN_HEAD_DIM = 128
DN_WIDTH = DN_HEADS * DN_HEAD_DIM
DN_CONV = 4
DN_CHUNK = 64
DN_TILE = 256
DN_STAGE_LAG = 1 + DN_HEADS + (DN_CHUNK.bit_length() - 2)

MB_HEADS = 8
MB_HEAD_DIM = 64
MB_WIDTH = MB_HEADS * MB_HEAD_DIM
MB_BLOCK = 256
MB_TOPK = 3
ALIBI_MAX_BIAS = 8.0
LANES = 128
SUBLANES = 8
MB_PAIRS = MB_WIDTH // LANES
NEG_BIG = -1e30

VMEM_LIMIT = 56 * 1024 * 1024


def _rms(x, w):
    ms = jnp.mean(x * x, axis=-1, keepdims=True)
    return x * lax.rsqrt(ms + NORM_EPS) * w


def _dot(a, b):
    return jnp.dot(a, b, preferred_element_type=F32)


def _dot_nt(a, b):
    return lax.dot_general(a, b, (((1,), (1,)), ((), ())), preferred_element_type=F32)


def _dot_tn(a, b):
    return lax.dot_general(a, b, (((0,), (0,)), ((), ())), preferred_element_type=F32)


def _split2(x):
    hi = x.astype(BF16)
    lo = (x - hi.astype(F32)).astype(BF16)
    return hi, lo


def _const_spec(shape):
    nd = len(shape)
    return pl.BlockSpec(shape, lambda *_: (0,) * nd, pipeline_mode=pl.Buffered(1))


def _ffn_kernel(x_ref, prew_ref, wg_ref, wu_ref, wd_ref, postw_ref, o_ref):
    x = x_ref[...]
    xn = _rms(x, prew_ref[...]).astype(BF16)
    g = _dot(xn, wg_ref[...])
    u = _dot(xn, wu_ref[...])
    a = (g * jax.nn.sigmoid(g) * u).astype(BF16)
    h = _dot(a, wd_ref[...])
    o_ref[...] = x + MACARON_WEIGHT * _rms(h, postw_ref[...])


def _ffn_block(x, pre_w, w_gate, w_up, w_down, post_w, tm=512):
    n, d = x.shape
    dff = w_gate.shape[1]
    return pl.pallas_call(
        _ffn_kernel,
        grid=(n // tm,),
        in_specs=[
            pl.BlockSpec((tm, d), lambda i: (i, 0)),
            _const_spec((1, d)),
            _const_spec((d, dff)),
            _const_spec((d, dff)),
            _const_spec((dff, d)),
            _const_spec((1, d)),
        ],
        out_specs=pl.BlockSpec((tm, d), lambda i: (i, 0)),
        out_shape=jax.ShapeDtypeStruct((n, d), F32),
        compiler_params=pltpu.CompilerParams(
            dimension_semantics=("arbitrary",), vmem_limit_bytes=VMEM_LIMIT),
        name="ffn_block",
    )(x, pre_w, w_gate, w_up, w_down, post_w)


def _inproj_kernel(x_ref, prew_ref, wdn_ref, wba_ref, wk_ref, wqt_ref, wvt_ref,
                   dn_ref, ba_ref, k_ref, kmean_ref, qt_ref, vt_ref):
    bs = MB_BLOCK
    nb = qt_ref.shape[0]
    h = _rms(x_ref[...], prew_ref[...])
    h_hi, h_lo = _split2(h)
    dn_ref[...] = _dot(h_hi, wdn_ref[...])
    ba_ref[...] = _dot(h_hi, wba_ref[0]) + _dot(h_lo, wba_ref[0]) + _dot(h_hi, wba_ref[1])
    k_ref[...] = _dot(h_hi, wk_ref[0]).astype(BF16)
    hbar = jnp.concatenate([jnp.mean(h[i * bs:(i + 1) * bs], axis=0, keepdims=True)
                            for i in range(nb)]
                           + [jnp.zeros((SUBLANES - nb, h.shape[1]), F32)], axis=0)
    hb_hi, hb_lo = _split2(hbar)
    kmean = _dot(hb_hi, wk_ref[0]) + _dot(hb_lo, wk_ref[0]) + _dot(hb_hi, wk_ref[1])
    kmean_ref[0] = kmean[:nb]
    qt = _dot_nt(wqt_ref[0], h_hi) + _dot_nt(wqt_ref[0], h_lo) + _dot_nt(wqt_ref[1], h_hi)
    vt = _dot_nt(wvt_ref[...], h_hi).astype(BF16)
    for i in range(nb):
        qt_ref[i] = qt[:, i * bs:(i + 1) * bs]
        vt_ref[i] = vt[:, i * bs:(i + 1) * bs]


def _in_proj(x, pre_w, w_dn, w_ba, w_k, w_qt, w_vt, tm=512):
    n, d = x.shape
    bs = MB_BLOCK
    return pl.pallas_call(
        _inproj_kernel,
        grid=(n // tm,),
        in_specs=[
            pl.BlockSpec((tm, d), lambda i: (i, 0)),
            _const_spec((1, d)),
            _const_spec(w_dn.shape),
            _const_spec(w_ba.shape),
            _const_spec(w_k.shape),
            _const_spec(w_qt.shape),
            _const_spec(w_vt.shape),
        ],
        out_specs=[
            pl.BlockSpec((tm, w_dn.shape[1]), lambda i: (i, 0)),
            pl.BlockSpec((tm, LANES), lambda i: (i, 0)),
            pl.BlockSpec((tm, MB_WIDTH), lambda i: (i, 0)),
            pl.BlockSpec((1, tm // bs, MB_WIDTH), lambda i: (i, 0, 0)),
            pl.BlockSpec((tm // bs, MB_WIDTH, bs), lambda i: (i, 0, 0)),
            pl.BlockSpec((tm // bs, MB_WIDTH, bs), lambda i: (i, 0, 0)),
        ],
        out_shape=[
            jax.ShapeDtypeStruct((n, w_dn.shape[1]), F32),
            jax.ShapeDtypeStruct((n, LANES), F32),
            jax.ShapeDtypeStruct((n, MB_WIDTH), BF16),
            jax.ShapeDtypeStruct((n // tm, tm // bs, MB_WIDTH), F32),
            jax.ShapeDtypeStruct((n // bs, MB_WIDTH, bs), F32),
            jax.ShapeDtypeStruct((n // bs, MB_WIDTH, bs), BF16),
        ],
        compiler_params=pltpu.CompilerParams(
            dimension_semantics=("arbitrary",), vmem_limit_bytes=VMEM_LIMIT),
        name="in_proj",
    )(x, pre_w, w_dn, w_ba, w_k, w_qt, w_vt)


def _dn_kernel(qkv_ref, ba_ref, convw_ref, alog_ref, dtb_ref, normw_ref, o_ref,
               xbuf_ref, state_ref):
    tt = DN_TILE
    c = DN_CHUNK
    dk = DN_HEAD_DIM
    n_batch = qkv_ref.shape[0]
    heads = range(DN_HEADS)

    @pl.when(pl.program_id(0) == 0)
    def _():
        xbuf_ref[:, 0:8, :] = jnp.zeros((n_batch, 8, 3 * DN_WIDTH), F32)
        state_ref[...] = jnp.zeros_like(state_ref)

    ri = lax.broadcasted_iota(jnp.int32, (tt, tt), 0)
    ci = lax.broadcasted_iota(jnp.int32, (tt, tt), 1)
    same_chunk = (ri // c) == (ci // c)
    incl = same_chunk & (ri >= ci)
    eye = ri == ci
    tril = jnp.where(incl, 1.0, 0.0).astype(BF16)
    ones_bd = jnp.where(same_chunk, 1.0, 0.0).astype(BF16)

    def batch_program(b):
        x = qkv_ref[b]
        xbuf_ref[b, 8:8 + tt, :] = x
        cw = convw_ref[...]
        y = x * cw[DN_CONV - 1:DN_CONV, :]
        for s in range(1, DN_CONV):
            y = y + xbuf_ref[b, 8 - s:8 - s + tt, :] * cw[DN_CONV - 1 - s:DN_CONV - s, :]
        xbuf_ref[b, 0:8, :] = x[tt - 8:tt, :]
        y = y * jax.nn.sigmoid(y)

        ba = ba_ref[b]
        beta_all = jax.nn.sigmoid(ba)
        g_all = -jnp.exp(alog_ref[...]) * jax.nn.softplus(ba + dtb_ref[...])

        g1 = g_all.astype(BF16)
        r1 = g_all - g1.astype(F32)
        g2 = r1.astype(BF16)
        g3 = (r1 - g2.astype(F32)).astype(BF16)
        gcs_all = _dot(tril, g1) + _dot(tril, g2) + _dot(tril, g3)
        gtot_all = _dot(ones_bd, g1) + _dot(ones_bd, g2) + _dot(ones_bd, g3)
        yield

        lmat, attn16, rhs, qd, kd, gtot = [], [], [], [], [], []
        for h in heads:
            qr = y[:, h * dk:(h + 1) * dk]
            kr = y[:, DN_WIDTH + h * dk:DN_WIDTH + (h + 1) * dk]
            v = y[:, 2 * DN_WIDTH + h * dk:2 * DN_WIDTH + (h + 1) * dk]
            q = qr * lax.rsqrt(jnp.sum(qr * qr, axis=-1, keepdims=True) + NORM_EPS) * (dk ** -0.5)
            k = kr * lax.rsqrt(jnp.sum(kr * kr, axis=-1, keepdims=True) + NORM_EPS)
            beta = beta_all[:, h:h + 1]
            gcs = gcs_all[:, DN_HEADS + h:DN_HEADS + h + 1]
            gtot.append(gtot_all[:, DN_HEADS + h:DN_HEADS + h + 1])
            eg = jnp.exp(gcs)

            g_row = jnp.sum(jnp.where(eye, gcs, 0.0), axis=0, keepdims=True)
            decay = jnp.exp(jnp.where(incl, gcs - g_row, NEG_BIG))

            kb = k * beta
            k16 = k.astype(BF16)
            lmat.append(jnp.where(eye, 0.0, _dot_nt(kb.astype(BF16), k16) * decay))
            attn16.append((_dot_nt(q.astype(BF16), k16) * decay).astype(BF16))
            rhs.append(jnp.concatenate([v * beta, kb * eg], axis=1).astype(BF16))
            qd.append(q * eg)
            kd.append((k * jnp.exp(gtot[h] - gcs)).astype(BF16))
            yield

        xinv = [jnp.where(eye, 1.0, -lmat[h]) for h in heads]
        m = [lmat[h].astype(BF16) for h in heads]
        power = 2
        while power < c:
            m = [_dot(m[h], m[h]).astype(BF16) for h in heads]
            xinv = [xinv[h] + _dot(xinv[h].astype(BF16), m[h]) for h in heads]
            power *= 2
            yield

        uw16 = [_dot(xinv[h].astype(BF16), rhs[h]).astype(BF16) for h in heads]
        au_aw = [_dot(attn16[h], uw16[h]) for h in heads]
        au = [au_aw[h][:, :dk] for h in heads]
        e16 = [(qd[h] - au_aw[h][:, dk:]).astype(BF16) for h in heads]
        yield

        s = [state_ref[b * DN_HEADS + h] for h in heads]
        outs = [[] for _ in heads]
        for ch in range(tt // c):
            lo, hi = ch * c, (ch + 1) * c
            bc = [_dot_tn(kd[h][lo:hi], uw16[h][lo:hi]) for h in heads]
            for h in heads:
                s16 = s[h].astype(BF16)
                outs[h].append(_dot(e16[h][lo:hi], s16) + au[h][lo:hi])
                s[h] = (s[h] * jnp.exp(gtot[h][lo:lo + 1, :]) + bc[h][:, :dk]
                        - _dot(bc[h][:, dk:].astype(BF16), s16))
            yield
        for h in heads:
            state_ref[b * DN_HEADS + h] = s[h]
            o = jnp.concatenate(outs[h], axis=0)
            o_ref[b, :, h * dk:(h + 1) * dk] = _rms(o, normw_ref[...])
        yield

    programs = [batch_program(b) for b in range(n_batch)]
    live = [True] * n_batch
    wave = 0
    while any(live):
        for b in range(n_batch):
            if live[b] and wave >= DN_STAGE_LAG * b:
                live[b] = next(programs[b], "done") != "done"
        wave += 1


def _deltanet(qkv, ba, conv_w, alog_row, dtb_row, norm_w):
    bsz, t_len, width = qkv.shape
    tt = DN_TILE
    return pl.pallas_call(
        _dn_kernel,
        grid=(t_len // tt,),
        in_specs=[
            pl.BlockSpec((bsz, tt, width), lambda t: (0, t, 0)),
            pl.BlockSpec((bsz, tt, LANES), lambda t: (0, t, 0)),
            _const_spec(conv_w.shape),
            _const_spec((1, LANES)),
            _const_spec((1, LANES)),
            _const_spec((1, DN_HEAD_DIM)),
        ],
        out_specs=pl.BlockSpec((bsz, tt, DN_WIDTH), lambda t: (0, t, 0)),
        out_shape=jax.ShapeDtypeStruct((bsz, t_len, DN_WIDTH), F32),
        scratch_shapes=[
            pltpu.VMEM((bsz, 8 + tt, width), F32),
            pltpu.VMEM((bsz * DN_HEADS, DN_HEAD_DIM, DN_HEAD_DIM), F32),
        ],
        compiler_params=pltpu.CompilerParams(
            dimension_semantics=("arbitrary",), vmem_limit_bytes=VMEM_LIMIT),
        name="deltanet",
    )(qkv, ba, conv_w, alog_row, dtb_row, norm_w)


MB_SUPER = 1
MB_GANG = 2
MB_AUX_MASK = 16
MB_SUM_ROWS = 16
ALIBI_STEP = int(ALIBI_MAX_BIAS) // MB_HEADS
assert ALIBI_STEP * MB_HEADS == ALIBI_MAX_BIAS
LOG2E = math.log2(math.e)
LOG2E_PIECES = (1.4453125, -0.00262451171875, 7.063150405883789e-06, -1.05355866253376e-08)


def _moba_kernel(qt_ref, k_ref, kmean_ref, vt_ref, o_ref, kaug_ref, sa_ref, sb_ref, *, n_blk):
    bs = MB_BLOCK
    hd = MB_HEAD_DIM
    sup = MB_SUPER * bs
    nbp = -(-n_blk // SUBLANES) * SUBLANES
    gang = pl.program_id(1)
    own = pl.program_id(2)
    pairs = range(MB_GANG)
    lane = lax.broadcasted_iota(jnp.int32, (bs, LANES), 1)
    row = lax.broadcasted_iota(jnp.int32, (bs, LANES), 0)

    @pl.when(own == 0)
    def _():
        def build(j, carry):
            off = pl.multiple_of(j * bs, bs)
            kstart = jnp.full((bs, LANES), j * bs, jnp.int32).astype(F32)
            aux = jnp.where(lane < 2, 1.0,
                            jnp.where(lane < 6, row.astype(F32),
                                      jnp.where(lane < 10, kstart,
                                                jnp.where(lane == MB_AUX_MASK + j, 1.0, 0.0))))
            for pp in pairs:
                kaug_ref[pp, pl.ds(off, bs), 0:LANES] = k_ref[0, pl.ds(off, bs),
                                                              pp * LANES:(pp + 1) * LANES]
                kaug_ref[pp, pl.ds(off, bs), LANES:2 * LANES] = aux.astype(BF16)
            return carry

        lax.fori_loop(0, n_blk, build, 0)

    chan = lax.broadcasted_iota(jnp.int32, (LANES, bs), 0)
    blk = lax.broadcasted_iota(jnp.int32, (nbp, bs), 0)
    blk_f = blk.astype(F32)
    aux_row = lax.broadcasted_iota(jnp.int32, (MB_AUX_MASK, bs), 0)
    qpos = (lax.broadcasted_iota(jnp.int32, (MB_AUX_MASK, bs), 1) + own * bs).astype(F32)
    aux_pad = jnp.zeros((LANES - MB_AUX_MASK - nbp, bs), F32)

    qaug = []
    for pp, hh in [(pp, hh) for pp in pairs for hh in range(2)]:
        qt = qt_ref[0, pp * LANES:(pp + 1) * LANES, :]
        km_hi, km_lo = _split2(kmean_ref[0, :, pp * LANES:(pp + 1) * LANES])
        qth = jnp.where((chan >= hh * hd) & (chan < (hh + 1) * hd), qt, 0.0)
        head = 2 * (MB_GANG * gang + pp) + hh
        slope_bits = (127 - ALIBI_STEP * (head + 1)) << 23
        slope = lax.bitcast_convert_type(jnp.full((MB_AUX_MASK, bs), slope_bits, jnp.int32), F32)

        q_hi, q_lo = _split2(qth)
        gate = _dot(km_hi, q_hi) + _dot(km_hi, q_lo) + _dot(km_lo, q_hi)
        gate = jnp.where(blk < own, gate, -jnp.inf)
        sel = jnp.zeros((nbp, bs), F32)
        for _ in range(MB_TOPK):
            mx = jnp.max(gate, axis=0, keepdims=True)
            first = jnp.min(jnp.where(gate == mx, blk_f, float(nbp)), axis=0, keepdims=True)
            hit = blk_f == first
            sel = jnp.where(hit, 1.0, sel)
            gate = jnp.where(hit, -jnp.inf, gate)
        keep = jnp.where(blk < own, sel, jnp.where(blk == own, 1.0, 0.0))
        mask_rows = jnp.where(keep > 0.5, 0.0, NEG_BIG)
        qconst = -(slope * LOG2E) * qpos
        qconst_hi = qconst.astype(BF16).astype(F32)
        piece_id = (aux_row + 2) & 3
        piece = jnp.where(piece_id == 0, LOG2E_PIECES[0],
                          jnp.where(piece_id == 1, LOG2E_PIECES[1],
                                    jnp.where(piece_id == 2, LOG2E_PIECES[2], LOG2E_PIECES[3])))
        bias_rows = jnp.where(aux_row == 0, qconst_hi,
                              jnp.where(aux_row == 1, qconst - qconst_hi,
                                        jnp.where(aux_row < 10, slope * piece, 0.0)))
        qaug.append(jnp.concatenate([qth * (hd ** -0.5 * LOG2E), bias_rows, mask_rows, aux_pad],
                                    axis=0).astype(BF16))

    def keys(pp, i):
        return kaug_ref[pp, pl.ds(pl.multiple_of(i * sup, sup), sup), :]

    def values_t(i, pp, hh):
        lo = pp * LANES + hh * hd
        return jnp.concatenate([vt_ref[i * MB_SUPER + u, lo:lo + hd, :]
                                for u in range(MB_SUPER)], axis=1)

    n_grp = n_blk // MB_SUPER
    grp = own // MB_SUPER
    qaug2 = [jnp.concatenate(qaug[2 * pp:2 * pp + 2], axis=1) for pp in pairs]
    heads = [(pp, hh) for pp in pairs for hh in range(2)]

    def group_at(t):
        g = jnp.where(t == 0, grp, jnp.where(t > grp, grp + 1, t - 1))
        return jnp.minimum(g, n_grp - 1)

    def scores(g):
        return [_dot(keys(pp, g), qaug2[pp]) for pp in pairs]

    ones_rows = jnp.ones((MB_SUM_ROWS, sup), BF16)

    def produce(s_ref, s2):
        for pp in pairs:
            s_ref[pp] = s2[pp]
        return [jnp.max(s2[pp], axis=0, keepdims=True) for pp in pairs]

    def softmax_step(s_ref, smax, g, carry):
        m_i = [carry[2 * n] for n in range(len(heads))]
        m_new = [jnp.maximum(m_i[n], smax[pp][:, hh * bs:(hh + 1) * bs])
                 for n, (pp, hh) in enumerate(heads)]
        alpha = [jnp.exp2(m_i[n] - m_new[n]) for n in range(len(heads))]
        pexp = [jnp.exp2((s_ref[pp, :, hh * bs:(hh + 1) * bs] - m_new[n]).astype(BF16))
                for n, (pp, hh) in enumerate(heads)]
        acc_new = [carry[2 * n + 1] * alpha[n]
                   + _dot(jnp.concatenate([values_t(g, pp, hh), ones_rows], axis=0), pexp[n])
                   for n, (pp, hh) in enumerate(heads)]
        out = []
        for n in range(len(heads)):
            out += [m_new[n], acc_new[n]]
        return tuple(out)

    rel = (lax.broadcasted_iota(jnp.int32, (sup, 2 * bs), 0) - (own - grp * MB_SUPER) * bs)
    qi = lax.broadcasted_iota(jnp.int32, (sup, 2 * bs), 1) & (bs - 1)
    future = (rel > qi) & (rel < bs)
    smax_a0 = produce(sa_ref, [jnp.where(future, NEG_BIG, s2) for s2 in scores(grp)])

    def pair(u, carry):
        t = 2 * u
        smax_a, carry = list(carry[:MB_GANG]), carry[MB_GANG:]
        smax_b = produce(sb_ref, scores(group_at(t + 1)))
        carry = softmax_step(sa_ref, smax_a, group_at(t), carry)
        smax_a = produce(sa_ref, scores(group_at(t + 2)))
        return tuple(smax_a) + softmax_step(sb_ref, smax_b, group_at(t + 1), carry)

    stat0 = jnp.full((1, bs), -jnp.inf, F32)
    acc0 = jnp.zeros((hd + MB_SUM_ROWS, bs), F32)
    fin = lax.fori_loop(0, grp // 2 + 1, pair, tuple(smax_a0) + (stat0, acc0) * len(heads))
    accs = [fin[MB_GANG + 2 * n + 1] for n in range(len(heads))]
    out_t = jnp.concatenate([a[:hd] / a[hd:hd + 1] for a in accs], axis=0)
    o_ref[0] = out_t.T


def _moba(qt, k, kmean, vt):
    bsz, t_len, _ = k.shape
    bs = MB_BLOCK
    n_blk = t_len // bs
    assert n_blk % (2 * MB_SUPER) == 0 and MB_AUX_MASK + n_blk <= LANES
    assert n_blk % SUBLANES == 0 and MB_PAIRS % MB_GANG == 0
    gw = MB_GANG * LANES
    return pl.pallas_call(
        functools.partial(_moba_kernel, n_blk=n_blk),
        grid=(bsz, MB_PAIRS // MB_GANG, n_blk),
        in_specs=[
            pl.BlockSpec((1, gw, bs), lambda b, p, i: (b * n_blk + i, p, 0)),
            pl.BlockSpec((1, t_len, gw), lambda b, p, i: (b, 0, p)),
            pl.BlockSpec((1, n_blk, gw), lambda b, p, i: (b, 0, p)),
            pl.BlockSpec((n_blk, gw, bs), lambda b, p, i: (b, p, 0)),
        ],
        out_specs=pl.BlockSpec((1, bs, gw), lambda b, p, i: (b, i, p)),
        out_shape=jax.ShapeDtypeStruct((bsz, t_len, MB_WIDTH), F32),
        scratch_shapes=[
            pltpu.VMEM((MB_GANG, t_len, 2 * LANES), BF16),
            pltpu.VMEM((MB_GANG, MB_SUPER * bs, 2 * bs), F32),
            pltpu.VMEM((MB_GANG, MB_SUPER * bs, 2 * bs), F32),
        ],
        compiler_params=pltpu.CompilerParams(
            dimension_semantics=("arbitrary", "arbitrary", "arbitrary"),
            vmem_limit_bytes=VMEM_LIMIT),
        name="moba",
    )(qt, k, kmean, vt)


def _mixout_kernel(x_ref, odn_ref, omb_ref, prew_ref, wz_ref, wgd_ref, wgm_ref,
                   wbd_ref, wbm_ref, wo_ref, postw_ref, o_ref):
    x = x_ref[...]
    h = _rms(x, prew_ref[...]).astype(BF16)
    z = _dot(h, wz_ref[...])
    gate_dn = jax.nn.sigmoid(_dot(h, wgd_ref[...]))
    gate_mb = jax.nn.sigmoid(_dot(h, wgm_ref[...]))
    o_dn = odn_ref[...] * (z * jax.nn.sigmoid(z))
    y_dn = _dot(o_dn.astype(BF16), wbd_ref[...])
    y_mb = _dot(omb_ref[...].astype(BF16), wbm_ref[...])
    merged = gate_dn * y_dn + gate_mb * y_mb
    y = _dot(merged.astype(BF16), wo_ref[...])
    o_ref[...] = x + _rms(y, postw_ref[...])


def _mix_out(x, o_dn, o_mb, pre_w, w_z, w_gd, w_gm, w_bd, w_bm, w_o, post_w, tm=512):
    n, d = x.shape
    return pl.pallas_call(
        _mixout_kernel,
        grid=(n // tm,),
        in_specs=[
            pl.BlockSpec((tm, d), lambda i: (i, 0)),
            pl.BlockSpec((tm, o_dn.shape[1]), lambda i: (i, 0)),
            pl.BlockSpec((tm, o_mb.shape[1]), lambda i: (i, 0)),
            _const_spec((1, d)),
            _const_spec(w_z.shape),
            _const_spec(w_gd.shape),
            _const_spec(w_gm.shape),
            _const_spec(w_bd.shape),
            _const_spec(w_bm.shape),
            _const_spec(w_o.shape),
            _const_spec((1, d)),
        ],
        out_specs=pl.BlockSpec((tm, d), lambda i: (i, 0)),
        out_shape=jax.ShapeDtypeStruct((n, d), F32),
        compiler_params=pltpu.CompilerParams(
            dimension_semantics=("arbitrary",), vmem_limit_bytes=VMEM_LIMIT),
        name="mix_out",
    )(x, o_dn, o_mb, pre_w, w_z, w_gd, w_gm, w_bd, w_bm, w_o, post_w)


def _layer(x, ffn1_pre_w, ffn1_w_gate, ffn1_w_up, ffn1_w_down, ffn1_post_w,
           mix_pre_w, w_in, dn_conv_w, dn_a_log, dn_dt_bias, dn_norm_w,
           w_branch_dn, w_branch_mb, w_out, mix_post_w,
           ffn2_pre_w, ffn2_w_gate, ffn2_w_up, ffn2_w_down, ffn2_post_w):
    bsz, t_len, d = x.shape
    n = bsz * t_len
    row = lambda w: w.reshape(1, -1).astype(F32)
    b16 = lambda w: w.astype(BF16)

    x = x.reshape(n, d)
    x = _ffn_block(x, row(ffn1_pre_w), b16(ffn1_w_gate), b16(ffn1_w_up), b16(ffn1_w_down),
                   row(ffn1_post_w))

    o = 0
    w_dn = w_in[:, o:o + 3 * DN_WIDTH]; o += 3 * DN_WIDTH
    w_z = w_in[:, o:o + DN_WIDTH]; o += DN_WIDTH
    w_ba = w_in[:, o:o + 2 * DN_HEADS]; o += 2 * DN_HEADS
    w_q = w_in[:, o:o + MB_WIDTH]; o += MB_WIDTH
    w_k = w_in[:, o:o + MB_WIDTH]; o += MB_WIDTH
    w_v = w_in[:, o:o + MB_WIDTH]; o += MB_WIDTH
    w_gd = w_in[:, o:o + d]; o += d
    w_gm = w_in[:, o:o + d]; o += d
    w_ba = jnp.pad(w_ba, ((0, 0), (0, LANES - 2 * DN_HEADS)))

    def hi_lo(w):
        hi = w.astype(BF16)
        return jnp.stack([hi, (w - hi.astype(F32)).astype(BF16)])

    dn_qkv, ba, mb_k, mb_kmean, mb_qt, mb_vt = _in_proj(
        x, row(mix_pre_w), b16(w_dn), hi_lo(w_ba), hi_lo(w_k), hi_lo(w_q.T), b16(w_v.T))

    pad_heads = lambda p: jnp.pad(p.astype(F32), (DN_HEADS, LANES - 2 * DN_HEADS)).reshape(1, LANES)
    o_dn = _deltanet(dn_qkv.reshape(bsz, t_len, -1), ba.reshape(bsz, t_len, LANES),
                     dn_conv_w.astype(F32), pad_heads(dn_a_log), pad_heads(dn_dt_bias),
                     row(dn_norm_w))

    o_mb = _moba(mb_qt, mb_k.reshape(bsz, t_len, MB_WIDTH),
                 mb_kmean.reshape(bsz, t_len // MB_BLOCK, MB_WIDTH), mb_vt)

    x = _mix_out(x, o_dn.reshape(n, DN_WIDTH), o_mb.reshape(n, MB_WIDTH), row(mix_pre_w),
                 b16(w_z), b16(w_gd), b16(w_gm), b16(w_branch_dn), b16(w_branch_mb), b16(w_out),
                 row(mix_post_w))

    x = _ffn_block(x, row(ffn2_pre_w), b16(ffn2_w_gate), b16(ffn2_w_up), b16(ffn2_w_down),
                   row(ffn2_post_w))
    return x.reshape(bsz, t_len, d)


def kernel(x, ffn1_pre_w, ffn1_w_gate, ffn1_w_up, ffn1_w_down, ffn1_post_w, mix_pre_w, w_in, dn_conv_w, dn_a_log, dn_dt_bias, dn_norm_w, w_branch_dn, w_branch_mb, w_out, mix_post_w, ffn2_pre_w, ffn2_w_gate, ffn2_w_up, ffn2_w_down, ffn2_post_w):
    depth = w_in.shape[0]
    for l in range(depth):
        x = _layer(x, ffn1_pre_w[l], ffn1_w_gate[l], ffn1_w_up[l], ffn1_w_down[l], ffn1_post_w[l],
                   mix_pre_w[l], w_in[l], dn_conv_w[l], dn_a_log[l], dn_dt_bias[l], dn_norm_w[l],
                   w_branch_dn[l], w_branch_mb[l], w_out[l], mix_post_w[l],
                   ffn2_pre_w[l], ffn2_w_gate[l], ffn2_w_up[l], ffn2_w_down[l], ffn2_post_w[l])
    return x
```

```python
import functools
import math

import jax
import jax.numpy as jnp
from jax import lax
from jax.experimental import pallas as pl
from jax.experimental.pallas import tpu as pltpu

F32 = jnp.float32
BF16 = jnp.bfloat16

NORM_EPS = 1e-6
MACARON_WEIGHT = 0.5

DN_HEADS = 4
DN_HEAD_DIM = 128
DN_WIDTH = DN_HEADS * DN_HEAD_DIM
DN_CONV = 4
DN_CHUNK = 64
DN_TILE = 256
DN_STAGE_LAG = 1 + DN_HEADS + (DN_CHUNK.bit_length() - 2)

MB_HEADS = 8
MB_HEAD_DIM = 64
MB_WIDTH = MB_HEADS * MB_HEAD_DIM
MB_BLOCK = 256
MB_TOPK = 3
ALIBI_MAX_BIAS = 8.0
LANES = 128
SUBLANES = 8
MB_PAIRS = MB_WIDTH // LANES
NEG_BIG = -1e30

VMEM_LIMIT = 56 * 1024 * 1024


def _rms(x, w):
    ms = jnp.mean(x * x, axis=-1, keepdims=True)
    return x * lax.rsqrt(ms + NORM_EPS) * w


def _dot(a, b):
    return jnp.dot(a, b, preferred_element_type=F32)


def _dot_nt(a, b):
    return lax.dot_general(a, b, (((1,), (1,)), ((), ())), preferred_element_type=F32)


def _dot_tn(a, b):
    return lax.dot_general(a, b, (((0,), (0,)), ((), ())), preferred_element_type=F32)


def _split2(x):
    hi = x.astype(BF16)
    lo = (x - hi.astype(F32)).astype(BF16)
    return hi, lo


def _const_spec(shape):
    nd = len(shape)
    return pl.BlockSpec(shape, lambda *_: (0,) * nd, pipeline_mode=pl.Buffered(1))


def _ffn_kernel(x_ref, prew_ref, wg_ref, wu_ref, wd_ref, postw_ref, o_ref):
    x = x_ref[...]
    xn = _rms(x, prew_ref[...]).astype(BF16)
    g = _dot(xn, wg_ref[...])
    u = _dot(xn, wu_ref[...])
    a = (g * jax.nn.sigmoid(g) * u).astype(BF16)
    h = _dot(a, wd_ref[...])
    o_ref[...] = x + MACARON_WEIGHT * _rms(h, postw_ref[...])


def _ffn_block(x, pre_w, w_gate, w_up, w_down, post_w, tm=512):
    n, d = x.shape
    dff = w_gate.shape[1]
    return pl.pallas_call(
        _ffn_kernel,
        grid=(n // tm,),
        in_specs=[
            pl.BlockSpec((tm, d), lambda i: (i, 0)),
            _const_spec((1, d)),
            _const_spec((d, dff)),
            _const_spec((d, dff)),
            _const_spec((dff, d)),
            _const_spec((1, d)),
        ],
        out_specs=pl.BlockSpec((tm, d), lambda i: (i, 0)),
        out_shape=jax.ShapeDtypeStruct((n, d), F32),
        compiler_params=pltpu.CompilerParams(
            dimension_semantics=("arbitrary",), vmem_limit_bytes=VMEM_LIMIT),
        name="ffn_block",
    )(x, pre_w, w_gate, w_up, w_down, post_w)


def _inproj_kernel(x_ref, prew_ref, wdn_ref, wba_ref, wk_ref, wqt_ref, wvt_ref,
                   dn_ref, ba_ref, k_ref, kmean_ref, qt_ref, vt_ref):
    bs = MB_BLOCK
    nb = qt_ref.shape[0]
    h = _rms(x_ref[...], prew_ref[...])
    h_hi, h_lo = _split2(h)
    dn_ref[...] = _dot(h_hi, wdn_ref[...])
    ba2 = _dot(h_hi, wba_ref[...])
    ba_ref[...] = ba2[:, :LANES] + ba2[:, LANES:] + _dot(h_lo, wba_ref[:, :LANES])
    k_ref[...] = _dot(h_hi, wk_ref[0]).astype(BF16)
    hbar = jnp.concatenate([jnp.mean(h[i * bs:(i + 1) * bs], axis=0, keepdims=True)
                            for i in range(nb)]
                           + [jnp.zeros((SUBLANES - nb, h.shape[1]), F32)], axis=0)
    hb_hi, hb_lo = _split2(hbar)
    kmean = _dot(hb_hi, wk_ref[0]) + _dot(hb_lo, wk_ref[0]) + _dot(hb_hi, wk_ref[1])
    kmean_ref[0] = kmean[:nb]
    qt = _dot_nt(wqt_ref[0], h_hi) + _dot_nt(wqt_ref[0], h_lo) + _dot_nt(wqt_ref[1], h_hi)
    vt = _dot_nt(wvt_ref[...], h_hi).astype(BF16)
    for i in range(nb):
        qt_ref[i] = qt[:, i * bs:(i + 1) * bs]
        vt_ref[i] = vt[:, i * bs:(i + 1) * bs]


def _in_proj(x, pre_w, w_dn, w_ba, w_k, w_qt, w_vt, tm=512):
    n, d = x.shape
    bs = MB_BLOCK
    return pl.pallas_call(
        _inproj_kernel,
        grid=(n // tm,),
        in_specs=[
            pl.BlockSpec((tm, d), lambda i: (i, 0)),
            _const_spec((1, d)),
            _const_spec(w_dn.shape),
            _const_spec(w_ba.shape),
            _const_spec(w_k.shape),
            _const_spec(w_qt.shape),
            _const_spec(w_vt.shape),
        ],
        out_specs=[
            pl.BlockSpec((tm, w_dn.shape[1]), lambda i: (i, 0)),
            pl.BlockSpec((tm, LANES), lambda i: (i, 0)),
            pl.BlockSpec((tm, MB_WIDTH), lambda i: (i, 0)),
            pl.BlockSpec((1, tm // bs, MB_WIDTH), lambda i: (i, 0, 0)),
            pl.BlockSpec((tm // bs, MB_WIDTH, bs), lambda i: (i, 0, 0)),
            pl.BlockSpec((tm // bs, MB_WIDTH, bs), lambda i: (i, 0, 0)),
        ],
        out_shape=[
            jax.ShapeDtypeStruct((n, w_dn.shape[1]), F32),
            jax.ShapeDtypeStruct((n, LANES), F32),
            jax.ShapeDtypeStruct((n, MB_WIDTH), BF16),
            jax.ShapeDtypeStruct((n // tm, tm // bs, MB_WIDTH), F32),
            jax.ShapeDtypeStruct((n // bs, MB_WIDTH, bs), F32),
            jax.ShapeDtypeStruct((n // bs, MB_WIDTH, bs), BF16),
        ],
        compiler_params=pltpu.CompilerParams(
            dimension_semantics=("arbitrary",), vmem_limit_bytes=VMEM_LIMIT),
        name="in_proj",
    )(x, pre_w, w_dn, w_ba, w_k, w_qt, w_vt)


def _dn_kernel(qkv_ref, ba_ref, convw_ref, alog_ref, dtb_ref, normw_ref, o_ref,
               xbuf_ref, state_ref):
    tt = DN_TILE
    c = DN_CHUNK
    dk = DN_HEAD_DIM
    n_batch = qkv_ref.shape[0]
    heads = range(DN_HEADS)

    @pl.when(pl.program_id(0) == 0)
    def _():
        xbuf_ref[:, 0:8, :] = jnp.zeros((n_batch, 8, 3 * DN_WIDTH), F32)
        state_ref[...] = jnp.zeros_like(state_ref)

    ri = lax.broadcasted_iota(jnp.int32, (tt, tt), 0)
    ci = lax.broadcasted_iota(jnp.int32, (tt, tt), 1)
    same_chunk = (ri // c) == (ci // c)
    incl = same_chunk & (ri >= ci)
    eye = ri == ci
    tril = jnp.where(incl, 1.0, 0.0).astype(BF16)
    ones_bd = jnp.where(same_chunk, 1.0, 0.0).astype(BF16)

    def batch_program(b):
        x = qkv_ref[b]
        xbuf_ref[b, 8:8 + tt, :] = x
        cw = convw_ref[...]
        y = x * cw[DN_CONV - 1:DN_CONV, :]
        for s in range(1, DN_CONV):
            y = y + xbuf_ref[b, 8 - s:8 - s + tt, :] * cw[DN_CONV - 1 - s:DN_CONV - s, :]
        xbuf_ref[b, 0:8, :] = x[tt - 8:tt, :]
        y = y * jax.nn.sigmoid(y)

        ba = ba_ref[b]
        beta_all = jax.nn.sigmoid(ba)
        g_all = -jnp.exp(alog_ref[...]) * jax.nn.softplus(ba + dtb_ref[...])

        g1 = g_all.astype(BF16)
        r1 = g_all - g1.astype(F32)
        g2 = r1.astype(BF16)
        g3 = (r1 - g2.astype(F32)).astype(BF16)
        gcs_all = _dot(tril, g1) + _dot(tril, g2) + _dot(tril, g3)
        gtot_all = _dot(ones_bd, g1) + _dot(ones_bd, g2) + _dot(ones_bd, g3)
        yield

        lmat, attn16, rhs, qd, kd, gtot = [], [], [], [], [], []
        for h in heads:
            qr = y[:, h * dk:(h + 1) * dk]
            kr = y[:, DN_WIDTH + h * dk:DN_WIDTH + (h + 1) * dk]
            v = y[:, 2 * DN_WIDTH + h * dk:2 * DN_WIDTH + (h + 1) * dk]
            q = qr * lax.rsqrt(jnp.sum(qr * qr, axis=-1, keepdims=True) + NORM_EPS) * (dk ** -0.5)
            k = kr * lax.rsqrt(jnp.sum(kr * kr, axis=-1, keepdims=True) + NORM_EPS)
            beta = beta_all[:, h:h + 1]
            gcs = gcs_all[:, DN_HEADS + h:DN_HEADS + h + 1]
            gtot.append(gtot_all[:, DN_HEADS + h:DN_HEADS + h + 1])
            eg = jnp.exp(gcs)

            g_row = jnp.sum(jnp.where(eye, gcs, 0.0), axis=0, keepdims=True)
            decay = jnp.exp(jnp.where(incl, gcs - g_row, NEG_BIG))

            kb = k * beta
            k16 = k.astype(BF16)
            lmat.append(jnp.where(eye, 0.0, _dot_nt(kb.astype(BF16), k16) * decay))
            attn16.append((_dot_nt(q.astype(BF16), k16) * decay).astype(BF16))
            rhs.append(jnp.concatenate([v * beta, kb * eg], axis=1).astype(BF16))
            qd.append(q * eg)
            kd.append((k * jnp.exp(gtot[h] - gcs)).astype(BF16))
            yield

        xinv = [jnp.where(eye, 1.0, -lmat[h]) for h in heads]
        m = [lmat[h].astype(BF16) for h in heads]
        power = 2
        while power < c:
            m = [_dot(m[h], m[h]).astype(BF16) for h in heads]
            xinv = [xinv[h] + _dot(xinv[h].astype(BF16), m[h]) for h in heads]
            power *= 2
            yield

        uw16 = [_dot(xinv[h].astype(BF16), rhs[h]).astype(BF16) for h in heads]
        au_aw = [_dot(attn16[h], uw16[h]) for h in heads]
        au = [au_aw[h][:, :dk] for h in heads]
        e16 = [(qd[h] - au_aw[h][:, dk:]).astype(BF16) for h in heads]
        yield

        s = [state_ref[b * DN_HEADS + h] for h in heads]
        outs = [[] for _ in heads]
        for ch in range(tt // c):
            lo, hi = ch * c, (ch + 1) * c
            bc = [_dot_tn(kd[h][lo:hi], uw16[h][lo:hi]) for h in heads]
            for h in heads:
                s16 = s[h].astype(BF16)
                outs[h].append(_dot(e16[h][lo:hi], s16) + au[h][lo:hi])
                s[h] = (s[h] * jnp.exp(gtot[h][lo:lo + 1, :]) + bc[h][:, :dk]
                        - _dot(bc[h][:, dk:].astype(BF16), s16))
            yield
        for h in heads:
            state_ref[b * DN_HEADS + h] = s[h]
            o = jnp.concatenate(outs[h], axis=0)
            o_ref[b, :, h * dk:(h + 1) * dk] = _rms(o, normw_ref[...])
        yield

    programs = [batch_program(b) for b in range(n_batch)]
    live = [True] * n_batch
    wave = 0
    while any(live):
        for b in range(n_batch):
            if live[b] and wave >= DN_STAGE_LAG * b:
                live[b] = next(programs[b], "done") != "done"
        wave += 1


def _deltanet(qkv, ba, conv_w, alog_row, dtb_row, norm_w):
    bsz, t_len, width = qkv.shape
    tt = DN_TILE
    return pl.pallas_call(
        _dn_kernel,
        grid=(t_len // tt,),
        in_specs=[
            pl.BlockSpec((bsz, tt, width), lambda t: (0, t, 0)),
            pl.BlockSpec((bsz, tt, LANES), lambda t: (0, t, 0)),
            _const_spec(conv_w.shape),
            _const_spec((1, LANES)),
            _const_spec((1, LANES)),
            _const_spec((1, DN_HEAD_DIM)),
        ],
        out_specs=pl.BlockSpec((bsz, tt, DN_WIDTH), lambda t: (0, t, 0)),
        out_shape=jax.ShapeDtypeStruct((bsz, t_len, DN_WIDTH), F32),
        scratch_shapes=[
            pltpu.VMEM((bsz, 8 + tt, width), F32),
            pltpu.VMEM((bsz * DN_HEADS, DN_HEAD_DIM, DN_HEAD_DIM), F32),
        ],
        compiler_params=pltpu.CompilerParams(
            dimension_semantics=("arbitrary",), vmem_limit_bytes=VMEM_LIMIT),
        name="deltanet",
    )(qkv, ba, conv_w, alog_row, dtb_row, norm_w)


MB_SUPER = 1
MB_GANG = 2
MB_UNROLL = 4
MB_AUX_MASK = 16
MB_SUM_ROWS = 16
ALIBI_STEP = int(ALIBI_MAX_BIAS) // MB_HEADS
assert ALIBI_STEP * MB_HEADS == ALIBI_MAX_BIAS
LOG2E = math.log2(math.e)
LOG2E_PIECES = (1.4453125, -0.00262451171875, 7.063150405883789e-06, -1.05355866253376e-08)


def _moba_kernel(qt_ref, k_ref, kmean_ref, vt_ref, o_ref, kaug_ref, sa_ref, sb_ref, *, n_blk):
    bs = MB_BLOCK
    hd = MB_HEAD_DIM
    sup = MB_SUPER * bs
    nbp = -(-n_blk // SUBLANES) * SUBLANES
    gang = pl.program_id(1)
    own = pl.program_id(2)
    pairs = range(MB_GANG)
    lane = lax.broadcasted_iota(jnp.int32, (bs, LANES), 1)
    row = lax.broadcasted_iota(jnp.int32, (bs, LANES), 0)

    @pl.when(own == 0)
    def _():
        def build(j, carry):
            off = pl.multiple_of(j * bs, bs)
            kstart = jnp.full((bs, LANES), j * bs, jnp.int32).astype(F32)
            aux = jnp.where(lane < 2, 1.0,
                            jnp.where(lane < 6, row.astype(F32),
                                      jnp.where(lane < 10, kstart,
                                                jnp.where(lane == MB_AUX_MASK + j, 1.0, 0.0))))
            for pp in pairs:
                kaug_ref[pp, pl.ds(off, bs), 0:LANES] = k_ref[0, pl.ds(off, bs),
                                                              pp * LANES:(pp + 1) * LANES]
                kaug_ref[pp, pl.ds(off, bs), LANES:2 * LANES] = aux.astype(BF16)
            return carry

        lax.fori_loop(0, n_blk, build, 0)

    chan = lax.broadcasted_iota(jnp.int32, (LANES, bs), 0)
    blk = lax.broadcasted_iota(jnp.int32, (nbp, bs), 0)
    blk_f = blk.astype(F32)
    aux_row = lax.broadcasted_iota(jnp.int32, (MB_AUX_MASK, bs), 0)
    qpos = (lax.broadcasted_iota(jnp.int32, (MB_AUX_MASK, bs), 1) + own * bs).astype(F32)
    aux_pad = jnp.zeros((LANES - MB_AUX_MASK - nbp, bs), F32)

    qaug = []
    for pp, hh in [(pp, hh) for pp in pairs for hh in range(2)]:
        qt = qt_ref[0, pp * LANES:(pp + 1) * LANES, :]
        km_hi, km_lo = _split2(kmean_ref[0, :, pp * LANES:(pp + 1) * LANES])
        qth = jnp.where((chan >= hh * hd) & (chan < (hh + 1) * hd), qt, 0.0)
        head = 2 * (MB_GANG * gang + pp) + hh
        slope_bits = (127 - ALIBI_STEP * (head + 1)) << 23
        slope = lax.bitcast_convert_type(jnp.full((MB_AUX_MASK, bs), slope_bits, jnp.int32), F32)

        q_hi, q_lo = _split2(qth)
        gate = _dot(km_hi, q_hi) + _dot(km_hi, q_lo) + _dot(km_lo, q_hi)
        gate = jnp.where(blk < own, gate, -jnp.inf)
        sel = jnp.zeros((nbp, bs), F32)
        for _ in range(MB_TOPK):
            mx = jnp.max(gate, axis=0, keepdims=True)
            first = jnp.min(jnp.where(gate == mx, blk_f, float(nbp)), axis=0, keepdims=True)
            hit = blk_f == first
            sel = jnp.where(hit, 1.0, sel)
            gate = jnp.where(hit, -jnp.inf, gate)
        keep = jnp.where(blk < own, sel, jnp.where(blk == own, 1.0, 0.0))
        mask_rows = jnp.where(keep > 0.5, 0.0, NEG_BIG)
        qconst = -(slope * LOG2E) * qpos
        qconst_hi = qconst.astype(BF16).astype(F32)
        piece_id = (aux_row + 2) & 3
        piece = jnp.where(piece_id == 0, LOG2E_PIECES[0],
                          jnp.where(piece_id == 1, LOG2E_PIECES[1],
                                    jnp.where(piece_id == 2, LOG2E_PIECES[2], LOG2E_PIECES[3])))
        bias_rows = jnp.where(aux_row == 0, qconst_hi,
                              jnp.where(aux_row == 1, qconst - qconst_hi,
                                        jnp.where(aux_row < 10, slope * piece, 0.0)))
        qaug.append(jnp.concatenate([qth * (hd ** -0.5 * LOG2E), bias_rows, mask_rows, aux_pad],
                                    axis=0).astype(BF16))

    def keys(pp, i):
        return kaug_ref[pp, pl.ds(pl.multiple_of(i * sup, sup), sup), :]

    def values_t(i, pp, hh):
        lo = pp * LANES + hh * hd
        return jnp.concatenate([vt_ref[i * MB_SUPER + u, lo:lo + hd, :]
                                for u in range(MB_SUPER)], axis=1)

    n_grp = n_blk // MB_SUPER
    grp = own // MB_SUPER
    qaug2 = [jnp.concatenate(qaug[2 * pp:2 * pp + 2], axis=1) for pp in pairs]
    heads = [(pp, hh) for pp in pairs for hh in range(2)]

    def group_at(t):
        g = jnp.where(t == 0, grp, jnp.where(t > grp, grp + 1, t - 1))
        return jnp.minimum(g, n_grp - 1)

    ones_rows = jnp.ones((MB_SUM_ROWS, sup), BF16)

    def produce(s_ref, pp, g, mask=None):
        s2 = _dot(keys(pp, g), qaug2[pp])
        if mask is not None:
            s2 = jnp.where(mask, NEG_BIG, s2)
        s_ref[pp] = s2
        return jnp.max(s2, axis=0, keepdims=True)

    def softmax_step(s_ref, pp, smax, g, carry):
        hs = range(2)
        m_i = [carry[2 * hh] for hh in hs]
        m_new = [jnp.maximum(m_i[hh], smax[:, hh * bs:(hh + 1) * bs]) for hh in hs]
        alpha = [jnp.exp2(m_i[hh] - m_new[hh]) for hh in hs]
        pexp = [jnp.exp2((s_ref[pp, :, hh * bs:(hh + 1) * bs] - m_new[hh]).astype(BF16))
                for hh in hs]
        acc_new = [carry[2 * hh + 1] * alpha[hh]
                   + _dot(jnp.concatenate([values_t(g, pp, hh), ones_rows], axis=0), pexp[hh])
                   for hh in hs]
        return (m_new[0], acc_new[0], m_new[1], acc_new[1])

    rel = (lax.broadcasted_iota(jnp.int32, (sup, 2 * bs), 0) - (own - grp * MB_SUPER) * bs)
    qi = lax.broadcasted_iota(jnp.int32, (sup, 2 * bs), 1) & (bs - 1)
    future = (rel > qi) & (rel < bs)
    smax_a0 = [produce(sa_ref, pp, grp, future) for pp in pairs]

    def pair(u, carry):
        t = MB_UNROLL * u
        smax_a = list(carry[:MB_GANG])
        stats = [carry[MB_GANG + 4 * pp:MB_GANG + 4 * pp + 4] for pp in pairs]
        for v in range(0, MB_UNROLL, 2):
            smax_b = [produce(sb_ref, pp, group_at(t + v + 1)) for pp in pairs]
            stats = [softmax_step(sa_ref, pp, smax_a[pp], group_at(t + v), stats[pp])
                     for pp in pairs]
            smax_a = [produce(sa_ref, pp, group_at(t + v + 2)) for pp in pairs]
            stats = [softmax_step(sb_ref, pp, smax_b[pp], group_at(t + v + 1), stats[pp])
                     for pp in pairs]
        out = tuple(smax_a)
        for pp in pairs:
            out += tuple(stats[pp])
        return out

    stat0 = jnp.full((1, bs), -jnp.inf, F32)
    acc0 = jnp.zeros((hd + MB_SUM_ROWS, bs), F32)
    fin = lax.fori_loop(0, grp // MB_UNROLL + 1, pair,
                        tuple(smax_a0) + (stat0, acc0) * len(heads))
    accs = [fin[MB_GANG + 2 * n + 1] for n in range(len(heads))]
    out_t = jnp.concatenate([a[:hd] / a[hd:hd + 1] for a in accs], axis=0)
    o_ref[0] = out_t.T


def _moba(qt, k, kmean, vt):
    bsz, t_len, _ = k.shape
    bs = MB_BLOCK
    n_blk = t_len // bs
    assert MB_UNROLL % 2 == 0 and n_blk % (MB_UNROLL * MB_SUPER) == 0
    assert MB_AUX_MASK + n_blk <= LANES
    assert n_blk % SUBLANES == 0 and MB_PAIRS % MB_GANG == 0
    gw = MB_GANG * LANES
    return pl.pallas_call(
        functools.partial(_moba_kernel, n_blk=n_blk),
        grid=(bsz, MB_PAIRS // MB_GANG, n_blk),
        in_specs=[
            pl.BlockSpec((1, gw, bs), lambda b, p, i: (b * n_blk + i, p, 0)),
            pl.BlockSpec((1, t_len, gw), lambda b, p, i: (b, 0, p)),
            pl.BlockSpec((1, n_blk, gw), lambda b, p, i: (b, 0, p)),
            pl.BlockSpec((n_blk, gw, bs), lambda b, p, i: (b, p, 0)),
        ],
        out_specs=pl.BlockSpec((1, bs, gw), lambda b, p, i: (b, i, p)),
        out_shape=jax.ShapeDtypeStruct((bsz, t_len, MB_WIDTH), F32),
        scratch_shapes=[
            pltpu.VMEM((MB_GANG, t_len, 2 * LANES), BF16),
            pltpu.VMEM((MB_GANG, MB_SUPER * bs, 2 * bs), F32),
            pltpu.VMEM((MB_GANG, MB_SUPER * bs, 2 * bs), F32),
        ],
        compiler_params=pltpu.CompilerParams(
            dimension_semantics=("arbitrary", "arbitrary", "arbitrary"),
            vmem_limit_bytes=VMEM_LIMIT),
        name="moba",
    )(qt, k, kmean, vt)


def _mixout_kernel(x_ref, odn_ref, omb_ref, prew_ref, wz_ref, wgd_ref, wgm_ref,
                   wbd_ref, wbm_ref, wo_ref, postw_ref, o_ref):
    x = x_ref[...]
    h = _rms(x, prew_ref[...]).astype(BF16)
    z = _dot(h, wz_ref[...])
    gate_dn = jax.nn.sigmoid(_dot(h, wgd_ref[...]))
    gate_mb = jax.nn.sigmoid(_dot(h, wgm_ref[...]))
    o_dn = odn_ref[...] * (z * jax.nn.sigmoid(z))
    y_dn = _dot(o_dn.astype(BF16), wbd_ref[...])
    y_mb = _dot(omb_ref[...].astype(BF16), wbm_ref[...])
    merged = gate_dn * y_dn + gate_mb * y_mb
    y = _dot(merged.astype(BF16), wo_ref[...])
    o_ref[...] = x + _rms(y, postw_ref[...])


def _mix_out(x, o_dn, o_mb, pre_w, w_z, w_gd, w_gm, w_bd, w_bm, w_o, post_w, tm=512):
    n, d = x.shape
    return pl.pallas_call(
        _mixout_kernel,
        grid=(n // tm,),
        in_specs=[
            pl.BlockSpec((tm, d), lambda i: (i, 0)),
            pl.BlockSpec((tm, o_dn.shape[1]), lambda i: (i, 0)),
            pl.BlockSpec((tm, o_mb.shape[1]), lambda i: (i, 0)),
            _const_spec((1, d)),
            _const_spec(w_z.shape),
            _const_spec(w_gd.shape),
            _const_spec(w_gm.shape),
            _const_spec(w_bd.shape),
            _const_spec(w_bm.shape),
            _const_spec(w_o.shape),
            _const_spec((1, d)),
        ],
        out_specs=pl.BlockSpec((tm, d), lambda i: (i, 0)),
        out_shape=jax.ShapeDtypeStruct((n, d), F32),
        compiler_params=pltpu.CompilerParams(
            dimension_semantics=("arbitrary",), vmem_limit_bytes=VMEM_LIMIT),
        name="mix_out",
    )(x, o_dn, o_mb, pre_w, w_z, w_gd, w_gm, w_bd, w_bm, w_o, post_w)


def _layer(x, ffn1_pre_w, ffn1_w_gate, ffn1_w_up, ffn1_w_down, ffn1_post_w,
           mix_pre_w, w_in, dn_conv_w, dn_a_log, dn_dt_bias, dn_norm_w,
           w_branch_dn, w_branch_mb, w_out, mix_post_w,
           ffn2_pre_w, ffn2_w_gate, ffn2_w_up, ffn2_w_down, ffn2_post_w):
    bsz, t_len, d = x.shape
    n = bsz * t_len
    row = lambda w: w.reshape(1, -1).astype(F32)
    b16 = lambda w: w.astype(BF16)

    x = x.reshape(n, d)
    x = _ffn_block(x, row(ffn1_pre_w), b16(ffn1_w_gate), b16(ffn1_w_up), b16(ffn1_w_down),
                   row(ffn1_post_w))

    o = 0
    w_dn = w_in[:, o:o + 3 * DN_WIDTH]; o += 3 * DN_WIDTH
    w_z = w_in[:, o:o + DN_WIDTH]; o += DN_WIDTH
    w_ba = w_in[:, o:o + 2 * DN_HEADS]; o += 2 * DN_HEADS
    w_q = w_in[:, o:o + MB_WIDTH]; o += MB_WIDTH
    w_k = w_in[:, o:o + MB_WIDTH]; o += MB_WIDTH
    w_v = w_in[:, o:o + MB_WIDTH]; o += MB_WIDTH
    w_gd = w_in[:, o:o + d]; o += d
    w_gm = w_in[:, o:o + d]; o += d
    w_ba = jnp.pad(w_ba, ((0, 0), (0, LANES - 2 * DN_HEADS)))

    def hi_lo(w):
        hi = w.astype(BF16)
        return jnp.stack([hi, (w - hi.astype(F32)).astype(BF16)])

    dn_qkv, ba, mb_k, mb_kmean, mb_qt, mb_vt = _in_proj(
        x, row(mix_pre_w), b16(w_dn), jnp.concatenate(list(hi_lo(w_ba)), axis=1), hi_lo(w_k),
        hi_lo(w_q.T), b16(w_v.T))

    pad_heads = lambda p: jnp.pad(p.astype(F32), (DN_HEADS, LANES - 2 * DN_HEADS)).reshape(1, LANES)
    o_dn = _deltanet(dn_qkv.reshape(bsz, t_len, -1), ba.reshape(bsz, t_len, LANES),
                     dn_conv_w.astype(F32), pad_heads(dn_a_log), pad_heads(dn_dt_bias),
                     row(dn_norm_w))

    o_mb = _moba(mb_qt, mb_k.reshape(bsz, t_len, MB_WIDTH),
                 mb_kmean.reshape(bsz, t_len // MB_BLOCK, MB_WIDTH), mb_vt)

    x = _mix_out(x, o_dn.reshape(n, DN_WIDTH), o_mb.reshape(n, MB_WIDTH), row(mix_pre_w),
                 b16(w_z), b16(w_gd), b16(w_gm), b16(w_branch_dn), b16(w_branch_mb), b16(w_out),
                 row(mix_post_w))

    x = _ffn_block(x, row(ffn2_pre_w), b16(ffn2_w_gate), b16(ffn2_w_up), b16(ffn2_w_down),
                   row(ffn2_post_w))
    return x.reshape(bsz, t_len, d)


def kernel(x, ffn1_pre_w, ffn1_w_gate, ffn1_w_up, ffn1_w_down, ffn1_post_w, mix_pre_w, w_in, dn_conv_w, dn_a_log, dn_dt_bias, dn_norm_w, w_branch_dn, w_branch_mb, w_out, mix_post_w, ffn2_pre_w, ffn2_w_gate, ffn2_w_up, ffn2_w_down, ffn2_post_w):
    depth = w_in.shape[0]
    for l in range(depth):
        x = _layer(x, ffn1_pre_w[l], ffn1_w_gate[l], ffn1_w_up[l], ffn1_w_down[l], ffn1_post_w[l],
                   mix_pre_w[l], w_in[l], dn_conv_w[l], dn_a_log[l], dn_dt_bias[l], dn_norm_w[l],
                   w_branch_dn[l], w_branch_mb[l], w_out[l], mix_post_w[l],
                   ffn2_pre_w[l], ffn2_w_gate[l], ffn2_w_up[l], ffn2_w_down[l], ffn2_post_w[l])
    return x
```

```python
import functools
import math

import jax
import jax.numpy as jnp
from jax import lax
from jax.experimental import pallas as pl
from jax.experimental.pallas import tpu as pltpu

F32 = jnp.float32
BF16 = jnp.bfloat16

NORM_EPS = 1e-6
MACARON_WEIGHT = 0.5

DN_HEADS = 4
DN_HEAD_DIM = 128
DN_WIDTH = DN_HEADS * DN_HEAD_DIM
DN_CONV = 4
DN_CHUNK = 64
DN_TILE = 256
DN_STAGE_LAG = 1 + DN_HEADS + (DN_CHUNK.bit_length() - 2)

MB_HEADS = 8
MB_HEAD_DIM = 64
MB_WIDTH = MB_HEADS * MB_HEAD_DIM
MB_BLOCK = 256
MB_TOPK = 3
ALIBI_MAX_BIAS = 8.0
LANES = 128
SUBLANES = 8
MB_PAIRS = MB_WIDTH // LANES
NEG_BIG = -1e30

VMEM_LIMIT = 56 * 1024 * 1024


def _rms(x, w):
    ms = jnp.mean(x * x, axis=-1, keepdims=True)
    return x * lax.rsqrt(ms + NORM_EPS) * w


def _dot(a, b):
    return jnp.dot(a, b, preferred_element_type=F32)


def _dot_nt(a, b):
    return lax.dot_general(a, b, (((1,), (1,)), ((), ())), preferred_element_type=F32)


def _dot_tn(a, b):
    return lax.dot_general(a, b, (((0,), (0,)), ((), ())), preferred_element_type=F32)


def _split2(x):
    hi = x.astype(BF16)
    lo = (x - hi.astype(F32)).astype(BF16)
    return hi, lo


def _const_spec(shape):
    nd = len(shape)
    return pl.BlockSpec(shape, lambda *_: (0,) * nd, pipeline_mode=pl.Buffered(1))


def _ffn_kernel(x_ref, prew_ref, wg_ref, wu_ref, wd_ref, postw_ref, o_ref):
    x = x_ref[...]
    xn = _rms(x, prew_ref[...]).astype(BF16)
    g = _dot(xn, wg_ref[...])
    u = _dot(xn, wu_ref[...])
    a = (g * jax.nn.sigmoid(g) * u).astype(BF16)
    h = _dot(a, wd_ref[...])
    o_ref[...] = x + MACARON_WEIGHT * _rms(h, postw_ref[...])


def _ffn_block(x, pre_w, w_gate, w_up, w_down, post_w, tm=512):
    n, d = x.shape
    dff = w_gate.shape[1]
    return pl.pallas_call(
        _ffn_kernel,
        grid=(n // tm,),
        in_specs=[
            pl.BlockSpec((tm, d), lambda i: (i, 0)),
            _const_spec((1, d)),
            _const_spec((d, dff)),
            _const_spec((d, dff)),
            _const_spec((dff, d)),
            _const_spec((1, d)),
        ],
        out_specs=pl.BlockSpec((tm, d), lambda i: (i, 0)),
        out_shape=jax.ShapeDtypeStruct((n, d), F32),
        compiler_params=pltpu.CompilerParams(
            dimension_semantics=("arbitrary",), vmem_limit_bytes=VMEM_LIMIT),
        name="ffn_block",
    )(x, pre_w, w_gate, w_up, w_down, post_w)


def _inproj_kernel(x_ref, prew_ref, wdn_ref, wba_ref, wk_ref, wqt_ref, wvt_ref,
                   dn_ref, ba_ref, k_ref, kmean_ref, qt_ref, vt_ref):
    bs = MB_BLOCK
    nb = qt_ref.shape[0]
    h = _rms(x_ref[...], prew_ref[...])
    h_hi, h_lo = _split2(h)
    dn_ref[...] = _dot(h_hi, wdn_ref[...])
    ba2 = _dot(h_hi, wba_ref[...])
    ba_ref[...] = ba2[:, :LANES] + ba2[:, LANES:] + _dot(h_lo, wba_ref[:, :LANES])
    k_ref[...] = _dot(h_hi, wk_ref[0]).astype(BF16)
    hbar = jnp.concatenate([jnp.mean(h[i * bs:(i + 1) * bs], axis=0, keepdims=True)
                            for i in range(nb)]
                           + [jnp.zeros((SUBLANES - nb, h.shape[1]), F32)], axis=0)
    hb_hi, hb_lo = _split2(hbar)
    kmean = _dot(hb_hi, wk_ref[0]) + _dot(hb_lo, wk_ref[0]) + _dot(hb_hi, wk_ref[1])
    kmean_ref[0] = kmean[:nb]
    qt = _dot_nt(wqt_ref[0], h_hi) + _dot_nt(wqt_ref[0], h_lo) + _dot_nt(wqt_ref[1], h_hi)
    vt = _dot_nt(wvt_ref[...], h_hi).astype(BF16)
    for i in range(nb):
        qt_ref[i] = qt[:, i * bs:(i + 1) * bs]
        vt_ref[i] = vt[:, i * bs:(i + 1) * bs]


def _in_proj(x, pre_w, w_dn, w_ba, w_k, w_qt, w_vt, tm=512):
    n, d = x.shape
    bs = MB_BLOCK
    return pl.pallas_call(
        _inproj_kernel,
        grid=(n // tm,),
        in_specs=[
            pl.BlockSpec((tm, d), lambda i: (i, 0)),
            _const_spec((1, d)),
            _const_spec(w_dn.shape),
            _const_spec(w_ba.shape),
            _const_spec(w_k.shape),
            _const_spec(w_qt.shape),
            _const_spec(w_vt.shape),
        ],
        out_specs=[
            pl.BlockSpec((tm, w_dn.shape[1]), lambda i: (i, 0)),
            pl.BlockSpec((tm, LANES), lambda i: (i, 0)),
            pl.BlockSpec((tm, MB_WIDTH), lambda i: (i, 0)),
            pl.BlockSpec((1, tm // bs, MB_WIDTH), lambda i: (i, 0, 0)),
            pl.BlockSpec((tm // bs, MB_WIDTH, bs), lambda i: (i, 0, 0)),
            pl.BlockSpec((tm // bs, MB_WIDTH, bs), lambda i: (i, 0, 0)),
        ],
        out_shape=[
            jax.ShapeDtypeStruct((n, w_dn.shape[1]), F32),
            jax.ShapeDtypeStruct((n, LANES), F32),
            jax.ShapeDtypeStruct((n, MB_WIDTH), BF16),
            jax.ShapeDtypeStruct((n // tm, tm // bs, MB_WIDTH), F32),
            jax.ShapeDtypeStruct((n // bs, MB_WIDTH, bs), F32),
            jax.ShapeDtypeStruct((n // bs, MB_WIDTH, bs), BF16),
        ],
        compiler_params=pltpu.CompilerParams(
            dimension_semantics=("arbitrary",), vmem_limit_bytes=VMEM_LIMIT),
        name="in_proj",
    )(x, pre_w, w_dn, w_ba, w_k, w_qt, w_vt)


def _dn_kernel(qkv_ref, ba_ref, convw_ref, alog_ref, dtb_ref, normw_ref, o_ref,
               xbuf_ref, state_ref):
    tt = DN_TILE
    c = DN_CHUNK
    dk = DN_HEAD_DIM
    n_batch = qkv_ref.shape[0]
    heads = range(DN_HEADS)

    @pl.when(pl.program_id(0) == 0)
    def _():
        xbuf_ref[:, 0:8, :] = jnp.zeros((n_batch, 8, 3 * DN_WIDTH), F32)
        state_ref[...] = jnp.zeros_like(state_ref)

    ri = lax.broadcasted_iota(jnp.int32, (tt, tt), 0)
    ci = lax.broadcasted_iota(jnp.int32, (tt, tt), 1)
    same_chunk = (ri // c) == (ci // c)
    incl = same_chunk & (ri >= ci)
    eye = ri == ci
    tril = jnp.where(incl, 1.0, 0.0).astype(BF16)
    ones_bd = jnp.where(same_chunk, 1.0, 0.0).astype(BF16)

    def batch_program(b):
        x = qkv_ref[b]
        xbuf_ref[b, 8:8 + tt, :] = x
        cw = convw_ref[...]
        y = x * cw[DN_CONV - 1:DN_CONV, :]
        for s in range(1, DN_CONV):
            y = y + xbuf_ref[b, 8 - s:8 - s + tt, :] * cw[DN_CONV - 1 - s:DN_CONV - s, :]
        xbuf_ref[b, 0:8, :] = x[tt - 8:tt, :]
        y = y * jax.nn.sigmoid(y)

        ba = ba_ref[b]
        beta_all = jax.nn.sigmoid(ba)
        g_all = -jnp.exp(alog_ref[...]) * jax.nn.softplus(ba + dtb_ref[...])

        g1 = g_all.astype(BF16)
        r1 = g_all - g1.astype(F32)
        g2 = r1.astype(BF16)
        g3 = (r1 - g2.astype(F32)).astype(BF16)
        gcs_all = _dot(tril, g1) + _dot(tril, g2) + _dot(tril, g3)
        gtot_all = _dot(ones_bd, g1) + _dot(ones_bd, g2) + _dot(ones_bd, g3)
        yield

        lmat, attn16, rhs, qd, kd, gtot = [], [], [], [], [], []
        for h in heads:
            qr = y[:, h * dk:(h + 1) * dk]
            kr = y[:, DN_WIDTH + h * dk:DN_WIDTH + (h + 1) * dk]
            v = y[:, 2 * DN_WIDTH + h * dk:2 * DN_WIDTH + (h + 1) * dk]
            q = qr * lax.rsqrt(jnp.sum(qr * qr, axis=-1, keepdims=True) + NORM_EPS) * (dk ** -0.5)
            k = kr * lax.rsqrt(jnp.sum(kr * kr, axis=-1, keepdims=True) + NORM_EPS)
            beta = beta_all[:, h:h + 1]
            gcs = gcs_all[:, DN_HEADS + h:DN_HEADS + h + 1]
            gtot.append(gtot_all[:, DN_HEADS + h:DN_HEADS + h + 1])
            eg = jnp.exp(gcs)

            g_row = jnp.sum(jnp.where(eye, gcs, 0.0), axis=0, keepdims=True)
            decay = jnp.exp(jnp.where(incl, gcs - g_row, NEG_BIG))

            kb = k * beta
            k16 = k.astype(BF16)
            lmat.append(jnp.where(eye, 0.0, _dot_nt(kb.astype(BF16), k16) * decay))
            attn16.append((_dot_nt(q.astype(BF16), k16) * decay).astype(BF16))
            rhs.append(jnp.concatenate([v * beta, kb * eg], axis=1).astype(BF16))
            qd.append(q * eg)
            kd.append((k * jnp.exp(gtot[h] - gcs)).astype(BF16))
            yield

        xinv = [jnp.where(eye, 1.0, -lmat[h]) for h in heads]
        m = [lmat[h].astype(BF16) for h in heads]
        m = [_dot(m[h], m[h]).astype(BF16) for h in heads]
        power = 2
        while power < c:
            if 2 * power < c:
                xm = [_dot(jnp.concatenate([xinv[h].astype(BF16), m[h]], axis=0), m[h])
                      for h in heads]
                xinv = [xinv[h] + xm[h][:tt] for h in heads]
                m = [xm[h][tt:].astype(BF16) for h in heads]
            else:
                xinv = [xinv[h] + _dot(xinv[h].astype(BF16), m[h]) for h in heads]
            power *= 2
            yield

        uw16 = [_dot(xinv[h].astype(BF16), rhs[h]).astype(BF16) for h in heads]
        au_aw = [_dot(attn16[h], uw16[h]) for h in heads]
        au = [au_aw[h][:, :dk] for h in heads]
        e16 = [(qd[h] - au_aw[h][:, dk:]).astype(BF16) for h in heads]
        yield

        s = [state_ref[b * DN_HEADS + h] for h in heads]
        outs = [[] for _ in heads]
        for ch in range(tt // c):
            lo, hi = ch * c, (ch + 1) * c
            bc = [_dot_tn(kd[h][lo:hi], uw16[h][lo:hi]) for h in heads]
            for h in heads:
                s16 = s[h].astype(BF16)
                outs[h].append(_dot(e16[h][lo:hi], s16) + au[h][lo:hi])
                s[h] = (s[h] * jnp.exp(gtot[h][lo:lo + 1, :]) + bc[h][:, :dk]
                        - _dot(bc[h][:, dk:].astype(BF16), s16))
            yield
        for h in heads:
            state_ref[b * DN_HEADS + h] = s[h]
            o = jnp.concatenate(outs[h], axis=0)
            o_ref[b, :, h * dk:(h + 1) * dk] = _rms(o, normw_ref[...])
        yield

    programs = [batch_program(b) for b in range(n_batch)]
    live = [True] * n_batch
    wave = 0
    while any(live):
        for b in range(n_batch):
            if live[b] and wave >= DN_STAGE_LAG * b:
                live[b] = next(programs[b], "done") != "done"
        wave += 1


def _deltanet(qkv, ba, conv_w, alog_row, dtb_row, norm_w):
    bsz, t_len, width = qkv.shape
    tt = DN_TILE
    return pl.pallas_call(
        _dn_kernel,
        grid=(t_len // tt,),
        in_specs=[
            pl.BlockSpec((bsz, tt, width), lambda t: (0, t, 0)),
            pl.BlockSpec((bsz, tt, LANES), lambda t: (0, t, 0)),
            _const_spec(conv_w.shape),
            _const_spec((1, LANES)),
            _const_spec((1, LANES)),
            _const_spec((1, DN_HEAD_DIM)),
        ],
        out_specs=pl.BlockSpec((bsz, tt, DN_WIDTH), lambda t: (0, t, 0)),
        out_shape=jax.ShapeDtypeStruct((bsz, t_len, DN_WIDTH), F32),
        scratch_shapes=[
            pltpu.VMEM((bsz, 8 + tt, width), F32),
            pltpu.VMEM((bsz * DN_HEADS, DN_HEAD_DIM, DN_HEAD_DIM), F32),
        ],
        compiler_params=pltpu.CompilerParams(
            dimension_semantics=("arbitrary",), vmem_limit_bytes=VMEM_LIMIT),
        name="deltanet",
    )(qkv, ba, conv_w, alog_row, dtb_row, norm_w)


MB_SUPER = 1
MB_GANG = 2
MB_UNROLL = 4
MB_AUX_MASK = 16
MB_SUM_ROWS = 16
ALIBI_STEP = int(ALIBI_MAX_BIAS) // MB_HEADS
assert ALIBI_STEP * MB_HEADS == ALIBI_MAX_BIAS
LOG2E = math.log2(math.e)
LOG2E_PIECES = (1.4453125, -0.00262451171875, 7.063150405883789e-06, -1.05355866253376e-08)


def _moba_kernel(qt_ref, k_ref, kmean_ref, vt_ref, o_ref, kaug_ref, sa_ref, sb_ref, *, n_blk):
    bs = MB_BLOCK
    hd = MB_HEAD_DIM
    sup = MB_SUPER * bs
    nbp = -(-n_blk // SUBLANES) * SUBLANES
    gang = pl.program_id(1)
    own = pl.program_id(2)
    pairs = range(MB_GANG)
    lane = lax.broadcasted_iota(jnp.int32, (bs, LANES), 1)
    row = lax.broadcasted_iota(jnp.int32, (bs, LANES), 0)

    @pl.when(own == 0)
    def _():
        def build(j, carry):
            off = pl.multiple_of(j * bs, bs)
            kstart = jnp.full((bs, LANES), j * bs, jnp.int32).astype(F32)
            aux = jnp.where(lane < 2, 1.0,
                            jnp.where(lane < 6, row.astype(F32),
                                      jnp.where(lane < 10, kstart,
                                                jnp.where(lane == MB_AUX_MASK + j, 1.0, 0.0))))
            for pp in pairs:
                kaug_ref[pp, pl.ds(off, bs), 0:LANES] = k_ref[0, pl.ds(off, bs),
                                                              pp * LANES:(pp + 1) * LANES]
                kaug_ref[pp, pl.ds(off, bs), LANES:2 * LANES] = aux.astype(BF16)
            return carry

        lax.fori_loop(0, n_blk, build, 0)

    chan = lax.broadcasted_iota(jnp.int32, (LANES, bs), 0)
    blk = lax.broadcasted_iota(jnp.int32, (nbp, bs), 0)
    blk_f = blk.astype(F32)
    aux_row = lax.broadcasted_iota(jnp.int32, (MB_AUX_MASK, bs), 0)
    qpos = (lax.broadcasted_iota(jnp.int32, (MB_AUX_MASK, bs), 1) + own * bs).astype(F32)
    aux_pad = jnp.zeros((LANES - MB_AUX_MASK - nbp, bs), F32)

    qaug = []
    for pp, hh in [(pp, hh) for pp in pairs for hh in range(2)]:
        qt = qt_ref[0, pp * LANES:(pp + 1) * LANES, :]
        km_hi, km_lo = _split2(kmean_ref[0, :, pp * LANES:(pp + 1) * LANES])
        qth = jnp.where((chan >= hh * hd) & (chan < (hh + 1) * hd), qt, 0.0)
        head = 2 * (MB_GANG * gang + pp) + hh
        slope_bits = (127 - ALIBI_STEP * (head + 1)) << 23
        slope = lax.bitcast_convert_type(jnp.full((MB_AUX_MASK, bs), slope_bits, jnp.int32), F32)

        q_hi, q_lo = _split2(qth)
        gate = _dot(km_hi, q_hi) + _dot(km_hi, q_lo) + _dot(km_lo, q_hi)
        gate = jnp.where(blk < own, gate, -jnp.inf)
        sel = jnp.zeros((nbp, bs), F32)
        for _ in range(MB_TOPK):
            mx = jnp.max(gate, axis=0, keepdims=True)
            first = jnp.min(jnp.where(gate == mx, blk_f, float(nbp)), axis=0, keepdims=True)
            hit = blk_f == first
            sel = jnp.where(hit, 1.0, sel)
            gate = jnp.where(hit, -jnp.inf, gate)
        keep = jnp.where(blk < own, sel, jnp.where(blk == own, 1.0, 0.0))
        mask_rows = jnp.where(keep > 0.5, 0.0, NEG_BIG)
        qconst = -(slope * LOG2E) * qpos
        qconst_hi = qconst.astype(BF16).astype(F32)
        piece_id = (aux_row + 2) & 3
        piece = jnp.where(piece_id == 0, LOG2E_PIECES[0],
                          jnp.where(piece_id == 1, LOG2E_PIECES[1],
                                    jnp.where(piece_id == 2, LOG2E_PIECES[2], LOG2E_PIECES[3])))
        bias_rows = jnp.where(aux_row == 0, qconst_hi,
                              jnp.where(aux_row == 1, qconst - qconst_hi,
                                        jnp.where(aux_row < 10, slope * piece, 0.0)))
        qaug.append(jnp.concatenate([qth * (hd ** -0.5 * LOG2E), bias_rows, mask_rows, aux_pad],
                                    axis=0).astype(BF16))

    def keys(pp, i):
        return kaug_ref[pp, pl.ds(pl.multiple_of(i * sup, sup), sup), :]

    def values_t(i, pp, hh):
        lo = pp * LANES + hh * hd
        return jnp.concatenate([vt_ref[i * MB_SUPER + u, lo:lo + hd, :]
                                for u in range(MB_SUPER)], axis=1)

    n_grp = n_blk // MB_SUPER
    grp = own // MB_SUPER
    qaug2 = [jnp.concatenate(qaug[2 * pp:2 * pp + 2], axis=1) for pp in pairs]
    heads = [(pp, hh) for pp in pairs for hh in range(2)]

    def group_at(t):
        g = jnp.where(t == 0, grp, jnp.where(t > grp, grp + 1, t - 1))
        return jnp.minimum(g, n_grp - 1)

    ones_rows = jnp.ones((MB_SUM_ROWS, sup), BF16)

    def produce(s_ref, pp, g, mask=None):
        s2 = _dot(keys(pp, g), qaug2[pp])
        if mask is not None:
            s2 = jnp.where(mask, NEG_BIG, s2)
        s_ref[pp] = s2
        return jnp.max(s2, axis=0, keepdims=True)

    def softmax_step(s_ref, pp, smax, g, carry):
        hs = range(2)
        m_i = [carry[2 * hh] for hh in hs]
        m_new = [jnp.maximum(m_i[hh], smax[:, hh * bs:(hh + 1) * bs]) for hh in hs]
        alpha = [jnp.exp2(m_i[hh] - m_new[hh]) for hh in hs]
        pexp = [jnp.exp2((s_ref[pp, :, hh * bs:(hh + 1) * bs] - m_new[hh]).astype(BF16))
                for hh in hs]
        acc_new = [carry[2 * hh + 1] * alpha[hh]
                   + _dot(jnp.concatenate([values_t(g, pp, hh), ones_rows], axis=0), pexp[hh])
                   for hh in hs]
        return (m_new[0], acc_new[0], m_new[1], acc_new[1])

    rel = (lax.broadcasted_iota(jnp.int32, (sup, 2 * bs), 0) - (own - grp * MB_SUPER) * bs)
    qi = lax.broadcasted_iota(jnp.int32, (sup, 2 * bs), 1) & (bs - 1)
    future = (rel > qi) & (rel < bs)
    smax_a0 = [produce(sa_ref, pp, grp, future) for pp in pairs]

    def steps(u, carry, unroll, t0):
        t = t0 + unroll * u
        smax_a = list(carry[:MB_GANG])
        stats = [carry[MB_GANG + 4 * pp:MB_GANG + 4 * pp + 4] for pp in pairs]
        for v in range(0, unroll, 2):
            smax_b = [produce(sb_ref, pp, group_at(t + v + 1)) for pp in pairs]
            stats = [softmax_step(sa_ref, pp, smax_a[pp], group_at(t + v), stats[pp])
                     for pp in pairs]
            smax_a = [produce(sa_ref, pp, group_at(t + v + 2)) for pp in pairs]
            stats = [softmax_step(sb_ref, pp, smax_b[pp], group_at(t + v + 1), stats[pp])
                     for pp in pairs]
        out = tuple(smax_a)
        for pp in pairs:
            out += tuple(stats[pp])
        return out

    stat0 = jnp.full((1, bs), -jnp.inf, F32)
    acc0 = jnp.zeros((hd + MB_SUM_ROWS, bs), F32)
    n_full = (grp + 1) // MB_UNROLL
    n_tail = (grp + 2 - n_full * MB_UNROLL) // 2
    fin = lax.fori_loop(0, n_full, functools.partial(steps, unroll=MB_UNROLL, t0=0),
                        tuple(smax_a0) + (stat0, acc0) * len(heads))
    fin = lax.fori_loop(0, n_tail, functools.partial(steps, unroll=2, t0=n_full * MB_UNROLL), fin)
    accs = [fin[MB_GANG + 2 * n + 1] for n in range(len(heads))]
    out_t = jnp.concatenate([a[:hd] / a[hd:hd + 1] for a in accs], axis=0)
    o_ref[0] = out_t.T


def _moba(qt, k, kmean, vt):
    bsz, t_len, _ = k.shape
    bs = MB_BLOCK
    n_blk = t_len // bs
    assert MB_UNROLL % 2 == 0 and n_blk % (MB_UNROLL * MB_SUPER) == 0
    assert MB_AUX_MASK + n_blk <= LANES
    assert n_blk % SUBLANES == 0 and MB_PAIRS % MB_GANG == 0
    gw = MB_GANG * LANES
    return pl.pallas_call(
        functools.partial(_moba_kernel, n_blk=n_blk),
        grid=(bsz, MB_PAIRS // MB_GANG, n_blk),
        in_specs=[
            pl.BlockSpec((1, gw, bs), lambda b, p, i: (b * n_blk + i, p, 0)),
            pl.BlockSpec((1, t_len, gw), lambda b, p, i: (b, 0, p)),
            pl.BlockSpec((1, n_blk, gw), lambda b, p, i: (b, 0, p)),
            pl.BlockSpec((n_blk, gw, bs), lambda b, p, i: (b, p, 0)),
        ],
        out_specs=pl.BlockSpec((1, bs, gw), lambda b, p, i: (b, i, p)),
        out_shape=jax.ShapeDtypeStruct((bsz, t_len, MB_WIDTH), F32),
        scratch_shapes=[
            pltpu.VMEM((MB_GANG, t_len, 2 * LANES), BF16),
            pltpu.VMEM((MB_GANG, MB_SUPER * bs, 2 * bs), F32),
            pltpu.VMEM((MB_GANG, MB_SUPER * bs, 2 * bs), F32),
        ],
        compiler_params=pltpu.CompilerParams(
            dimension_semantics=("arbitrary", "arbitrary", "arbitrary"),
            vmem_limit_bytes=VMEM_LIMIT),
        name="moba",
    )(qt, k, kmean, vt)


def _mixout_kernel(x_ref, odn_ref, omb_ref, prew_ref, wz_ref, wgd_ref, wgm_ref,
                   wbd_ref, wbm_ref, wo_ref, postw_ref, o_ref):
    x = x_ref[...]
    h = _rms(x, prew_ref[...]).astype(BF16)
    z = _dot(h, wz_ref[...])
    gate_dn = jax.nn.sigmoid(_dot(h, wgd_ref[...]))
    gate_mb = jax.nn.sigmoid(_dot(h, wgm_ref[...]))
    o_dn = odn_ref[...] * (z * jax.nn.sigmoid(z))
    y_dn = _dot(o_dn.astype(BF16), wbd_ref[...])
    y_mb = _dot(omb_ref[...].astype(BF16), wbm_ref[...])
    merged = gate_dn * y_dn + gate_mb * y_mb
    y = _dot(merged.astype(BF16), wo_ref[...])
    o_ref[...] = x + _rms(y, postw_ref[...])


def _mix_out(x, o_dn, o_mb, pre_w, w_z, w_gd, w_gm, w_bd, w_bm, w_o, post_w, tm=512):
    n, d = x.shape
    return pl.pallas_call(
        _mixout_kernel,
        grid=(n // tm,),
        in_specs=[
            pl.BlockSpec((tm, d), lambda i: (i, 0)),
            pl.BlockSpec((tm, o_dn.shape[1]), lambda i: (i, 0)),
            pl.BlockSpec((tm, o_mb.shape[1]), lambda i: (i, 0)),
            _const_spec((1, d)),
            _const_spec(w_z.shape),
            _const_spec(w_gd.shape),
            _const_spec(w_gm.shape),
            _const_spec(w_bd.shape),
            _const_spec(w_bm.shape),
            _const_spec(w_o.shape),
            _const_spec((1, d)),
        ],
        out_specs=pl.BlockSpec((tm, d), lambda i: (i, 0)),
        out_shape=jax.ShapeDtypeStruct((n, d), F32),
        compiler_params=pltpu.CompilerParams(
            dimension_semantics=("arbitrary",), vmem_limit_bytes=VMEM_LIMIT),
        name="mix_out",
    )(x, o_dn, o_mb, pre_w, w_z, w_gd, w_gm, w_bd, w_bm, w_o, post_w)


def _layer(x, ffn1_pre_w, ffn1_w_gate, ffn1_w_up, ffn1_w_down, ffn1_post_w,
           mix_pre_w, w_in, dn_conv_w, dn_a_log, dn_dt_bias, dn_norm_w,
           w_branch_dn, w_branch_mb, w_out, mix_post_w,
           ffn2_pre_w, ffn2_w_gate, ffn2_w_up, ffn2_w_down, ffn2_post_w):
    bsz, t_len, d = x.shape
    n = bsz * t_len
    row = lambda w: w.reshape(1, -1).astype(F32)
    b16 = lambda w: w.astype(BF16)

    x = x.reshape(n, d)
    x = _ffn_block(x, row(ffn1_pre_w), b16(ffn1_w_gate), b16(ffn1_w_up), b16(ffn1_w_down),
                   row(ffn1_post_w))

    o = 0
    w_dn = w_in[:, o:o + 3 * DN_WIDTH]; o += 3 * DN_WIDTH
    w_z = w_in[:, o:o + DN_WIDTH]; o += DN_WIDTH
    w_ba = w_in[:, o:o + 2 * DN_HEADS]; o += 2 * DN_HEADS
    w_q = w_in[:, o:o + MB_WIDTH]; o += MB_WIDTH
    w_k = w_in[:, o:o + MB_WIDTH]; o += MB_WIDTH
    w_v = w_in[:, o:o + MB_WIDTH]; o += MB_WIDTH
    w_gd = w_in[:, o:o + d]; o += d
    w_gm = w_in[:, o:o + d]; o += d
    w_ba = jnp.pad(w_ba, ((0, 0), (0, LANES - 2 * DN_HEADS)))

    def hi_lo(w):
        hi = w.astype(BF16)
        return jnp.stack([hi, (w - hi.astype(F32)).astype(BF16)])

    dn_qkv, ba, mb_k, mb_kmean, mb_qt, mb_vt = _in_proj(
        x, row(mix_pre_w), b16(w_dn), jnp.concatenate(list(hi_lo(w_ba)), axis=1), hi_lo(w_k),
        hi_lo(w_q.T), b16(w_v.T))

    pad_heads = lambda p: jnp.pad(p.astype(F32), (DN_HEADS, LANES - 2 * DN_HEADS)).reshape(1, LANES)
    o_dn = _deltanet(dn_qkv.reshape(bsz, t_len, -1), ba.reshape(bsz, t_len, LANES),
                     dn_conv_w.astype(F32), pad_heads(dn_a_log), pad_heads(dn_dt_bias),
                     row(dn_norm_w))

    o_mb = _moba(mb_qt, mb_k.reshape(bsz, t_len, MB_WIDTH),
                 mb_kmean.reshape(bsz, t_len // MB_BLOCK, MB_WIDTH), mb_vt)

    x = _mix_out(x, o_dn.reshape(n, DN_WIDTH), o_mb.reshape(n, MB_WIDTH), row(mix_pre_w),
                 b16(w_z), b16(w_gd), b16(w_gm), b16(w_branch_dn), b16(w_branch_mb), b16(w_out),
                 row(mix_post_w))

    x = _ffn_block(x, row(ffn2_pre_w), b16(ffn2_w_gate), b16(ffn2_w_up), b16(ffn2_w_down),
                   row(ffn2_post_w))
    return x.reshape(bsz, t_len, d)


def kernel(x, ffn1_pre_w, ffn1_w_gate, ffn1_w_up, ffn1_w_down, ffn1_post_w, mix_pre_w, w_in, dn_conv_w, dn_a_log, dn_dt_bias, dn_norm_w, w_branch_dn, w_branch_mb, w_out, mix_post_w, ffn2_pre_w, ffn2_w_gate, ffn2_w_up, ffn2_w_down, ffn2_post_w):
    depth = w_in.shape[0]
    for l in range(depth):
        x = _layer(x, ffn1_pre_w[l], ffn1_w_gate[l], ffn1_w_up[l], ffn1_w_down[l], ffn1_post_w[l],
                   mix_pre_w[l], w_in[l], dn_conv_w[l], dn_a_log[l], dn_dt_bias[l], dn_norm_w[l],
                   w_branch_dn[l], w_branch_mb[l], w_out[l], mix_post_w[l],
                   ffn2_pre_w[l], ffn2_w_gate[l], ffn2_w_up[l], ffn2_w_down[l], ffn2_post_w[l])
    return x
```

```python
import functools
import math

import jax
import jax.numpy as jnp
from jax import lax
from jax.experimental import pallas as pl
from jax.experimental.pallas import tpu as pltpu

F32 = jnp.float32
BF16 = jnp.bfloat16

NORM_EPS = 1e-6
MACARON_WEIGHT = 0.5

DN_HEADS = 4
DN_HEAD_DIM = 128
DN_WIDTH = DN_HEADS * DN_HEAD_DIM
DN_CONV = 4
DN_CHUNK = 64
DN_TILE = 256
DN_STAGE_LAG = 1 + DN_HEADS + (DN_CHUNK.bit_length() - 2)

MB_HEADS = 8
MB_HEAD_DIM = 64
MB_WIDTH = MB_HEADS * MB_HEAD_DIM
MB_BLOCK = 256
MB_TOPK = 3
ALIBI_MAX_BIAS = 8.0
LANES = 128
SUBLANES = 8
MB_PAIRS = MB_WIDTH // LANES
NEG_BIG = -1e30

VMEM_LIMIT = 56 * 1024 * 1024


def _rms(x, w):
    ms = jnp.mean(x * x, axis=-1, keepdims=True)
    return x * lax.rsqrt(ms + NORM_EPS) * w


def _dot(a, b):
    return jnp.dot(a, b, preferred_element_type=F32)


def _dot_nt(a, b):
    return lax.dot_general(a, b, (((1,), (1,)), ((), ())), preferred_element_type=F32)


def _dot_tn(a, b):
    return lax.dot_general(a, b, (((0,), (0,)), ((), ())), preferred_element_type=F32)


def _split2(x):
    hi = x.astype(BF16)
    lo = (x - hi.astype(F32)).astype(BF16)
    return hi, lo


def _const_spec(shape):
    nd = len(shape)
    return pl.BlockSpec(shape, lambda *_: (0,) * nd, pipeline_mode=pl.Buffered(1))


def _ffn_kernel(x_ref, prew_ref, wg_ref, wu_ref, wd_ref, postw_ref, o_ref):
    x = x_ref[...]
    xn = _rms(x, prew_ref[...]).astype(BF16)
    g = _dot(xn, wg_ref[...])
    u = _dot(xn, wu_ref[...])
    a = (g * jax.nn.sigmoid(g) * u).astype(BF16)
    h = _dot(a, wd_ref[...])
    o_ref[...] = x + MACARON_WEIGHT * _rms(h, postw_ref[...])


def _ffn_block(x, pre_w, w_gate, w_up, w_down, post_w, tm=512):
    n, d = x.shape
    dff = w_gate.shape[1]
    return pl.pallas_call(
        _ffn_kernel,
        grid=(n // tm,),
        in_specs=[
            pl.BlockSpec((tm, d), lambda i: (i, 0)),
            _const_spec((1, d)),
            _const_spec((d, dff)),
            _const_spec((d, dff)),
            _const_spec((dff, d)),
            _const_spec((1, d)),
        ],
        out_specs=pl.BlockSpec((tm, d), lambda i: (i, 0)),
        out_shape=jax.ShapeDtypeStruct((n, d), F32),
        compiler_params=pltpu.CompilerParams(
            dimension_semantics=("arbitrary",), vmem_limit_bytes=VMEM_LIMIT),
        name="ffn_block",
    )(x, pre_w, w_gate, w_up, w_down, post_w)


def _inproj_kernel(x_ref, prew_ref, wdn_ref, wba_ref, wk_ref, wqt_ref, wvt_ref,
                   dn_ref, ba_ref, k_ref, kmean_ref, qt_ref, vt_ref):
    bs = MB_BLOCK
    nb = qt_ref.shape[0]
    h = _rms(x_ref[...], prew_ref[...])
    h16 = h.astype(BF16)
    dn_ref[...] = _dot(h16, wdn_ref[...])
    ba2 = _dot(h16, wba_ref[...])
    ba_ref[...] = ba2[:, :LANES] + ba2[:, LANES:]
    k_ref[...] = _dot(h16, wk_ref[0]).astype(BF16)
    hbar = jnp.concatenate([jnp.mean(h[i * bs:(i + 1) * bs], axis=0, keepdims=True)
                            for i in range(nb)]
                           + [jnp.zeros((SUBLANES - nb, h.shape[1]), F32)], axis=0)
    hb_hi, hb_lo = _split2(hbar)
    kmean = _dot(hb_hi, wk_ref[0]) + _dot(hb_lo, wk_ref[0]) + _dot(hb_hi, wk_ref[1])
    kmean_ref[0] = kmean[:nb]
    qt = _dot_nt(wqt_ref[...], h16)
    vt = _dot_nt(wvt_ref[...], h16).astype(BF16)
    for i in range(nb):
        qt_ref[i] = qt[:, i * bs:(i + 1) * bs]
        vt_ref[i] = vt[:, i * bs:(i + 1) * bs]


def _in_proj(x, pre_w, w_dn, w_ba, w_k, w_qt, w_vt, tm=512):
    n, d = x.shape
    bs = MB_BLOCK
    return pl.pallas_call(
        _inproj_kernel,
        grid=(n // tm,),
        in_specs=[
            pl.BlockSpec((tm, d), lambda i: (i, 0)),
            _const_spec((1, d)),
            _const_spec(w_dn.shape),
            _const_spec(w_ba.shape),
            _const_spec(w_k.shape),
            _const_spec(w_qt.shape),
            _const_spec(w_vt.shape),
        ],
        out_specs=[
            pl.BlockSpec((tm, w_dn.shape[1]), lambda i: (i, 0)),
            pl.BlockSpec((tm, LANES), lambda i: (i, 0)),
            pl.BlockSpec((tm, MB_WIDTH), lambda i: (i, 0)),
            pl.BlockSpec((1, tm // bs, MB_WIDTH), lambda i: (i, 0, 0)),
            pl.BlockSpec((tm // bs, MB_WIDTH, bs), lambda i: (i, 0, 0)),
            pl.BlockSpec((tm // bs, MB_WIDTH, bs), lambda i: (i, 0, 0)),
        ],
        out_shape=[
            jax.ShapeDtypeStruct((n, w_dn.shape[1]), F32),
            jax.ShapeDtypeStruct((n, LANES), F32),
            jax.ShapeDtypeStruct((n, MB_WIDTH), BF16),
            jax.ShapeDtypeStruct((n // tm, tm // bs, MB_WIDTH), F32),
            jax.ShapeDtypeStruct((n // bs, MB_WIDTH, bs), F32),
            jax.ShapeDtypeStruct((n // bs, MB_WIDTH, bs), BF16),
        ],
        compiler_params=pltpu.CompilerParams(
            dimension_semantics=("arbitrary",), vmem_limit_bytes=VMEM_LIMIT),
        name="in_proj",
    )(x, pre_w, w_dn, w_ba, w_k, w_qt, w_vt)


def _dn_kernel(qkv_ref, ba_ref, convw_ref, alog_ref, dtb_ref, normw_ref, o_ref,
               xbuf_ref, state_ref):
    tt = DN_TILE
    c = DN_CHUNK
    dk = DN_HEAD_DIM
    n_batch = qkv_ref.shape[0]
    heads = range(DN_HEADS)

    @pl.when(pl.program_id(0) == 0)
    def _():
        xbuf_ref[:, 0:8, :] = jnp.zeros((n_batch, 8, 3 * DN_WIDTH), F32)
        state_ref[...] = jnp.zeros_like(state_ref)

    ri = lax.broadcasted_iota(jnp.int32, (tt, tt), 0)
    ci = lax.broadcasted_iota(jnp.int32, (tt, tt), 1)
    same_chunk = (ri // c) == (ci // c)
    incl = same_chunk & (ri >= ci)
    eye = ri == ci
    tril = jnp.where(incl, 1.0, 0.0).astype(BF16)
    ones_bd = jnp.where(same_chunk, 1.0, 0.0).astype(BF16)

    def batch_program(b):
        x = qkv_ref[b]
        xbuf_ref[b, 8:8 + tt, :] = x
        cw = convw_ref[...]
        y = x * cw[DN_CONV - 1:DN_CONV, :]
        for s in range(1, DN_CONV):
            y = y + xbuf_ref[b, 8 - s:8 - s + tt, :] * cw[DN_CONV - 1 - s:DN_CONV - s, :]
        xbuf_ref[b, 0:8, :] = x[tt - 8:tt, :]
        y = y * jax.nn.sigmoid(y)

        ba = ba_ref[b]
        beta_all = jax.nn.sigmoid(ba)
        g_all = -jnp.exp(alog_ref[...]) * jax.nn.softplus(ba + dtb_ref[...])

        g1 = g_all.astype(BF16)
        r1 = g_all - g1.astype(F32)
        g2 = r1.astype(BF16)
        g3 = (r1 - g2.astype(F32)).astype(BF16)
        gcs_all = _dot(tril, g1) + _dot(tril, g2) + _dot(tril, g3)
        gtot_all = _dot(ones_bd, g1) + _dot(ones_bd, g2) + _dot(ones_bd, g3)
        yield

        lmat, attn16, rhs, qd, kd, gtot = [], [], [], [], [], []
        for h in heads:
            qr = y[:, h * dk:(h + 1) * dk]
            kr = y[:, DN_WIDTH + h * dk:DN_WIDTH + (h + 1) * dk]
            v = y[:, 2 * DN_WIDTH + h * dk:2 * DN_WIDTH + (h + 1) * dk]
            q = qr * lax.rsqrt(jnp.sum(qr * qr, axis=-1, keepdims=True) + NORM_EPS) * (dk ** -0.5)
            k = kr * lax.rsqrt(jnp.sum(kr * kr, axis=-1, keepdims=True) + NORM_EPS)
            beta = beta_all[:, h:h + 1]
            gcs = gcs_all[:, DN_HEADS + h:DN_HEADS + h + 1]
            gtot.append(gtot_all[:, DN_HEADS + h:DN_HEADS + h + 1])
            eg = jnp.exp(gcs)

            g_row = jnp.sum(jnp.where(eye, gcs, 0.0), axis=0, keepdims=True)
            decay = jnp.exp(jnp.where(incl, gcs - g_row, NEG_BIG))

            kb = k * beta
            k16 = k.astype(BF16)
            lmat.append(jnp.where(eye, 0.0, _dot_nt(kb.astype(BF16), k16) * decay))
            attn16.append((_dot_nt(q.astype(BF16), k16) * decay).astype(BF16))
            rhs.append(jnp.concatenate([v * beta, kb * eg], axis=1).astype(BF16))
            qd.append(q * eg)
            kd.append((k * jnp.exp(gtot[h] - gcs)).astype(BF16))
            yield

        xinv = [jnp.where(eye, 1.0, -lmat[h]) for h in heads]
        m = [lmat[h].astype(BF16) for h in heads]
        m = [_dot(m[h], m[h]).astype(BF16) for h in heads]
        power = 2
        while power < c:
            if 2 * power < c:
                xm = [_dot(jnp.concatenate([xinv[h].astype(BF16), m[h]], axis=0), m[h])
                      for h in heads]
                xinv = [xinv[h] + xm[h][:tt] for h in heads]
                m = [xm[h][tt:].astype(BF16) for h in heads]
            else:
                xinv = [xinv[h] + _dot(xinv[h].astype(BF16), m[h]) for h in heads]
            power *= 2
            yield

        uw16 = [_dot(xinv[h].astype(BF16), rhs[h]).astype(BF16) for h in heads]
        au_aw = [_dot(attn16[h], uw16[h]) for h in heads]
        au = [au_aw[h][:, :dk] for h in heads]
        e16 = [(qd[h] - au_aw[h][:, dk:]).astype(BF16) for h in heads]
        yield

        s = [state_ref[b * DN_HEADS + h] for h in heads]
        outs = [[] for _ in heads]
        for ch in range(tt // c):
            lo, hi = ch * c, (ch + 1) * c
            bc = [_dot_tn(kd[h][lo:hi], uw16[h][lo:hi]) for h in heads]
            for h in heads:
                s16 = s[h].astype(BF16)
                outs[h].append(_dot(e16[h][lo:hi], s16) + au[h][lo:hi])
                s[h] = (s[h] * jnp.exp(gtot[h][lo:lo + 1, :]) + bc[h][:, :dk]
                        - _dot(bc[h][:, dk:].astype(BF16), s16))
            yield
        for h in heads:
            state_ref[b * DN_HEADS + h] = s[h]
            o = jnp.concatenate(outs[h], axis=0)
            o_ref[b, :, h * dk:(h + 1) * dk] = _rms(o, normw_ref[...])
        yield

    programs = [batch_program(b) for b in range(n_batch)]
    live = [True] * n_batch
    wave = 0
    while any(live):
        for b in range(n_batch):
            if live[b] and wave >= DN_STAGE_LAG * b:
                live[b] = next(programs[b], "done") != "done"
        wave += 1


def _deltanet(qkv, ba, conv_w, alog_row, dtb_row, norm_w):
    bsz, t_len, width = qkv.shape
    tt = DN_TILE
    return pl.pallas_call(
        _dn_kernel,
        grid=(t_len // tt,),
        in_specs=[
            pl.BlockSpec((bsz, tt, width), lambda t: (0, t, 0)),
            pl.BlockSpec((bsz, tt, LANES), lambda t: (0, t, 0)),
            _const_spec(conv_w.shape),
            _const_spec((1, LANES)),
            _const_spec((1, LANES)),
            _const_spec((1, DN_HEAD_DIM)),
        ],
        out_specs=pl.BlockSpec((bsz, tt, DN_WIDTH), lambda t: (0, t, 0)),
        out_shape=jax.ShapeDtypeStruct((bsz, t_len, DN_WIDTH), F32),
        scratch_shapes=[
            pltpu.VMEM((bsz, 8 + tt, width), F32),
            pltpu.VMEM((bsz * DN_HEADS, DN_HEAD_DIM, DN_HEAD_DIM), F32),
        ],
        compiler_params=pltpu.CompilerParams(
            dimension_semantics=("arbitrary",), vmem_limit_bytes=VMEM_LIMIT),
        name="deltanet",
    )(qkv, ba, conv_w, alog_row, dtb_row, norm_w)


MB_SUPER = 1
MB_GANG = 2
MB_UNROLL = 4
MB_AUX_MASK = 16
MB_SUM_ROWS = 16
ALIBI_STEP = int(ALIBI_MAX_BIAS) // MB_HEADS
assert ALIBI_STEP * MB_HEADS == ALIBI_MAX_BIAS
LOG2E = math.log2(math.e)
LOG2E_PIECES = (1.4453125, -0.00262451171875, 7.063150405883789e-06, -1.05355866253376e-08)


def _moba_kernel(qt_ref, k_ref, kmean_ref, vt_ref, o_ref, kaug_ref, sa_ref, sb_ref, *, n_blk):
    bs = MB_BLOCK
    hd = MB_HEAD_DIM
    sup = MB_SUPER * bs
    nbp = -(-n_blk // SUBLANES) * SUBLANES
    gang = pl.program_id(1)
    own = pl.program_id(2)
    pairs = range(MB_GANG)
    lane = lax.broadcasted_iota(jnp.int32, (bs, LANES), 1)
    row = lax.broadcasted_iota(jnp.int32, (bs, LANES), 0)

    @pl.when(own == 0)
    def _():
        def build(j, carry):
            off = pl.multiple_of(j * bs, bs)
            kstart = jnp.full((bs, LANES), j * bs, jnp.int32).astype(F32)
            aux = jnp.where(lane < 2, 1.0,
                            jnp.where(lane < 6, row.astype(F32),
                                      jnp.where(lane < 10, kstart,
                                                jnp.where(lane == MB_AUX_MASK + j, 1.0, 0.0))))
            for pp in pairs:
                kaug_ref[pp, pl.ds(off, bs), 0:LANES] = k_ref[0, pl.ds(off, bs),
                                                              pp * LANES:(pp + 1) * LANES]
                kaug_ref[pp, pl.ds(off, bs), LANES:2 * LANES] = aux.astype(BF16)
            return carry

        lax.fori_loop(0, n_blk, build, 0)

    chan = lax.broadcasted_iota(jnp.int32, (LANES, bs), 0)
    blk = lax.broadcasted_iota(jnp.int32, (nbp, bs), 0)
    blk_f = blk.astype(F32)
    aux_row = lax.broadcasted_iota(jnp.int32, (MB_AUX_MASK, bs), 0)
    qpos = (lax.broadcasted_iota(jnp.int32, (MB_AUX_MASK, bs), 1) + own * bs).astype(F32)
    aux_pad = jnp.zeros((LANES - MB_AUX_MASK - nbp, bs), F32)

    heads = [(pp, hh) for pp in pairs for hh in range(2)]
    qth, gate = [], []
    for pp, hh in heads:
        qt = qt_ref[0, pp * LANES:(pp + 1) * LANES, :]
        km_hi, km_lo = _split2(kmean_ref[0, :, pp * LANES:(pp + 1) * LANES])
        qth.append(jnp.where((chan >= hh * hd) & (chan < (hh + 1) * hd), qt, 0.0))
        q_hi, q_lo = _split2(qth[-1])
        g = _dot(km_hi, q_hi) + _dot(km_hi, q_lo) + _dot(km_lo, q_hi)
        gate.append(jnp.where(blk < own, g, -jnp.inf))

    piece_id = (aux_row + 2) & 3
    piece = jnp.where(piece_id == 0, LOG2E_PIECES[0],
                      jnp.where(piece_id == 1, LOG2E_PIECES[1],
                                jnp.where(piece_id == 2, LOG2E_PIECES[2], LOG2E_PIECES[3])))
    q_rows = []
    for n, (pp, hh) in enumerate(heads):
        head = 2 * (MB_GANG * gang + pp) + hh
        slope_bits = (127 - ALIBI_STEP * (head + 1)) << 23
        slope = lax.bitcast_convert_type(jnp.full((MB_AUX_MASK, bs), slope_bits, jnp.int32), F32)
        qconst = -(slope * LOG2E) * qpos
        qconst_hi = qconst.astype(BF16).astype(F32)
        bias_rows = jnp.where(aux_row == 0, qconst_hi,
                              jnp.where(aux_row == 1, qconst - qconst_hi,
                                        jnp.where(aux_row < 10, slope * piece, 0.0)))
        q_rows.append(jnp.concatenate([qth[n] * (hd ** -0.5 * LOG2E), bias_rows], axis=0))

    def query_operand(mask_rows):
        ops = [jnp.concatenate([q_rows[n], mask_rows[n], aux_pad], axis=0).astype(BF16)
               for n in range(len(heads))]
        return [jnp.concatenate(ops[2 * pp:2 * pp + 2], axis=1) for pp in pairs]

    def keys(pp, i):
        return kaug_ref[pp, pl.ds(pl.multiple_of(i * sup, sup), sup), :]

    def values_t(i, pp, hh):
        lo = pp * LANES + hh * hd
        return jnp.concatenate([vt_ref[i * MB_SUPER + u, lo:lo + hd, :]
                                for u in range(MB_SUPER)], axis=1)

    def produce(s_ref, pp, g, q_op, mask=None):
        s2 = _dot(keys(pp, g), q_op[pp])
        if mask is not None:
            s2 = jnp.where(mask, NEG_BIG, s2)
        s_ref[pp] = s2
        return jnp.max(s2, axis=0, keepdims=True)

    n_grp = n_blk // MB_SUPER
    grp = own // MB_SUPER

    assert MB_SUPER == 1
    rel = (lax.broadcasted_iota(jnp.int32, (sup, 2 * bs), 0) - (own - grp * MB_SUPER) * bs)
    qi = lax.broadcasted_iota(jnp.int32, (sup, 2 * bs), 1) & (bs - 1)
    future = (rel > qi) & (rel < bs)
    q_own = query_operand([jnp.zeros((nbp, bs), F32)] * len(heads))
    smax_a0 = [produce(sa_ref, pp, grp, q_own, future) for pp in pairs]

    sel = [jnp.zeros((nbp, bs), F32) for _ in heads]
    for _ in range(MB_TOPK):
        mx = [jnp.max(g, axis=0, keepdims=True) for g in gate]
        first = [jnp.min(jnp.where(g == m, blk_f, float(nbp)), axis=0, keepdims=True)
                 for g, m in zip(gate, mx)]
        hit = [blk_f == f for f in first]
        sel = [jnp.where(h, 1.0, s) for h, s in zip(hit, sel)]
        gate = [jnp.where(h, -jnp.inf, g) for h, g in zip(hit, gate)]
    keep = [jnp.where(blk < own, s, jnp.where(blk == own, 1.0, 0.0)) for s in sel]
    qaug2 = query_operand([jnp.where(k > 0.5, 0.0, NEG_BIG) for k in keep])

    def group_at(t):
        g = jnp.where(t == 0, grp, jnp.where(t > grp, grp + 1, t - 1))
        return jnp.minimum(g, n_grp - 1)

    ones_rows = jnp.ones((MB_SUM_ROWS, sup), BF16)

    def softmax_step(s_ref, pp, smax, g, carry):
        hs = range(2)
        m_i = [carry[2 * hh] for hh in hs]
        m_new = [jnp.maximum(m_i[hh], smax[:, hh * bs:(hh + 1) * bs]) for hh in hs]
        alpha = [jnp.exp2(m_i[hh] - m_new[hh]) for hh in hs]
        pexp = [jnp.exp2((s_ref[pp, :, hh * bs:(hh + 1) * bs] - m_new[hh]).astype(BF16))
                for hh in hs]
        acc_new = [carry[2 * hh + 1] * alpha[hh]
                   + _dot(jnp.concatenate([values_t(g, pp, hh), ones_rows], axis=0), pexp[hh])
                   for hh in hs]
        return (m_new[0], acc_new[0], m_new[1], acc_new[1])

    def steps(u, carry, unroll, t0):
        t = t0 + unroll * u
        smax_a = list(carry[:MB_GANG])
        stats = [carry[MB_GANG + 4 * pp:MB_GANG + 4 * pp + 4] for pp in pairs]
        for v in range(0, unroll, 2):
            smax_b = [produce(sb_ref, pp, group_at(t + v + 1), qaug2) for pp in pairs]
            stats = [softmax_step(sa_ref, pp, smax_a[pp], group_at(t + v), stats[pp])
                     for pp in pairs]
            smax_a = [produce(sa_ref, pp, group_at(t + v + 2), qaug2) for pp in pairs]
            stats = [softmax_step(sb_ref, pp, smax_b[pp], group_at(t + v + 1), stats[pp])
                     for pp in pairs]
        out = tuple(smax_a)
        for pp in pairs:
            out += tuple(stats[pp])
        return out

    stat0 = jnp.full((1, bs), -jnp.inf, F32)
    acc0 = jnp.zeros((hd + MB_SUM_ROWS, bs), F32)
    n_full = (grp + 1) // MB_UNROLL
    n_tail = (grp + 2 - n_full * MB_UNROLL) // 2
    fin = lax.fori_loop(0, n_full, functools.partial(steps, unroll=MB_UNROLL, t0=0),
                        tuple(smax_a0) + (stat0, acc0) * len(heads))
    fin = lax.fori_loop(0, n_tail, functools.partial(steps, unroll=2, t0=n_full * MB_UNROLL), fin)
    accs = [fin[MB_GANG + 2 * n + 1] for n in range(len(heads))]
    out_t = jnp.concatenate([a[:hd] / a[hd:hd + 1] for a in accs], axis=0)
    o_ref[0] = out_t.T


def _moba(qt, k, kmean, vt):
    bsz, t_len, _ = k.shape
    bs = MB_BLOCK
    n_blk = t_len // bs
    assert MB_UNROLL % 2 == 0 and n_blk % (MB_UNROLL * MB_SUPER) == 0
    assert MB_AUX_MASK + n_blk <= LANES
    assert n_blk % SUBLANES == 0 and MB_PAIRS % MB_GANG == 0
    gw = MB_GANG * LANES
    return pl.pallas_call(
        functools.partial(_moba_kernel, n_blk=n_blk),
        grid=(bsz, MB_PAIRS // MB_GANG, n_blk),
        in_specs=[
            pl.BlockSpec((1, gw, bs), lambda b, p, i: (b * n_blk + i, p, 0)),
            pl.BlockSpec((1, t_len, gw), lambda b, p, i: (b, 0, p)),
            pl.BlockSpec((1, n_blk, gw), lambda b, p, i: (b, 0, p)),
            pl.BlockSpec((n_blk, gw, bs), lambda b, p, i: (b, p, 0)),
        ],
        out_specs=pl.BlockSpec((1, bs, gw), lambda b, p, i: (b, i, p)),
        out_shape=jax.ShapeDtypeStruct((bsz, t_len, MB_WIDTH), F32),
        scratch_shapes=[
            pltpu.VMEM((MB_GANG, t_len, 2 * LANES), BF16),
            pltpu.VMEM((MB_GANG, MB_SUPER * bs, 2 * bs), F32),
            pltpu.VMEM((MB_GANG, MB_SUPER * bs, 2 * bs), F32),
        ],
        compiler_params=pltpu.CompilerParams(
            dimension_semantics=("arbitrary", "arbitrary", "arbitrary"),
            vmem_limit_bytes=VMEM_LIMIT),
        name="moba",
    )(qt, k, kmean, vt)


def _mixout_kernel(x_ref, odn_ref, omb_ref, prew_ref, wz_ref, wgd_ref, wgm_ref,
                   wbd_ref, wbm_ref, wo_ref, postw_ref, o_ref):
    x = x_ref[...]
    h = _rms(x, prew_ref[...]).astype(BF16)
    z = _dot(h, wz_ref[...])
    gate_dn = jax.nn.sigmoid(_dot(h, wgd_ref[...]))
    gate_mb = jax.nn.sigmoid(_dot(h, wgm_ref[...]))
    o_dn = odn_ref[...] * (z * jax.nn.sigmoid(z))
    y_dn = _dot(o_dn.astype(BF16), wbd_ref[...])
    y_mb = _dot(omb_ref[...].astype(BF16), wbm_ref[...])
    merged = gate_dn * y_dn + gate_mb * y_mb
    y = _dot(merged.astype(BF16), wo_ref[...])
    o_ref[...] = x + _rms(y, postw_ref[...])


def _mix_out(x, o_dn, o_mb, pre_w, w_z, w_gd, w_gm, w_bd, w_bm, w_o, post_w, tm=512):
    n, d = x.shape
    return pl.pallas_call(
        _mixout_kernel,
        grid=(n // tm,),
        in_specs=[
            pl.BlockSpec((tm, d), lambda i: (i, 0)),
            pl.BlockSpec((tm, o_dn.shape[1]), lambda i: (i, 0)),
            pl.BlockSpec((tm, o_mb.shape[1]), lambda i: (i, 0)),
            _const_spec((1, d)),
            _const_spec(w_z.shape),
            _const_spec(w_gd.shape),
            _const_spec(w_gm.shape),
            _const_spec(w_bd.shape),
            _const_spec(w_bm.shape),
            _const_spec(w_o.shape),
            _const_spec((1, d)),
        ],
        out_specs=pl.BlockSpec((tm, d), lambda i: (i, 0)),
        out_shape=jax.ShapeDtypeStruct((n, d), F32),
        compiler_params=pltpu.CompilerParams(
            dimension_semantics=("arbitrary",), vmem_limit_bytes=VMEM_LIMIT),
        name="mix_out",
    )(x, o_dn, o_mb, pre_w, w_z, w_gd, w_gm, w_bd, w_bm, w_o, post_w)


def _layer(x, ffn1_pre_w, ffn1_w_gate, ffn1_w_up, ffn1_w_down, ffn1_post_w,
           mix_pre_w, w_in, dn_conv_w, dn_a_log, dn_dt_bias, dn_norm_w,
           w_branch_dn, w_branch_mb, w_out, mix_post_w,
           ffn2_pre_w, ffn2_w_gate, ffn2_w_up, ffn2_w_down, ffn2_post_w):
    bsz, t_len, d = x.shape
    n = bsz * t_len
    row = lambda w: w.reshape(1, -1).astype(F32)
    b16 = lambda w: w.astype(BF16)

    x = x.reshape(n, d)
    x = _ffn_block(x, row(ffn1_pre_w), b16(ffn1_w_gate), b16(ffn1_w_up), b16(ffn1_w_down),
                   row(ffn1_post_w))

    o = 0
    w_dn = w_in[:, o:o + 3 * DN_WIDTH]; o += 3 * DN_WIDTH
    w_z = w_in[:, o:o + DN_WIDTH]; o += DN_WIDTH
    w_ba = w_in[:, o:o + 2 * DN_HEADS]; o += 2 * DN_HEADS
    w_q = w_in[:, o:o + MB_WIDTH]; o += MB_WIDTH
    w_k = w_in[:, o:o + MB_WIDTH]; o += MB_WIDTH
    w_v = w_in[:, o:o + MB_WIDTH]; o += MB_WIDTH
    w_gd = w_in[:, o:o + d]; o += d
    w_gm = w_in[:, o:o + d]; o += d
    w_ba = jnp.pad(w_ba, ((0, 0), (0, LANES - 2 * DN_HEADS)))

    def hi_lo(w):
        hi = w.astype(BF16)
        return jnp.stack([hi, (w - hi.astype(F32)).astype(BF16)])

    dn_qkv, ba, mb_k, mb_kmean, mb_qt, mb_vt = _in_proj(
        x, row(mix_pre_w), b16(w_dn), jnp.concatenate(list(hi_lo(w_ba)), axis=1), hi_lo(w_k),
        b16(w_q.T), b16(w_v.T))

    pad_heads = lambda p: jnp.pad(p.astype(F32), (DN_HEADS, LANES - 2 * DN_HEADS)).reshape(1, LANES)
    o_dn = _deltanet(dn_qkv.reshape(bsz, t_len, -1), ba.reshape(bsz, t_len, LANES),
                     dn_conv_w.astype(F32), pad_heads(dn_a_log), pad_heads(dn_dt_bias),
                     row(dn_norm_w))

    o_mb = _moba(mb_qt, mb_k.reshape(bsz, t_len, MB_WIDTH),
                 mb_kmean.reshape(bsz, t_len // MB_BLOCK, MB_WIDTH), mb_vt)

    x = _mix_out(x, o_dn.reshape(n, DN_WIDTH), o_mb.reshape(n, MB_WIDTH), row(mix_pre_w),
                 b16(w_z), b16(w_gd), b16(w_gm), b16(w_branch_dn), b16(w_branch_mb), b16(w_out),
                 row(mix_post_w))

    x = _ffn_block(x, row(ffn2_pre_w), b16(ffn2_w_gate), b16(ffn2_w_up), b16(ffn2_w_down),
                   row(ffn2_post_w))
    return x.reshape(bsz, t_len, d)


def kernel(x, ffn1_pre_w, ffn1_w_gate, ffn1_w_up, ffn1_w_down, ffn1_post_w, mix_pre_w, w_in, dn_conv_w, dn_a_log, dn_dt_bias, dn_norm_w, w_branch_dn, w_branch_mb, w_out, mix_post_w, ffn2_pre_w, ffn2_w_gate, ffn2_w_up, ffn2_w_down, ffn2_post_w):
    depth = w_in.shape[0]
    for l in range(depth):
        x = _layer(x, ffn1_pre_w[l], ffn1_w_gate[l], ffn1_w_up[l], ffn1_w_down[l], ffn1_post_w[l],
                   mix_pre_w[l], w_in[l], dn_conv_w[l], dn_a_log[l], dn_dt_bias[l], dn_norm_w[l],
                   w_branch_dn[l], w_branch_mb[l], w_out[l], mix_post_w[l],
                   ffn2_pre_w[l], ffn2_w_gate[l], ffn2_w_up[l], ffn2_w_down[l], ffn2_post_w[l])
    return x
```

```python
import functools
import math

import jax
import jax.numpy as jnp
from jax import lax
from jax.experimental import pallas as pl
from jax.experimental.pallas import tpu as pltpu

F32 = jnp.float32
BF16 = jnp.bfloat16

NORM_EPS = 1e-6
MACARON_WEIGHT = 0.5

DN_HEADS = 4
DN_HEAD_DIM = 128
DN_WIDTH = DN_HEADS * DN_HEAD_DIM
DN_CONV = 4
DN_CHUNK = 64
DN_TILE = 256
DN_STAGE_LAG = 1 + DN_HEADS + (DN_CHUNK.bit_length() - 2)

MB_HEADS = 8
MB_HEAD_DIM = 64
MB_WIDTH = MB_HEADS * MB_HEAD_DIM
MB_BLOCK = 256
MB_TOPK = 3
ALIBI_MAX_BIAS = 8.0
LANES = 128
SUBLANES = 8
MB_PAIRS = MB_WIDTH // LANES
NEG_BIG = -1e30

VMEM_LIMIT = 56 * 1024 * 1024


def _rms(x, w):
    ms = jnp.mean(x * x, axis=-1, keepdims=True)
    return x * lax.rsqrt(ms + NORM_EPS) * w


def _dot(a, b):
    return jnp.dot(a, b, preferred_element_type=F32)


def _dot_nt(a, b):
    return lax.dot_general(a, b, (((1,), (1,)), ((), ())), preferred_element_type=F32)


def _dot_tn(a, b):
    return lax.dot_general(a, b, (((0,), (0,)), ((), ())), preferred_element_type=F32)


def _split2(x):
    hi = x.astype(BF16)
    lo = (x - hi.astype(F32)).astype(BF16)
    return hi, lo


def _const_spec(shape):
    nd = len(shape)
    return pl.BlockSpec(shape, lambda *_: (0,) * nd, pipeline_mode=pl.Buffered(1))


def _ffn_kernel(x_ref, prew_ref, wg_ref, wu_ref, wd_ref, postw_ref, o_ref):
    x = x_ref[...]
    xn = _rms(x, prew_ref[...]).astype(BF16)
    g = _dot(xn, wg_ref[...])
    u = _dot(xn, wu_ref[...])
    a = (g * jax.nn.sigmoid(g) * u).astype(BF16)
    h = _dot(a, wd_ref[...])
    o_ref[...] = x + MACARON_WEIGHT * _rms(h, postw_ref[...])


def _ffn_block(x, pre_w, w_gate, w_up, w_down, post_w, tm=512):
    n, d = x.shape
    dff = w_gate.shape[1]
    return pl.pallas_call(
        _ffn_kernel,
        grid=(n // tm,),
        in_specs=[
            pl.BlockSpec((tm, d), lambda i: (i, 0)),
            _const_spec((1, d)),
            _const_spec((d, dff)),
            _const_spec((d, dff)),
            _const_spec((dff, d)),
            _const_spec((1, d)),
        ],
        out_specs=pl.BlockSpec((tm, d), lambda i: (i, 0)),
        out_shape=jax.ShapeDtypeStruct((n, d), F32),
        compiler_params=pltpu.CompilerParams(
            dimension_semantics=("arbitrary",), vmem_limit_bytes=VMEM_LIMIT),
        name="ffn_block",
    )(x, pre_w, w_gate, w_up, w_down, post_w)


def _inproj_kernel(x_ref, prew_ref, wdn_ref, wba_ref, wk_ref, wqt_ref, wvt_ref,
                   dn_ref, ba_ref, k_ref, kmean_ref, qt_ref, vt_ref):
    bs = MB_BLOCK
    nb = qt_ref.shape[0]
    h = _rms(x_ref[...], prew_ref[...])
    h16 = h.astype(BF16)
    dn_ref[...] = _dot(h16, wdn_ref[...])
    ba2 = _dot(h16, wba_ref[...])
    ba_ref[...] = ba2[:, :LANES] + ba2[:, LANES:]
    k_ref[...] = _dot(h16, wk_ref[0]).astype(BF16)
    hbar = jnp.concatenate([jnp.mean(h[i * bs:(i + 1) * bs], axis=0, keepdims=True)
                            for i in range(nb)]
                           + [jnp.zeros((SUBLANES - nb, h.shape[1]), F32)], axis=0)
    hb_hi, hb_lo = _split2(hbar)
    kmean = _dot(hb_hi, wk_ref[0]) + _dot(hb_lo, wk_ref[0]) + _dot(hb_hi, wk_ref[1])
    kmean_ref[0] = kmean[:nb]
    qt = _dot_nt(wqt_ref[...], h16)
    vt = _dot_nt(wvt_ref[...], h16).astype(BF16)
    for i in range(nb):
        qt_ref[i] = qt[:, i * bs:(i + 1) * bs]
        vt_ref[i] = vt[:, i * bs:(i + 1) * bs]


def _in_proj(x, pre_w, w_dn, w_ba, w_k, w_qt, w_vt, tm=512):
    n, d = x.shape
    bs = MB_BLOCK
    return pl.pallas_call(
        _inproj_kernel,
        grid=(n // tm,),
        in_specs=[
            pl.BlockSpec((tm, d), lambda i: (i, 0)),
            _const_spec((1, d)),
            _const_spec(w_dn.shape),
            _const_spec(w_ba.shape),
            _const_spec(w_k.shape),
            _const_spec(w_qt.shape),
            _const_spec(w_vt.shape),
        ],
        out_specs=[
            pl.BlockSpec((tm, w_dn.shape[1]), lambda i: (i, 0)),
            pl.BlockSpec((tm, LANES), lambda i: (i, 0)),
            pl.BlockSpec((tm, MB_WIDTH), lambda i: (i, 0)),
            pl.BlockSpec((1, tm // bs, MB_WIDTH), lambda i: (i, 0, 0)),
            pl.BlockSpec((tm // bs, MB_WIDTH, bs), lambda i: (i, 0, 0)),
            pl.BlockSpec((tm // bs, MB_WIDTH, bs), lambda i: (i, 0, 0)),
        ],
        out_shape=[
            jax.ShapeDtypeStruct((n, w_dn.shape[1]), F32),
            jax.ShapeDtypeStruct((n, LANES), F32),
            jax.ShapeDtypeStruct((n, MB_WIDTH), BF16),
            jax.ShapeDtypeStruct((n // tm, tm // bs, MB_WIDTH), F32),
            jax.ShapeDtypeStruct((n // bs, MB_WIDTH, bs), F32),
            jax.ShapeDtypeStruct((n // bs, MB_WIDTH, bs), BF16),
        ],
        compiler_params=pltpu.CompilerParams(
            dimension_semantics=("arbitrary",), vmem_limit_bytes=VMEM_LIMIT),
        name="in_proj",
    )(x, pre_w, w_dn, w_ba, w_k, w_qt, w_vt)


def _dn_kernel(qkv_ref, ba_ref, convw_ref, alog_ref, dtb_ref, normw_ref, o_ref,
               xbuf_ref, state_ref):
    tt = DN_TILE
    c = DN_CHUNK
    dk = DN_HEAD_DIM
    n_batch = qkv_ref.shape[0]
    heads = range(DN_HEADS)

    @pl.when(pl.program_id(0) == 0)
    def _():
        xbuf_ref[:, 0:8, :] = jnp.zeros((n_batch, 8, 3 * DN_WIDTH), F32)
        state_ref[...] = jnp.zeros_like(state_ref)

    ri = lax.broadcasted_iota(jnp.int32, (tt, tt), 0)
    ci = lax.broadcasted_iota(jnp.int32, (tt, tt), 1)
    same_chunk = (ri // c) == (ci // c)
    incl = same_chunk & (ri >= ci)
    eye = ri == ci
    tril = jnp.where(incl, 1.0, 0.0).astype(BF16)
    ones_bd = jnp.where(same_chunk, 1.0, 0.0).astype(BF16)

    def batch_program(b):
        x = qkv_ref[b]
        xbuf_ref[b, 8:8 + tt, :] = x
        cw = convw_ref[...]
        y = x * cw[DN_CONV - 1:DN_CONV, :]
        for s in range(1, DN_CONV):
            y = y + xbuf_ref[b, 8 - s:8 - s + tt, :] * cw[DN_CONV - 1 - s:DN_CONV - s, :]
        xbuf_ref[b, 0:8, :] = x[tt - 8:tt, :]
        y = y * jax.nn.sigmoid(y)

        ba = ba_ref[b]
        beta_all = jax.nn.sigmoid(ba)
        g_all = -jnp.exp(alog_ref[...]) * jax.nn.softplus(ba + dtb_ref[...])

        g1 = g_all.astype(BF16)
        r1 = g_all - g1.astype(F32)
        g2 = r1.astype(BF16)
        g3 = (r1 - g2.astype(F32)).astype(BF16)
        gcs_all = _dot(tril, g1) + _dot(tril, g2) + _dot(tril, g3)
        gtot_all = _dot(ones_bd, g1) + _dot(ones_bd, g2) + _dot(ones_bd, g3)
        yield

        lmat, attn16, rhs, qd, kd, gtot = [], [], [], [], [], []
        for h in heads:
            qr = y[:, h * dk:(h + 1) * dk]
            kr = y[:, DN_WIDTH + h * dk:DN_WIDTH + (h + 1) * dk]
            v = y[:, 2 * DN_WIDTH + h * dk:2 * DN_WIDTH + (h + 1) * dk]
            q = qr * lax.rsqrt(jnp.sum(qr * qr, axis=-1, keepdims=True) + NORM_EPS) * (dk ** -0.5)
            k = kr * lax.rsqrt(jnp.sum(kr * kr, axis=-1, keepdims=True) + NORM_EPS)
            beta = beta_all[:, h:h + 1]
            gcs = gcs_all[:, DN_HEADS + h:DN_HEADS + h + 1]
            gtot.append(gtot_all[:, DN_HEADS + h:DN_HEADS + h + 1])
            eg = jnp.exp(gcs)

            g_row = jnp.sum(jnp.where(eye, gcs, 0.0), axis=0, keepdims=True)
            decay = jnp.exp(jnp.where(incl, gcs - g_row, NEG_BIG))

            kb = k * beta
            k16 = k.astype(BF16)
            lmat.append(jnp.where(eye, 0.0, _dot_nt(kb.astype(BF16), k16) * decay))
            attn16.append((_dot_nt(q.astype(BF16), k16) * decay).astype(BF16))
            rhs.append(jnp.concatenate([v * beta, kb * eg], axis=1).astype(BF16))
            qd.append(q * eg)
            kd.append((k * jnp.exp(gtot[h] - gcs)).astype(BF16))
            yield

        def compact(a):
            return sum(a[ch * c:(ch + 1) * c] for ch in range(1, tt // c)) + a[:c]

        def expand(a):
            return jnp.where(same_chunk, jnp.concatenate([a] * (tt // c), axis=0), 0.0)

        xc = [compact(jnp.where(eye, 1.0, -lmat[h])) for h in heads]
        mc = [_dot(compact(lmat[h]).astype(BF16), lmat[h].astype(BF16)) for h in heads]
        power = 2
        while power < c:
            m_bd = [expand(mc[h]).astype(BF16) for h in heads]
            if 2 * power < c:
                xm = [_dot(jnp.concatenate([xc[h], mc[h]], axis=0).astype(BF16), m_bd[h])
                      for h in heads]
                xc = [xc[h] + xm[h][:c] for h in heads]
                mc = [xm[h][c:] for h in heads]
            else:
                xc = [xc[h] + _dot(xc[h].astype(BF16), m_bd[h]) for h in heads]
            power *= 2
            yield

        uw16 = [_dot(expand(xc[h]).astype(BF16), rhs[h]).astype(BF16) for h in heads]
        au_aw = [_dot(attn16[h], uw16[h]) for h in heads]
        au = [au_aw[h][:, :dk] for h in heads]
        e16 = [(qd[h] - au_aw[h][:, dk:]).astype(BF16) for h in heads]
        yield

        s = [state_ref[b * DN_HEADS + h] for h in heads]
        outs = [[] for _ in heads]
        for ch in range(tt // c):
            lo, hi = ch * c, (ch + 1) * c
            bc = [_dot_tn(kd[h][lo:hi], uw16[h][lo:hi]) for h in heads]
            for h in heads:
                s16 = s[h].astype(BF16)
                outs[h].append(_dot(e16[h][lo:hi], s16) + au[h][lo:hi])
                s[h] = (s[h] * jnp.exp(gtot[h][lo:lo + 1, :]) + bc[h][:, :dk]
                        - _dot(bc[h][:, dk:].astype(BF16), s16))
            yield
        for h in heads:
            state_ref[b * DN_HEADS + h] = s[h]
            o = jnp.concatenate(outs[h], axis=0)
            o_ref[b, :, h * dk:(h + 1) * dk] = _rms(o, normw_ref[...])
        yield

    programs = [batch_program(b) for b in range(n_batch)]
    live = [True] * n_batch
    wave = 0
    while any(live):
        for b in range(n_batch):
            if live[b] and wave >= DN_STAGE_LAG * b:
                live[b] = next(programs[b], "done") != "done"
        wave += 1


def _deltanet(qkv, ba, conv_w, alog_row, dtb_row, norm_w):
    bsz, t_len, width = qkv.shape
    tt = DN_TILE
    return pl.pallas_call(
        _dn_kernel,
        grid=(t_len // tt,),
        in_specs=[
            pl.BlockSpec((bsz, tt, width), lambda t: (0, t, 0)),
            pl.BlockSpec((bsz, tt, LANES), lambda t: (0, t, 0)),
            _const_spec(conv_w.shape),
            _const_spec((1, LANES)),
            _const_spec((1, LANES)),
            _const_spec((1, DN_HEAD_DIM)),
        ],
        out_specs=pl.BlockSpec((bsz, tt, DN_WIDTH), lambda t: (0, t, 0)),
        out_shape=jax.ShapeDtypeStruct((bsz, t_len, DN_WIDTH), F32),
        scratch_shapes=[
            pltpu.VMEM((bsz, 8 + tt, width), F32),
            pltpu.VMEM((bsz * DN_HEADS, DN_HEAD_DIM, DN_HEAD_DIM), F32),
        ],
        compiler_params=pltpu.CompilerParams(
            dimension_semantics=("arbitrary",), vmem_limit_bytes=VMEM_LIMIT),
        name="deltanet",
    )(qkv, ba, conv_w, alog_row, dtb_row, norm_w)


MB_SUPER = 1
MB_GANG = 2
MB_UNROLL = 4
MB_AUX_MASK = 16
MB_SUM_ROWS = 16
ALIBI_STEP = int(ALIBI_MAX_BIAS) // MB_HEADS
assert ALIBI_STEP * MB_HEADS == ALIBI_MAX_BIAS
LOG2E = math.log2(math.e)
LOG2E_PIECES = (1.4453125, -0.00262451171875, 7.063150405883789e-06, -1.05355866253376e-08)


def _moba_kernel(qt_ref, k_ref, kmean_ref, vt_ref, o_ref, kaug_ref, sa_ref, sb_ref, *, n_blk):
    bs = MB_BLOCK
    hd = MB_HEAD_DIM
    sup = MB_SUPER * bs
    nbp = -(-n_blk // SUBLANES) * SUBLANES
    gang = pl.program_id(1)
    own = pl.program_id(2)
    pairs = range(MB_GANG)
    lane = lax.broadcasted_iota(jnp.int32, (bs, LANES), 1)
    row = lax.broadcasted_iota(jnp.int32, (bs, LANES), 0)

    @pl.when(own == 0)
    def _():
        def build(j, carry):
            off = pl.multiple_of(j * bs, bs)
            kstart = jnp.full((bs, LANES), j * bs, jnp.int32).astype(F32)
            aux = jnp.where(lane < 2, 1.0,
                            jnp.where(lane < 6, row.astype(F32),
                                      jnp.where(lane < 10, kstart,
                                                jnp.where(lane == MB_AUX_MASK + j, 1.0, 0.0))))
            for pp in pairs:
                kaug_ref[pp, pl.ds(off, bs), 0:LANES] = k_ref[0, pl.ds(off, bs),
                                                              pp * LANES:(pp + 1) * LANES]
                kaug_ref[pp, pl.ds(off, bs), LANES:2 * LANES] = aux.astype(BF16)
            return carry

        lax.fori_loop(0, n_blk, build, 0)

    chan = lax.broadcasted_iota(jnp.int32, (LANES, bs), 0)
    blk = lax.broadcasted_iota(jnp.int32, (nbp, bs), 0)
    blk_f = blk.astype(F32)
    aux_row = lax.broadcasted_iota(jnp.int32, (MB_AUX_MASK, bs), 0)
    qpos = (lax.broadcasted_iota(jnp.int32, (MB_AUX_MASK, bs), 1) + own * bs).astype(F32)
    aux_pad = jnp.zeros((LANES - MB_AUX_MASK - nbp, bs), F32)

    heads = [(pp, hh) for pp in pairs for hh in range(2)]
    qth, gate = [], []
    for pp, hh in heads:
        qt = qt_ref[0, pp * LANES:(pp + 1) * LANES, :]
        km_hi, km_lo = _split2(kmean_ref[0, :, pp * LANES:(pp + 1) * LANES])
        qth.append(jnp.where((chan >= hh * hd) & (chan < (hh + 1) * hd), qt, 0.0))
        q_hi, q_lo = _split2(qth[-1])
        g = _dot(km_hi, q_hi) + _dot(km_hi, q_lo) + _dot(km_lo, q_hi)
        gate.append(jnp.where(blk < own, g, -jnp.inf))

    piece_id = (aux_row + 2) & 3
    piece = jnp.where(piece_id == 0, LOG2E_PIECES[0],
                      jnp.where(piece_id == 1, LOG2E_PIECES[1],
                                jnp.where(piece_id == 2, LOG2E_PIECES[2], LOG2E_PIECES[3])))
    q_rows = []
    for n, (pp, hh) in enumerate(heads):
        head = 2 * (MB_GANG * gang + pp) + hh
        slope_bits = (127 - ALIBI_STEP * (head + 1)) << 23
        slope = lax.bitcast_convert_type(jnp.full((MB_AUX_MASK, bs), slope_bits, jnp.int32), F32)
        qconst = -(slope * LOG2E) * qpos
        qconst_hi = qconst.astype(BF16).astype(F32)
        bias_rows = jnp.where(aux_row == 0, qconst_hi,
                              jnp.where(aux_row == 1, qconst - qconst_hi,
                                        jnp.where(aux_row < 10, slope * piece, 0.0)))
        q_rows.append(jnp.concatenate([qth[n] * (hd ** -0.5 * LOG2E), bias_rows], axis=0))

    def query_operand(mask_rows):
        ops = [jnp.concatenate([q_rows[n], mask_rows[n], aux_pad], axis=0).astype(BF16)
               for n in range(len(heads))]
        return [jnp.concatenate(ops[2 * pp:2 * pp + 2], axis=1) for pp in pairs]

    def keys(pp, i):
        return kaug_ref[pp, pl.ds(pl.multiple_of(i * sup, sup), sup), :]

    def values_t(i, pp, hh):
        lo = pp * LANES + hh * hd
        return jnp.concatenate([vt_ref[i * MB_SUPER + u, lo:lo + hd, :]
                                for u in range(MB_SUPER)], axis=1)

    def produce(s_ref, pp, g, q_op, mask=None):
        s2 = _dot(keys(pp, g), q_op[pp])
        if mask is not None:
            s2 = jnp.where(mask, NEG_BIG, s2)
        s_ref[pp] = s2
        return jnp.max(s2, axis=0, keepdims=True)

    n_grp = n_blk // MB_SUPER
    grp = own // MB_SUPER

    assert MB_SUPER == 1
    rel = (lax.broadcasted_iota(jnp.int32, (sup, 2 * bs), 0) - (own - grp * MB_SUPER) * bs)
    qi = lax.broadcasted_iota(jnp.int32, (sup, 2 * bs), 1) & (bs - 1)
    future = (rel > qi) & (rel < bs)
    q_own = query_operand([jnp.zeros((nbp, bs), F32)] * len(heads))
    smax_a0 = [produce(sa_ref, pp, grp, q_own, future) for pp in pairs]

    sel = [jnp.zeros((nbp, bs), F32) for _ in heads]
    for _ in range(MB_TOPK):
        mx = [jnp.max(g, axis=0, keepdims=True) for g in gate]
        first = [jnp.min(jnp.where(g == m, blk_f, float(nbp)), axis=0, keepdims=True)
                 for g, m in zip(gate, mx)]
        hit = [blk_f == f for f in first]
        sel = [jnp.where(h, 1.0, s) for h, s in zip(hit, sel)]
        gate = [jnp.where(h, -jnp.inf, g) for h, g in zip(hit, gate)]
    keep = [jnp.where(blk < own, s, jnp.where(blk == own, 1.0, 0.0)) for s in sel]
    qaug2 = query_operand([jnp.where(k > 0.5, 0.0, NEG_BIG) for k in keep])

    def group_at(t):
        g = jnp.where(t == 0, grp, jnp.where(t > grp, grp + 1, t - 1))
        return jnp.minimum(g, n_grp - 1)

    ones_rows = jnp.ones((MB_SUM_ROWS, sup), BF16)

    def softmax_step(s_ref, pp, smax, g, carry):
        hs = range(2)
        m_i = [carry[2 * hh] for hh in hs]
        m_new = [jnp.maximum(m_i[hh], smax[:, hh * bs:(hh + 1) * bs]) for hh in hs]
        alpha = [jnp.exp2(m_i[hh] - m_new[hh]) for hh in hs]
        pexp = [jnp.exp2((s_ref[pp, :, hh * bs:(hh + 1) * bs] - m_new[hh]).astype(BF16))
                for hh in hs]
        acc_new = [carry[2 * hh + 1] * alpha[hh]
                   + _dot(jnp.concatenate([values_t(g, pp, hh), ones_rows], axis=0), pexp[hh])
                   for hh in hs]
        return (m_new[0], acc_new[0], m_new[1], acc_new[1])

    def steps(u, carry, unroll, t0):
        t = t0 + unroll * u
        smax_a = list(carry[:MB_GANG])
        stats = [carry[MB_GANG + 4 * pp:MB_GANG + 4 * pp + 4] for pp in pairs]
        for v in range(0, unroll, 2):
            smax_b = [produce(sb_ref, pp, group_at(t + v + 1), qaug2) for pp in pairs]
            stats = [softmax_step(sa_ref, pp, smax_a[pp], group_at(t + v), stats[pp])
                     for pp in pairs]
            smax_a = [produce(sa_ref, pp, group_at(t + v + 2), qaug2) for pp in pairs]
            stats = [softmax_step(sb_ref, pp, smax_b[pp], group_at(t + v + 1), stats[pp])
                     for pp in pairs]
        out = tuple(smax_a)
        for pp in pairs:
            out += tuple(stats[pp])
        return out

    stat0 = jnp.full((1, bs), -jnp.inf, F32)
    acc0 = jnp.zeros((hd + MB_SUM_ROWS, bs), F32)
    n_full = (grp + 1) // MB_UNROLL
    n_tail = (grp + 2 - n_full * MB_UNROLL) // 2
    fin = lax.fori_loop(0, n_full, functools.partial(steps, unroll=MB_UNROLL, t0=0),
                        tuple(smax_a0) + (stat0, acc0) * len(heads))
    fin = lax.fori_loop(0, n_tail, functools.partial(steps, unroll=2, t0=n_full * MB_UNROLL), fin)
    accs = [fin[MB_GANG + 2 * n + 1] for n in range(len(heads))]
    out_t = jnp.concatenate([a[:hd] / a[hd:hd + 1] for a in accs], axis=0)
    o_ref[0] = out_t.T


def _moba(qt, k, kmean, vt):
    bsz, t_len, _ = k.shape
    bs = MB_BLOCK
    n_blk = t_len // bs
    assert MB_UNROLL % 2 == 0 and n_blk % (MB_UNROLL * MB_SUPER) == 0
    assert MB_AUX_MASK + n_blk <= LANES
    assert n_blk % SUBLANES == 0 and MB_PAIRS % MB_GANG == 0
    gw = MB_GANG * LANES
    return pl.pallas_call(
        functools.partial(_moba_kernel, n_blk=n_blk),
        grid=(bsz, MB_PAIRS // MB_GANG, n_blk),
        in_specs=[
            pl.BlockSpec((1, gw, bs), lambda b, p, i: (b * n_blk + i, p, 0)),
            pl.BlockSpec((1, t_len, gw), lambda b, p, i: (b, 0, p)),
            pl.BlockSpec((1, n_blk, gw), lambda b, p, i: (b, 0, p)),
            pl.BlockSpec((n_blk, gw, bs), lambda b, p, i: (b, p, 0)),
        ],
        out_specs=pl.BlockSpec((1, bs, gw), lambda b, p, i: (b, i, p)),
        out_shape=jax.ShapeDtypeStruct((bsz, t_len, MB_WIDTH), F32),
        scratch_shapes=[
            pltpu.VMEM((MB_GANG, t_len, 2 * LANES), BF16),
            pltpu.VMEM((MB_GANG, MB_SUPER * bs, 2 * bs), F32),
            pltpu.VMEM((MB_GANG, MB_SUPER * bs, 2 * bs), F32),
        ],
        compiler_params=pltpu.CompilerParams(
            dimension_semantics=("arbitrary", "arbitrary", "arbitrary"),
            vmem_limit_bytes=VMEM_LIMIT),
        name="moba",
    )(qt, k, kmean, vt)


def _mixout_kernel(x_ref, odn_ref, omb_ref, prew_ref, wz_ref, wgd_ref, wgm_ref,
                   wbd_ref, wbm_ref, wo_ref, postw_ref, o_ref):
    x = x_ref[...]
    h = _rms(x, prew_ref[...]).astype(BF16)
    z = _dot(h, wz_ref[...])
    gate_dn = jax.nn.sigmoid(_dot(h, wgd_ref[...]))
    gate_mb = jax.nn.sigmoid(_dot(h, wgm_ref[...]))
    o_dn = odn_ref[...] * (z * jax.nn.sigmoid(z))
    y_dn = _dot(o_dn.astype(BF16), wbd_ref[...])
    y_mb = _dot(omb_ref[...].astype(BF16), wbm_ref[...])
    merged = gate_dn * y_dn + gate_mb * y_mb
    y = _dot(merged.astype(BF16), wo_ref[...])
    o_ref[...] = x + _rms(y, postw_ref[...])


def _mix_out(x, o_dn, o_mb, pre_w, w_z, w_gd, w_gm, w_bd, w_bm, w_o, post_w, tm=512):
    n, d = x.shape
    return pl.pallas_call(
        _mixout_kernel,
        grid=(n // tm,),
        in_specs=[
            pl.BlockSpec((tm, d), lambda i: (i, 0)),
            pl.BlockSpec((tm, o_dn.shape[1]), lambda i: (i, 0)),
            pl.BlockSpec((tm, o_mb.shape[1]), lambda i: (i, 0)),
            _const_spec((1, d)),
            _const_spec(w_z.shape),
            _const_spec(w_gd.shape),
            _const_spec(w_gm.shape),
            _const_spec(w_bd.shape),
            _const_spec(w_bm.shape),
            _const_spec(w_o.shape),
            _const_spec((1, d)),
        ],
        out_specs=pl.BlockSpec((tm, d), lambda i: (i, 0)),
        out_shape=jax.ShapeDtypeStruct((n, d), F32),
        compiler_params=pltpu.CompilerParams(
            dimension_semantics=("arbitrary",), vmem_limit_bytes=VMEM_LIMIT),
        name="mix_out",
    )(x, o_dn, o_mb, pre_w, w_z, w_gd, w_gm, w_bd, w_bm, w_o, post_w)


def _layer(x, ffn1_pre_w, ffn1_w_gate, ffn1_w_up, ffn1_w_down, ffn1_post_w,
           mix_pre_w, w_in, dn_conv_w, dn_a_log, dn_dt_bias, dn_norm_w,
           w_branch_dn, w_branch_mb, w_out, mix_post_w,
           ffn2_pre_w, ffn2_w_gate, ffn2_w_up, ffn2_w_down, ffn2_post_w):
    bsz, t_len, d = x.shape
    n = bsz * t_len
    row = lambda w: w.reshape(1, -1).astype(F32)
    b16 = lambda w: w.astype(BF16)

    x = x.reshape(n, d)
    x = _ffn_block(x, row(ffn1_pre_w), b16(ffn1_w_gate), b16(ffn1_w_up), b16(ffn1_w_down),
                   row(ffn1_post_w))

    o = 0
    w_dn = w_in[:, o:o + 3 * DN_WIDTH]; o += 3 * DN_WIDTH
    w_z = w_in[:, o:o + DN_WIDTH]; o += DN_WIDTH
    w_ba = w_in[:, o:o + 2 * DN_HEADS]; o += 2 * DN_HEADS
    w_q = w_in[:, o:o + MB_WIDTH]; o += MB_WIDTH
    w_k = w_in[:, o:o + MB_WIDTH]; o += MB_WIDTH
    w_v = w_in[:, o:o + MB_WIDTH]; o += MB_WIDTH
    w_gd = w_in[:, o:o + d]; o += d
    w_gm = w_in[:, o:o + d]; o += d
    w_ba = jnp.pad(w_ba, ((0, 0), (0, LANES - 2 * DN_HEADS)))

    def hi_lo(w):
        hi = w.astype(BF16)
        return jnp.stack([hi, (w - hi.astype(F32)).astype(BF16)])

    dn_qkv, ba, mb_k, mb_kmean, mb_qt, mb_vt = _in_proj(
        x, row(mix_pre_w), b16(w_dn), jnp.concatenate(list(hi_lo(w_ba)), axis=1), hi_lo(w_k),
        b16(w_q.T), b16(w_v.T))

    pad_heads = lambda p: jnp.pad(p.astype(F32), (DN_HEADS, LANES - 2 * DN_HEADS)).reshape(1, LANES)
    o_dn = _deltanet(dn_qkv.reshape(bsz, t_len, -1), ba.reshape(bsz, t_len, LANES),
                     dn_conv_w.astype(F32), pad_heads(dn_a_log), pad_heads(dn_dt_bias),
                     row(dn_norm_w))

    o_mb = _moba(mb_qt, mb_k.reshape(bsz, t_len, MB_WIDTH),
                 mb_kmean.reshape(bsz, t_len // MB_BLOCK, MB_WIDTH), mb_vt)

    x = _mix_out(x, o_dn.reshape(n, DN_WIDTH), o_mb.reshape(n, MB_WIDTH), row(mix_pre_w),
                 b16(w_z), b16(w_gd), b16(w_gm), b16(w_branch_dn), b16(w_branch_mb), b16(w_out),
                 row(mix_post_w))

    x = _ffn_block(x, row(ffn2_pre_w), b16(ffn2_w_gate), b16(ffn2_w_up), b16(ffn2_w_down),
                   row(ffn2_post_w))
    return x.reshape(bsz, t_len, d)


def kernel(x, ffn1_pre_w, ffn1_w_gate, ffn1_w_up, ffn1_w_down, ffn1_post_w, mix_pre_w, w_in, dn_conv_w, dn_a_log, dn_dt_bias, dn_norm_w, w_branch_dn, w_branch_mb, w_out, mix_post_w, ffn2_pre_w, ffn2_w_gate, ffn2_w_up, ffn2_w_down, ffn2_post_w):
    depth = w_in.shape[0]
    for l in range(depth):
        x = _layer(x, ffn1_pre_w[l], ffn1_w_gate[l], ffn1_w_up[l], ffn1_w_down[l], ffn1_post_w[l],
                   mix_pre_w[l], w_in[l], dn_conv_w[l], dn_a_log[l], dn_dt_bias[l], dn_norm_w[l],
                   w_branch_dn[l], w_branch_mb[l], w_out[l], mix_post_w[l],
                   ffn2_pre_w[l], ffn2_w_gate[l], ffn2_w_up[l], ffn2_w_down[l], ffn2_post_w[l])
    return x
```

```python
import functools
import math

import jax
import jax.numpy as jnp
from jax import lax
from jax.experimental import pallas as pl
from jax.experimental.pallas import tpu as pltpu

F32 = jnp.float32
BF16 = jnp.bfloat16

NORM_EPS = 1e-6
MACARON_WEIGHT = 0.5

DN_HEADS = 4
DN_HEAD_DIM = 128
DN_WIDTH = DN_HEADS * DN_HEAD_DIM
DN_CONV = 4
DN_CHUNK = 64
DN_TILE = 256
DN_STAGE_LAG = 1 + DN_HEADS + (DN_CHUNK.bit_length() - 2)

MB_HEADS = 8
MB_HEAD_DIM = 64
MB_WIDTH = MB_HEADS * MB_HEAD_DIM
MB_BLOCK = 256
MB_TOPK = 3
ALIBI_MAX_BIAS = 8.0
LANES = 128
SUBLANES = 8
MB_PAIRS = MB_WIDTH // LANES
NEG_BIG = -1e30

VMEM_LIMIT = 56 * 1024 * 1024


def _rms(x, w):
    ms = jnp.mean(x * x, axis=-1, keepdims=True)
    return x * lax.rsqrt(ms + NORM_EPS) * w


def _dot(a, b):
    return jnp.dot(a, b, preferred_element_type=F32)


def _dot_nt(a, b):
    return lax.dot_general(a, b, (((1,), (1,)), ((), ())), preferred_element_type=F32)


def _dot_tn(a, b):
    return lax.dot_general(a, b, (((0,), (0,)), ((), ())), preferred_element_type=F32)


def _dot_tt(a, b):
    return lax.dot_general(a, b, (((0,), (1,)), ((), ())), preferred_element_type=F32)


def _split2(x):
    hi = x.astype(BF16)
    lo = (x - hi.astype(F32)).astype(BF16)
    return hi, lo


def _const_spec(shape):
    nd = len(shape)
    return pl.BlockSpec(shape, lambda *_: (0,) * nd, pipeline_mode=pl.Buffered(1))


def _ffn_kernel(x_ref, prew_ref, wg_ref, wu_ref, wd_ref, postw_ref, o_ref):
    x = x_ref[...]
    xn = _rms(x, prew_ref[...]).astype(BF16)
    g = _dot(xn, wg_ref[...])
    u = _dot(xn, wu_ref[...])
    a = (g * jax.nn.sigmoid(g) * u).astype(BF16)
    h = _dot(a, wd_ref[...])
    o_ref[...] = x + MACARON_WEIGHT * _rms(h, postw_ref[...])


def _ffn_block(x, pre_w, w_gate, w_up, w_down, post_w, tm=512):
    n, d = x.shape
    dff = w_gate.shape[1]
    return pl.pallas_call(
        _ffn_kernel,
        grid=(n // tm,),
        in_specs=[
            pl.BlockSpec((tm, d), lambda i: (i, 0)),
            _const_spec((1, d)),
            _const_spec((d, dff)),
            _const_spec((d, dff)),
            _const_spec((dff, d)),
            _const_spec((1, d)),
        ],
        out_specs=pl.BlockSpec((tm, d), lambda i: (i, 0)),
        out_shape=jax.ShapeDtypeStruct((n, d), F32),
        compiler_params=pltpu.CompilerParams(
            dimension_semantics=("arbitrary",), vmem_limit_bytes=VMEM_LIMIT),
        name="ffn_block",
    )(x, pre_w, w_gate, w_up, w_down, post_w)


def _inproj_kernel(x_ref, prew_ref, wdn_ref, wba_ref, wk_ref, wq_ref, wv_ref,
                   dn_ref, ba_ref, k_ref, kmean_ref, qt_ref, vt_ref):
    bs = MB_BLOCK
    nb = qt_ref.shape[0]
    h = _rms(x_ref[...], prew_ref[...])
    h16 = h.astype(BF16)
    dn_ref[...] = _dot(h16, wdn_ref[...])
    ba2 = _dot(h16, wba_ref[...])
    ba_ref[...] = ba2[:, :LANES] + ba2[:, LANES:]
    k_ref[...] = _dot(h16, wk_ref[0]).astype(BF16)
    hbar = jnp.concatenate([jnp.mean(h[i * bs:(i + 1) * bs], axis=0, keepdims=True)
                            for i in range(nb)]
                           + [jnp.zeros((SUBLANES - nb, h.shape[1]), F32)], axis=0)
    hb_hi, hb_lo = _split2(hbar)
    kmean = _dot(hb_hi, wk_ref[0]) + _dot(hb_lo, wk_ref[0]) + _dot(hb_hi, wk_ref[1])
    kmean_ref[0] = kmean[:nb]
    qt = _dot_tt(wq_ref[...], h16)
    vt = _dot_tt(wv_ref[...], h16).astype(BF16)
    for i in range(nb):
        qt_ref[i] = qt[:, i * bs:(i + 1) * bs]
        vt_ref[i] = vt[:, i * bs:(i + 1) * bs]


def _in_proj(x, pre_w, w_dn, w_ba, w_k, w_qt, w_vt, tm=512):
    n, d = x.shape
    bs = MB_BLOCK
    return pl.pallas_call(
        _inproj_kernel,
        grid=(n // tm,),
        in_specs=[
            pl.BlockSpec((tm, d), lambda i: (i, 0)),
            _const_spec((1, d)),
            _const_spec(w_dn.shape),
            _const_spec(w_ba.shape),
            _const_spec(w_k.shape),
            _const_spec(w_qt.shape),
            _const_spec(w_vt.shape),
        ],
        out_specs=[
            pl.BlockSpec((tm, w_dn.shape[1]), lambda i: (i, 0)),
            pl.BlockSpec((tm, LANES), lambda i: (i, 0)),
            pl.BlockSpec((tm, MB_WIDTH), lambda i: (i, 0)),
            pl.BlockSpec((1, tm // bs, MB_WIDTH), lambda i: (i, 0, 0)),
            pl.BlockSpec((tm // bs, MB_WIDTH, bs), lambda i: (i, 0, 0)),
            pl.BlockSpec((tm // bs, MB_WIDTH, bs), lambda i: (i, 0, 0)),
        ],
        out_shape=[
            jax.ShapeDtypeStruct((n, w_dn.shape[1]), F32),
            jax.ShapeDtypeStruct((n, LANES), F32),
            jax.ShapeDtypeStruct((n, MB_WIDTH), BF16),
            jax.ShapeDtypeStruct((n // tm, tm // bs, MB_WIDTH), F32),
            jax.ShapeDtypeStruct((n // bs, MB_WIDTH, bs), F32),
            jax.ShapeDtypeStruct((n // bs, MB_WIDTH, bs), BF16),
        ],
        compiler_params=pltpu.CompilerParams(
            dimension_semantics=("arbitrary",), vmem_limit_bytes=VMEM_LIMIT),
        name="in_proj",
    )(x, pre_w, w_dn, w_ba, w_k, w_qt, w_vt)


def _dn_kernel(qkv_ref, ba_ref, convw_ref, alog_ref, dtb_ref, normw_ref, o_ref,
               xbuf_ref, state_ref):
    tt = DN_TILE
    c = DN_CHUNK
    dk = DN_HEAD_DIM
    n_batch = qkv_ref.shape[0]
    heads = range(DN_HEADS)

    @pl.when(pl.program_id(0) == 0)
    def _():
        xbuf_ref[:, 0:8, :] = jnp.zeros((n_batch, 8, 3 * DN_WIDTH), F32)
        state_ref[...] = jnp.zeros_like(state_ref)

    ri = lax.broadcasted_iota(jnp.int32, (tt, tt), 0)
    ci = lax.broadcasted_iota(jnp.int32, (tt, tt), 1)
    same_chunk = (ri // c) == (ci // c)
    incl = same_chunk & (ri >= ci)
    eye = ri == ci
    tril = jnp.where(incl, 1.0, 0.0).astype(BF16)
    ones_bd = jnp.where(same_chunk, 1.0, 0.0).astype(BF16)

    def batch_program(b):
        x = qkv_ref[b]
        xbuf_ref[b, 8:8 + tt, :] = x
        cw = convw_ref[...]
        y = x * cw[DN_CONV - 1:DN_CONV, :]
        for s in range(1, DN_CONV):
            y = y + xbuf_ref[b, 8 - s:8 - s + tt, :] * cw[DN_CONV - 1 - s:DN_CONV - s, :]
        xbuf_ref[b, 0:8, :] = x[tt - 8:tt, :]
        y = y * jax.nn.sigmoid(y)

        ba = ba_ref[b]
        beta_all = jax.nn.sigmoid(ba)
        g_all = -jnp.exp(alog_ref[...]) * jax.nn.softplus(ba + dtb_ref[...])

        g1 = g_all.astype(BF16)
        r1 = g_all - g1.astype(F32)
        g2 = r1.astype(BF16)
        g3 = (r1 - g2.astype(F32)).astype(BF16)
        gcs_all = _dot(tril, g1) + _dot(tril, g2) + _dot(tril, g3)
        gtot_all = _dot(ones_bd, g1) + _dot(ones_bd, g2) + _dot(ones_bd, g3)
        yield

        lmat, attn16, rhs, qd, kd, gtot = [], [], [], [], [], []
        for h in heads:
            qr = y[:, h * dk:(h + 1) * dk]
            kr = y[:, DN_WIDTH + h * dk:DN_WIDTH + (h + 1) * dk]
            v = y[:, 2 * DN_WIDTH + h * dk:2 * DN_WIDTH + (h + 1) * dk]
            q = qr * lax.rsqrt(jnp.sum(qr * qr, axis=-1, keepdims=True) + NORM_EPS) * (dk ** -0.5)
            k = kr * lax.rsqrt(jnp.sum(kr * kr, axis=-1, keepdims=True) + NORM_EPS)
            beta = beta_all[:, h:h + 1]
            gcs = gcs_all[:, DN_HEADS + h:DN_HEADS + h + 1]
            gtot.append(gtot_all[:, DN_HEADS + h:DN_HEADS + h + 1])
            eg = jnp.exp(gcs)

            g_row = jnp.sum(jnp.where(eye, gcs, 0.0), axis=0, keepdims=True)
            decay = jnp.exp(jnp.where(incl, gcs - g_row, NEG_BIG))

            kb = k * beta
            k16 = k.astype(BF16)
            lmat.append(jnp.where(eye, 0.0, _dot_nt(kb.astype(BF16), k16) * decay))
            attn16.append((_dot_nt(q.astype(BF16), k16) * decay).astype(BF16))
            rhs.append(jnp.concatenate([v * beta, kb * eg], axis=1).astype(BF16))
            qd.append(q * eg)
            kd.append((k * jnp.exp(gtot[h] - gcs)).astype(BF16))
            yield

        def compact(a):
            return sum(a[ch * c:(ch + 1) * c] for ch in range(1, tt // c)) + a[:c]

        def expand(a):
            return jnp.where(same_chunk, jnp.concatenate([a] * (tt // c), axis=0), 0.0)

        xc = [compact(jnp.where(eye, 1.0, -lmat[h])) for h in heads]
        mc = [_dot(compact(lmat[h]).astype(BF16), lmat[h].astype(BF16)) for h in heads]
        power = 2
        while power < c:
            m_bd = [expand(mc[h]).astype(BF16) for h in heads]
            if 2 * power < c:
                xm = [_dot(jnp.concatenate([xc[h], mc[h]], axis=0).astype(BF16), m_bd[h])
                      for h in heads]
                xc = [xc[h] + xm[h][:c] for h in heads]
                mc = [xm[h][c:] for h in heads]
            else:
                xc = [xc[h] + _dot(xc[h].astype(BF16), m_bd[h]) for h in heads]
            power *= 2
            yield

        uw16 = [_dot(expand(xc[h]).astype(BF16), rhs[h]).astype(BF16) for h in heads]
        au_aw = [_dot(attn16[h], uw16[h]) for h in heads]
        au = [au_aw[h][:, :dk] for h in heads]
        e16 = [(qd[h] - au_aw[h][:, dk:]).astype(BF16) for h in heads]
        yield

        s = [state_ref[b * DN_HEADS + h] for h in heads]
        outs = [[] for _ in heads]
        for ch in range(tt // c):
            lo, hi = ch * c, (ch + 1) * c
            bc = [_dot_tn(kd[h][lo:hi], uw16[h][lo:hi]) for h in heads]
            for h in heads:
                s16 = s[h].astype(BF16)
                outs[h].append(_dot(e16[h][lo:hi], s16) + au[h][lo:hi])
                s[h] = (s[h] * jnp.exp(gtot[h][lo:lo + 1, :]) + bc[h][:, :dk]
                        - _dot(bc[h][:, dk:].astype(BF16), s16))
            yield
        for h in heads:
            state_ref[b * DN_HEADS + h] = s[h]
            o = jnp.concatenate(outs[h], axis=0)
            o_ref[b, :, h * dk:(h + 1) * dk] = _rms(o, normw_ref[...])
        yield

    programs = [batch_program(b) for b in range(n_batch)]
    live = [True] * n_batch
    wave = 0
    while any(live):
        for b in range(n_batch):
            if live[b] and wave >= DN_STAGE_LAG * b:
                live[b] = next(programs[b], "done") != "done"
        wave += 1


def _deltanet(qkv, ba, conv_w, alog_row, dtb_row, norm_w):
    bsz, t_len, width = qkv.shape
    tt = DN_TILE
    return pl.pallas_call(
        _dn_kernel,
        grid=(t_len // tt,),
        in_specs=[
            pl.BlockSpec((bsz, tt, width), lambda t: (0, t, 0)),
            pl.BlockSpec((bsz, tt, LANES), lambda t: (0, t, 0)),
            _const_spec(conv_w.shape),
            _const_spec((1, LANES)),
            _const_spec((1, LANES)),
            _const_spec((1, DN_HEAD_DIM)),
        ],
        out_specs=pl.BlockSpec((bsz, tt, DN_WIDTH), lambda t: (0, t, 0)),
        out_shape=jax.ShapeDtypeStruct((bsz, t_len, DN_WIDTH), F32),
        scratch_shapes=[
            pltpu.VMEM((bsz, 8 + tt, width), F32),
            pltpu.VMEM((bsz * DN_HEADS, DN_HEAD_DIM, DN_HEAD_DIM), F32),
        ],
        compiler_params=pltpu.CompilerParams(
            dimension_semantics=("arbitrary",), vmem_limit_bytes=VMEM_LIMIT),
        name="deltanet",
    )(qkv, ba, conv_w, alog_row, dtb_row, norm_w)


MB_SUPER = 1
MB_GANG = 2
MB_UNROLL = 4
MB_AUX_MASK = 16
MB_SUM_ROWS = 16
ALIBI_STEP = int(ALIBI_MAX_BIAS) // MB_HEADS
assert ALIBI_STEP * MB_HEADS == ALIBI_MAX_BIAS
LOG2E = math.log2(math.e)
LOG2E_PIECES = (1.4453125, -0.00262451171875, 7.063150405883789e-06, -1.05355866253376e-08)


def _moba_kernel(qt_ref, k_ref, kmean_ref, vt_ref, o_ref, kaug_ref, sa_ref, sb_ref, *, n_blk):
    bs = MB_BLOCK
    hd = MB_HEAD_DIM
    sup = MB_SUPER * bs
    nbp = -(-n_blk // SUBLANES) * SUBLANES
    gang = pl.program_id(1)
    own = pl.program_id(2)
    pairs = range(MB_GANG)
    lane = lax.broadcasted_iota(jnp.int32, (bs, LANES), 1)
    row = lax.broadcasted_iota(jnp.int32, (bs, LANES), 0)

    @pl.when(own == 0)
    def _():
        def build(j, carry):
            off = pl.multiple_of(j * bs, bs)
            kstart = jnp.full((bs, LANES), j * bs, jnp.int32).astype(F32)
            aux = jnp.where(lane < 2, 1.0,
                            jnp.where(lane < 6, row.astype(F32),
                                      jnp.where(lane < 10, kstart,
                                                jnp.where(lane == MB_AUX_MASK + j, 1.0, 0.0))))
            for pp in pairs:
                kaug_ref[pp, pl.ds(off, bs), 0:LANES] = k_ref[0, pl.ds(off, bs),
                                                              pp * LANES:(pp + 1) * LANES]
                kaug_ref[pp, pl.ds(off, bs), LANES:2 * LANES] = aux.astype(BF16)
            return carry

        lax.fori_loop(0, n_blk, build, 0)

    chan = lax.broadcasted_iota(jnp.int32, (LANES, bs), 0)
    blk = lax.broadcasted_iota(jnp.int32, (nbp, bs), 0)
    blk_f = blk.astype(F32)
    aux_row = lax.broadcasted_iota(jnp.int32, (MB_AUX_MASK, bs), 0)
    qpos = (lax.broadcasted_iota(jnp.int32, (MB_AUX_MASK, bs), 1) + own * bs).astype(F32)
    aux_pad = jnp.zeros((LANES - MB_AUX_MASK - nbp, bs), F32)

    heads = [(pp, hh) for pp in pairs for hh in range(2)]
    qth, gate = [], []
    for pp, hh in heads:
        qt = qt_ref[0, pp * LANES:(pp + 1) * LANES, :]
        km_hi, km_lo = _split2(kmean_ref[0, :, pp * LANES:(pp + 1) * LANES])
        qth.append(jnp.where((chan >= hh * hd) & (chan < (hh + 1) * hd), qt, 0.0))
        q_hi, q_lo = _split2(qth[-1])
        g = _dot(km_hi, q_hi) + _dot(km_hi, q_lo) + _dot(km_lo, q_hi)
        gate.append(jnp.where(blk < own, g, -jnp.inf))

    piece_id = (aux_row + 2) & 3
    piece = jnp.where(piece_id == 0, LOG2E_PIECES[0],
                      jnp.where(piece_id == 1, LOG2E_PIECES[1],
                                jnp.where(piece_id == 2, LOG2E_PIECES[2], LOG2E_PIECES[3])))
    q_rows = []
    for n, (pp, hh) in enumerate(heads):
        head = 2 * (MB_GANG * gang + pp) + hh
        slope_bits = (127 - ALIBI_STEP * (head + 1)) << 23
        slope = lax.bitcast_convert_type(jnp.full((MB_AUX_MASK, bs), slope_bits, jnp.int32), F32)
        qconst = -(slope * LOG2E) * qpos
        qconst_hi = qconst.astype(BF16).astype(F32)
        bias_rows = jnp.where(aux_row == 0, qconst_hi,
                              jnp.where(aux_row == 1, qconst - qconst_hi,
                                        jnp.where(aux_row < 10, slope * piece, 0.0)))
        q_rows.append(jnp.concatenate([qth[n] * (hd ** -0.5 * LOG2E), bias_rows], axis=0))

    def query_operand(mask_rows):
        ops = [jnp.concatenate([q_rows[n], mask_rows[n], aux_pad], axis=0).astype(BF16)
               for n in range(len(heads))]
        return [jnp.concatenate(ops[2 * pp:2 * pp + 2], axis=1) for pp in pairs]

    def keys(pp, i):
        return kaug_ref[pp, pl.ds(pl.multiple_of(i * sup, sup), sup), :]

    def values_t(i, pp, hh):
        lo = pp * LANES + hh * hd
        return jnp.concatenate([vt_ref[i * MB_SUPER + u, lo:lo + hd, :]
                                for u in range(MB_SUPER)], axis=1)

    def produce(s_ref, pp, g, q_op, mask=None):
        s2 = _dot(keys(pp, g), q_op[pp])
        if mask is not None:
            s2 = jnp.where(mask, NEG_BIG, s2)
        s_ref[pp] = s2
        return jnp.max(s2, axis=0, keepdims=True)

    n_grp = n_blk // MB_SUPER
    grp = own // MB_SUPER

    assert MB_SUPER == 1
    rel = (lax.broadcasted_iota(jnp.int32, (sup, 2 * bs), 0) - (own - grp * MB_SUPER) * bs)
    qi = lax.broadcasted_iota(jnp.int32, (sup, 2 * bs), 1) & (bs - 1)
    future = (rel > qi) & (rel < bs)
    q_own = query_operand([jnp.zeros((nbp, bs), F32)] * len(heads))
    smax_a0 = [produce(sa_ref, pp, grp, q_own, future) for pp in pairs]

    sel = [jnp.zeros((nbp, bs), F32) for _ in heads]
    for _ in range(MB_TOPK):
        mx = [jnp.max(g, axis=0, keepdims=True) for g in gate]
        first = [jnp.min(jnp.where(g == m, blk_f, float(nbp)), axis=0, keepdims=True)
                 for g, m in zip(gate, mx)]
        hit = [blk_f == f for f in first]
        sel = [jnp.where(h, 1.0, s) for h, s in zip(hit, sel)]
        gate = [jnp.where(h, -jnp.inf, g) for h, g in zip(hit, gate)]
    keep = [jnp.where(blk < own, s, jnp.where(blk == own, 1.0, 0.0)) for s in sel]
    qaug2 = query_operand([jnp.where(k > 0.5, 0.0, NEG_BIG) for k in keep])

    def group_at(t):
        g = jnp.where(t == 0, grp, jnp.where(t > grp, grp + 1, t - 1))
        return jnp.minimum(g, n_grp - 1)

    ones_rows = jnp.ones((MB_SUM_ROWS, sup), BF16)

    def softmax_step(s_ref, pp, smax, g, carry):
        hs = range(2)
        m_i = [carry[2 * hh] for hh in hs]
        m_new = [jnp.maximum(m_i[hh], smax[:, hh * bs:(hh + 1) * bs]) for hh in hs]
        alpha = [jnp.exp2(m_i[hh] - m_new[hh]) for hh in hs]
        pexp = [jnp.exp2((s_ref[pp, :, hh * bs:(hh + 1) * bs] - m_new[hh]).astype(BF16))
                for hh in hs]
        acc_new = [carry[2 * hh + 1] * alpha[hh]
                   + _dot(jnp.concatenate([values_t(g, pp, hh), ones_rows], axis=0), pexp[hh])
                   for hh in hs]
        return (m_new[0], acc_new[0], m_new[1], acc_new[1])

    def steps(u, carry, unroll, t0):
        t = t0 + unroll * u
        smax_a = list(carry[:MB_GANG])
        stats = [carry[MB_GANG + 4 * pp:MB_GANG + 4 * pp + 4] for pp in pairs]
        for v in range(0, unroll, 2):
            smax_b = [produce(sb_ref, pp, group_at(t + v + 1), qaug2) for pp in pairs]
            stats = [softmax_step(sa_ref, pp, smax_a[pp], group_at(t + v), stats[pp])
                     for pp in pairs]
            smax_a = [produce(sa_ref, pp, group_at(t + v + 2), qaug2) for pp in pairs]
            stats = [softmax_step(sb_ref, pp, smax_b[pp], group_at(t + v + 1), stats[pp])
                     for pp in pairs]
        out = tuple(smax_a)
        for pp in pairs:
            out += tuple(stats[pp])
        return out

    stat0 = jnp.full((1, bs), -jnp.inf, F32)
    acc0 = jnp.zeros((hd + MB_SUM_ROWS, bs), F32)
    n_full = (grp + 1) // MB_UNROLL
    n_tail = (grp + 2 - n_full * MB_UNROLL) // 2
    fin = lax.fori_loop(0, n_full, functools.partial(steps, unroll=MB_UNROLL, t0=0),
                        tuple(smax_a0) + (stat0, acc0) * len(heads))
    fin = lax.fori_loop(0, n_tail, functools.partial(steps, unroll=2, t0=n_full * MB_UNROLL), fin)
    accs = [fin[MB_GANG + 2 * n + 1] for n in range(len(heads))]
    out_t = jnp.concatenate([a[:hd] / a[hd:hd + 1] for a in accs], axis=0)
    o_ref[0] = out_t.T


def _moba(qt, k, kmean, vt):
    bsz, t_len, _ = k.shape
    bs = MB_BLOCK
    n_blk = t_len // bs
    assert MB_UNROLL % 2 == 0 and n_blk % (MB_UNROLL * MB_SUPER) == 0
    assert MB_AUX_MASK + n_blk <= LANES
    assert n_blk % SUBLANES == 0 and MB_PAIRS % MB_GANG == 0
    gw = MB_GANG * LANES
    return pl.pallas_call(
        functools.partial(_moba_kernel, n_blk=n_blk),
        grid=(bsz, MB_PAIRS // MB_GANG, n_blk),
        in_specs=[
            pl.BlockSpec((1, gw, bs), lambda b, p, i: (b * n_blk + i, p, 0)),
            pl.BlockSpec((1, t_len, gw), lambda b, p, i: (b, 0, p)),
            pl.BlockSpec((1, n_blk, gw), lambda b, p, i: (b, 0, p)),
            pl.BlockSpec((n_blk, gw, bs), lambda b, p, i: (b, p, 0)),
        ],
        out_specs=pl.BlockSpec((1, bs, gw), lambda b, p, i: (b, i, p)),
        out_shape=jax.ShapeDtypeStruct((bsz, t_len, MB_WIDTH), F32),
        scratch_shapes=[
            pltpu.VMEM((MB_GANG, t_len, 2 * LANES), BF16),
            pltpu.VMEM((MB_GANG, MB_SUPER * bs, 2 * bs), F32),
            pltpu.VMEM((MB_GANG, MB_SUPER * bs, 2 * bs), F32),
        ],
        compiler_params=pltpu.CompilerParams(
            dimension_semantics=("arbitrary", "arbitrary", "arbitrary"),
            vmem_limit_bytes=VMEM_LIMIT),
        name="moba",
    )(qt, k, kmean, vt)


def _mixout_kernel(x_ref, odn_ref, omb_ref, prew_ref, wz_ref, wgd_ref, wgm_ref,
                   wbd_ref, wbm_ref, wo_ref, postw_ref, o_ref):
    x = x_ref[...]
    h = _rms(x, prew_ref[...]).astype(BF16)
    z = _dot(h, wz_ref[...])
    gate_dn = jax.nn.sigmoid(_dot(h, wgd_ref[...]))
    gate_mb = jax.nn.sigmoid(_dot(h, wgm_ref[...]))
    o_dn = odn_ref[...] * (z * jax.nn.sigmoid(z))
    y_dn = _dot(o_dn.astype(BF16), wbd_ref[...])
    y_mb = _dot(omb_ref[...].astype(BF16), wbm_ref[...])
    merged = gate_dn * y_dn + gate_mb * y_mb
    y = _dot(merged.astype(BF16), wo_ref[...])
    o_ref[...] = x + _rms(y, postw_ref[...])


def _mix_out(x, o_dn, o_mb, pre_w, w_z, w_gd, w_gm, w_bd, w_bm, w_o, post_w, tm=512):
    n, d = x.shape
    return pl.pallas_call(
        _mixout_kernel,
        grid=(n // tm,),
        in_specs=[
            pl.BlockSpec((tm, d), lambda i: (i, 0)),
            pl.BlockSpec((tm, o_dn.shape[1]), lambda i: (i, 0)),
            pl.BlockSpec((tm, o_mb.shape[1]), lambda i: (i, 0)),
            _const_spec((1, d)),
            _const_spec(w_z.shape),
            _const_spec(w_gd.shape),
            _const_spec(w_gm.shape),
            _const_spec(w_bd.shape),
            _const_spec(w_bm.shape),
            _const_spec(w_o.shape),
            _const_spec((1, d)),
        ],
        out_specs=pl.BlockSpec((tm, d), lambda i: (i, 0)),
        out_shape=jax.ShapeDtypeStruct((n, d), F32),
        compiler_params=pltpu.CompilerParams(
            dimension_semantics=("arbitrary",), vmem_limit_bytes=VMEM_LIMIT),
        name="mix_out",
    )(x, o_dn, o_mb, pre_w, w_z, w_gd, w_gm, w_bd, w_bm, w_o, post_w)


def _layer(x, ffn1_pre_w, ffn1_w_gate, ffn1_w_up, ffn1_w_down, ffn1_post_w,
           mix_pre_w, w_in, dn_conv_w, dn_a_log, dn_dt_bias, dn_norm_w,
           w_branch_dn, w_branch_mb, w_out, mix_post_w,
           ffn2_pre_w, ffn2_w_gate, ffn2_w_up, ffn2_w_down, ffn2_post_w):
    bsz, t_len, d = x.shape
    n = bsz * t_len
    row = lambda w: w.reshape(1, -1).astype(F32)
    b16 = lambda w: w.astype(BF16)

    x = x.reshape(n, d)
    x = _ffn_block(x, row(ffn1_pre_w), b16(ffn1_w_gate), b16(ffn1_w_up), b16(ffn1_w_down),
                   row(ffn1_post_w))

    o = 0
    w_dn = w_in[:, o:o + 3 * DN_WIDTH]; o += 3 * DN_WIDTH
    w_z = w_in[:, o:o + DN_WIDTH]; o += DN_WIDTH
    w_ba = w_in[:, o:o + 2 * DN_HEADS]; o += 2 * DN_HEADS
    w_q = w_in[:, o:o + MB_WIDTH]; o += MB_WIDTH
    w_k = w_in[:, o:o + MB_WIDTH]; o += MB_WIDTH
    w_v = w_in[:, o:o + MB_WIDTH]; o += MB_WIDTH
    w_gd = w_in[:, o:o + d]; o += d
    w_gm = w_in[:, o:o + d]; o += d
    w_ba = jnp.pad(w_ba, ((0, 0), (0, LANES - 2 * DN_HEADS)))

    def hi_lo(w):
        hi = w.astype(BF16)
        return jnp.stack([hi, (w - hi.astype(F32)).astype(BF16)])

    dn_qkv, ba, mb_k, mb_kmean, mb_qt, mb_vt = _in_proj(
        x, row(mix_pre_w), b16(w_dn), jnp.concatenate(list(hi_lo(w_ba)), axis=1), hi_lo(w_k),
        b16(w_q), b16(w_v))

    pad_heads = lambda p: jnp.pad(p.astype(F32), (DN_HEADS, LANES - 2 * DN_HEADS)).reshape(1, LANES)
    o_dn = _deltanet(dn_qkv.reshape(bsz, t_len, -1), ba.reshape(bsz, t_len, LANES),
                     dn_conv_w.astype(F32), pad_heads(dn_a_log), pad_heads(dn_dt_bias),
                     row(dn_norm_w))

    o_mb = _moba(mb_qt, mb_k.reshape(bsz, t_len, MB_WIDTH),
                 mb_kmean.reshape(bsz, t_len // MB_BLOCK, MB_WIDTH), mb_vt)

    x = _mix_out(x, o_dn.reshape(n, DN_WIDTH), o_mb.reshape(n, MB_WIDTH), row(mix_pre_w),
                 b16(w_z), b16(w_gd), b16(w_gm), b16(w_branch_dn), b16(w_branch_mb), b16(w_out),
                 row(mix_post_w))

    x = _ffn_block(x, row(ffn2_pre_w), b16(ffn2_w_gate), b16(ffn2_w_up), b16(ffn2_w_down),
                   row(ffn2_post_w))
    return x.reshape(bsz, t_len, d)


def kernel(x, ffn1_pre_w, ffn1_w_gate, ffn1_w_up, ffn1_w_down, ffn1_post_w, mix_pre_w, w_in, dn_conv_w, dn_a_log, dn_dt_bias, dn_norm_w, w_branch_dn, w_branch_mb, w_out, mix_post_w, ffn2_pre_w, ffn2_w_gate, ffn2_w_up, ffn2_w_down, ffn2_post_w):
    depth = w_in.shape[0]
    for l in range(depth):
        x = _layer(x, ffn1_pre_w[l], ffn1_w_gate[l], ffn1_w_up[l], ffn1_w_down[l], ffn1_post_w[l],
                   mix_pre_w[l], w_in[l], dn_conv_w[l], dn_a_log[l], dn_dt_bias[l], dn_norm_w[l],
                   w_branch_dn[l], w_branch_mb[l], w_out[l], mix_post_w[l],
                   ffn2_pre_w[l], ffn2_w_gate[l], ffn2_w_up[l], ffn2_w_down[l], ffn2_post_w[l])
    return x
```

```python
import functools
import math

import jax
import jax.numpy as jnp
from jax import lax
from jax.experimental import pallas as pl
from jax.experimental.pallas import tpu as pltpu

F32 = jnp.float32
BF16 = jnp.bfloat16

NORM_EPS = 1e-6
MACARON_WEIGHT = 0.5

DN_HEADS = 4
DN_HEAD_DIM = 128
DN_WIDTH = DN_HEADS * DN_HEAD_DIM
DN_CONV = 4
DN_CHUNK = 64
DN_TILE = 256
DN_STAGE_LAG = 1 + DN_HEADS + (DN_CHUNK.bit_length() - 2)

MB_HEADS = 8
MB_HEAD_DIM = 64
MB_WIDTH = MB_HEADS * MB_HEAD_DIM
MB_BLOCK = 256
MB_TOPK = 3
ALIBI_MAX_BIAS = 8.0
LANES = 128
SUBLANES = 8
MB_PAIRS = MB_WIDTH // LANES
NEG_BIG = -1e30

VMEM_LIMIT = 56 * 1024 * 1024


def _rms(x, w):
    ms = jnp.mean(x * x, axis=-1, keepdims=True)
    return x * lax.rsqrt(ms + NORM_EPS) * w


def _dot(a, b):
    return jnp.dot(a, b, preferred_element_type=F32)


def _dot_nt(a, b):
    return lax.dot_general(a, b, (((1,), (1,)), ((), ())), preferred_element_type=F32)


def _dot_tn(a, b):
    return lax.dot_general(a, b, (((0,), (0,)), ((), ())), preferred_element_type=F32)


def _split2(x):
    hi = x.astype(BF16)
    lo = (x - hi.astype(F32)).astype(BF16)
    return hi, lo


def _const_spec(shape):
    nd = len(shape)
    return pl.BlockSpec(shape, lambda *_: (0,) * nd, pipeline_mode=pl.Buffered(1))


def _ffn_kernel(x_ref, prew_ref, wg_ref, wu_ref, wd_ref, postw_ref, o_ref):
    x = x_ref[...]
    xn = _rms(x, prew_ref[...]).astype(BF16)
    g = _dot(xn, wg_ref[...])
    u = _dot(xn, wu_ref[...])
    a = (g * jax.nn.sigmoid(g) * u).astype(BF16)
    h = _dot(a, wd_ref[...])
    o_ref[...] = x + MACARON_WEIGHT * _rms(h, postw_ref[...])


def _ffn_block(x, pre_w, w_gate, w_up, w_down, post_w, tm=512):
    n, d = x.shape
    dff = w_gate.shape[1]
    return pl.pallas_call(
        _ffn_kernel,
        grid=(n // tm,),
        in_specs=[
            pl.BlockSpec((tm, d), lambda i: (i, 0)),
            _const_spec((1, d)),
            _const_spec((d, dff)),
            _const_spec((d, dff)),
            _const_spec((dff, d)),
            _const_spec((1, d)),
        ],
        out_specs=pl.BlockSpec((tm, d), lambda i: (i, 0)),
        out_shape=jax.ShapeDtypeStruct((n, d), F32),
        compiler_params=pltpu.CompilerParams(
            dimension_semantics=("arbitrary",), vmem_limit_bytes=VMEM_LIMIT),
        name="ffn_block",
    )(x, pre_w, w_gate, w_up, w_down, post_w)


def _inproj_kernel(x_ref, prew_ref, wdn_ref, wba_ref, wk_ref, wq_ref, wv_ref,
                   dn_ref, ba_ref, k_ref, kmean_ref, qt_ref, vt_ref):
    bs = MB_BLOCK
    nb = qt_ref.shape[0]
    h = _rms(x_ref[...], prew_ref[...])
    h16 = h.astype(BF16)
    dn_ref[...] = _dot_nt(h16, wdn_ref[...])
    ba2 = _dot_nt(h16, wba_ref[...])
    ba_ref[...] = ba2[:, :LANES] + ba2[:, LANES:]
    k_ref[...] = _dot_nt(h16, wk_ref[0]).astype(BF16)
    hbar = jnp.concatenate([jnp.mean(h[i * bs:(i + 1) * bs], axis=0, keepdims=True)
                            for i in range(nb)]
                           + [jnp.zeros((SUBLANES - nb, h.shape[1]), F32)], axis=0)
    hb_hi, hb_lo = _split2(hbar)
    kmean = _dot_nt(hb_hi, wk_ref[0]) + _dot_nt(hb_lo, wk_ref[0]) + _dot_nt(hb_hi, wk_ref[1])
    kmean_ref[0] = kmean[:nb]
    qt = _dot_nt(wq_ref[...], h16)
    vt = _dot_nt(wv_ref[...], h16).astype(BF16)
    for i in range(nb):
        qt_ref[i] = qt[:, i * bs:(i + 1) * bs]
        vt_ref[i] = vt[:, i * bs:(i + 1) * bs]


def _in_proj(x, pre_w, w_dn, w_ba, w_k, w_qt, w_vt, tm=512):
    n, d = x.shape
    bs = MB_BLOCK
    return pl.pallas_call(
        _inproj_kernel,
        grid=(n // tm,),
        in_specs=[
            pl.BlockSpec((tm, d), lambda i: (i, 0)),
            _const_spec((1, d)),
            _const_spec(w_dn.shape),
            _const_spec(w_ba.shape),
            _const_spec(w_k.shape),
            _const_spec(w_qt.shape),
            _const_spec(w_vt.shape),
        ],
        out_specs=[
            pl.BlockSpec((tm, w_dn.shape[0]), lambda i: (i, 0)),
            pl.BlockSpec((tm, LANES), lambda i: (i, 0)),
            pl.BlockSpec((tm, MB_WIDTH), lambda i: (i, 0)),
            pl.BlockSpec((1, tm // bs, MB_WIDTH), lambda i: (i, 0, 0)),
            pl.BlockSpec((tm // bs, MB_WIDTH, bs), lambda i: (i, 0, 0)),
            pl.BlockSpec((tm // bs, MB_WIDTH, bs), lambda i: (i, 0, 0)),
        ],
        out_shape=[
            jax.ShapeDtypeStruct((n, w_dn.shape[0]), F32),
            jax.ShapeDtypeStruct((n, LANES), F32),
            jax.ShapeDtypeStruct((n, MB_WIDTH), BF16),
            jax.ShapeDtypeStruct((n // tm, tm // bs, MB_WIDTH), F32),
            jax.ShapeDtypeStruct((n // bs, MB_WIDTH, bs), F32),
            jax.ShapeDtypeStruct((n // bs, MB_WIDTH, bs), BF16),
        ],
        compiler_params=pltpu.CompilerParams(
            dimension_semantics=("arbitrary",), vmem_limit_bytes=VMEM_LIMIT),
        name="in_proj",
    )(x, pre_w, w_dn, w_ba, w_k, w_qt, w_vt)


def _dn_kernel(qkv_ref, ba_ref, convw_ref, alog_ref, dtb_ref, normw_ref, o_ref,
               xbuf_ref, state_ref):
    tt = DN_TILE
    c = DN_CHUNK
    dk = DN_HEAD_DIM
    n_batch = qkv_ref.shape[0]
    heads = range(DN_HEADS)

    @pl.when(pl.program_id(0) == 0)
    def _():
        xbuf_ref[:, 0:8, :] = jnp.zeros((n_batch, 8, 3 * DN_WIDTH), F32)
        state_ref[...] = jnp.zeros_like(state_ref)

    ri = lax.broadcasted_iota(jnp.int32, (tt, tt), 0)
    ci = lax.broadcasted_iota(jnp.int32, (tt, tt), 1)
    same_chunk = (ri // c) == (ci // c)
    incl = same_chunk & (ri >= ci)
    eye = ri == ci
    tril = jnp.where(incl, 1.0, 0.0).astype(BF16)
    ones_bd = jnp.where(same_chunk, 1.0, 0.0).astype(BF16)

    def batch_program(b):
        x = qkv_ref[b]
        xbuf_ref[b, 8:8 + tt, :] = x
        cw = convw_ref[...]
        y = x * cw[DN_CONV - 1:DN_CONV, :]
        for s in range(1, DN_CONV):
            y = y + xbuf_ref[b, 8 - s:8 - s + tt, :] * cw[DN_CONV - 1 - s:DN_CONV - s, :]
        xbuf_ref[b, 0:8, :] = x[tt - 8:tt, :]
        y = y * jax.nn.sigmoid(y)

        ba = ba_ref[b]
        beta_all = jax.nn.sigmoid(ba)
        g_all = -jnp.exp(alog_ref[...]) * jax.nn.softplus(ba + dtb_ref[...])

        g1 = g_all.astype(BF16)
        r1 = g_all - g1.astype(F32)
        g2 = r1.astype(BF16)
        g3 = (r1 - g2.astype(F32)).astype(BF16)
        gcs_all = _dot(tril, g1) + _dot(tril, g2) + _dot(tril, g3)
        gtot_all = _dot(ones_bd, g1) + _dot(ones_bd, g2) + _dot(ones_bd, g3)
        yield

        lmat, attn16, rhs, qd, kd, gtot = [], [], [], [], [], []
        for h in heads:
            qr = y[:, h * dk:(h + 1) * dk]
            kr = y[:, DN_WIDTH + h * dk:DN_WIDTH + (h + 1) * dk]
            v = y[:, 2 * DN_WIDTH + h * dk:2 * DN_WIDTH + (h + 1) * dk]
            q = qr * lax.rsqrt(jnp.sum(qr * qr, axis=-1, keepdims=True) + NORM_EPS) * (dk ** -0.5)
            k = kr * lax.rsqrt(jnp.sum(kr * kr, axis=-1, keepdims=True) + NORM_EPS)
            beta = beta_all[:, h:h + 1]
            gcs = gcs_all[:, DN_HEADS + h:DN_HEADS + h + 1]
            gtot.append(gtot_all[:, DN_HEADS + h:DN_HEADS + h + 1])
            eg = jnp.exp(gcs)

            g_row = jnp.sum(jnp.where(eye, gcs, 0.0), axis=0, keepdims=True)
            decay = jnp.exp(jnp.where(incl, gcs - g_row, NEG_BIG))

            kb = k * beta
            k16 = k.astype(BF16)
            lmat.append(jnp.where(eye, 0.0, _dot_nt(kb.astype(BF16), k16) * decay))
            attn16.append((_dot_nt(q.astype(BF16), k16) * decay).astype(BF16))
            rhs.append(jnp.concatenate([v * beta, kb * eg], axis=1).astype(BF16))
            qd.append(q * eg)
            kd.append((k * jnp.exp(gtot[h] - gcs)).astype(BF16))
            yield

        def compact(a):
            return sum(a[ch * c:(ch + 1) * c] for ch in range(1, tt // c)) + a[:c]

        def expand(a):
            return jnp.where(same_chunk, jnp.concatenate([a] * (tt // c), axis=0), 0.0)

        xc = [compact(jnp.where(eye, 1.0, -lmat[h])) for h in heads]
        mc = [_dot(compact(lmat[h]).astype(BF16), lmat[h].astype(BF16)) for h in heads]
        power = 2
        while power < c:
            m_bd = [expand(mc[h]).astype(BF16) for h in heads]
            if 2 * power < c:
                xm = [_dot(jnp.concatenate([xc[h], mc[h]], axis=0).astype(BF16), m_bd[h])
                      for h in heads]
                xc = [xc[h] + xm[h][:c] for h in heads]
                mc = [xm[h][c:] for h in heads]
            else:
                xc = [xc[h] + _dot(xc[h].astype(BF16), m_bd[h]) for h in heads]
            power *= 2
            yield

        uw16 = [_dot(expand(xc[h]).astype(BF16), rhs[h]).astype(BF16) for h in heads]
        au_aw = [_dot(attn16[h], uw16[h]) for h in heads]
        au = [au_aw[h][:, :dk] for h in heads]
        e16 = [(qd[h] - au_aw[h][:, dk:]).astype(BF16) for h in heads]
        yield

        s = [state_ref[b * DN_HEADS + h] for h in heads]
        outs = [[] for _ in heads]
        for ch in range(tt // c):
            lo, hi = ch * c, (ch + 1) * c
            bc = [_dot_tn(kd[h][lo:hi], uw16[h][lo:hi]) for h in heads]
            for h in heads:
                s16 = s[h].astype(BF16)
                outs[h].append(_dot(e16[h][lo:hi], s16) + au[h][lo:hi])
                s[h] = (s[h] * jnp.exp(gtot[h][lo:lo + 1, :]) + bc[h][:, :dk]
                        - _dot(bc[h][:, dk:].astype(BF16), s16))
            yield
        for h in heads:
            state_ref[b * DN_HEADS + h] = s[h]
            o = jnp.concatenate(outs[h], axis=0)
            o_ref[b, :, h * dk:(h + 1) * dk] = _rms(o, normw_ref[...])
        yield

    programs = [batch_program(b) for b in range(n_batch)]
    live = [True] * n_batch
    wave = 0
    while any(live):
        for b in range(n_batch):
            if live[b] and wave >= DN_STAGE_LAG * b:
                live[b] = next(programs[b], "done") != "done"
        wave += 1


def _deltanet(qkv, ba, conv_w, alog_row, dtb_row, norm_w):
    bsz, t_len, width = qkv.shape
    tt = DN_TILE
    return pl.pallas_call(
        _dn_kernel,
        grid=(t_len // tt,),
        in_specs=[
            pl.BlockSpec((bsz, tt, width), lambda t: (0, t, 0)),
            pl.BlockSpec((bsz, tt, LANES), lambda t: (0, t, 0)),
            _const_spec(conv_w.shape),
            _const_spec((1, LANES)),
            _const_spec((1, LANES)),
            _const_spec((1, DN_HEAD_DIM)),
        ],
        out_specs=pl.BlockSpec((bsz, tt, DN_WIDTH), lambda t: (0, t, 0)),
        out_shape=jax.ShapeDtypeStruct((bsz, t_len, DN_WIDTH), F32),
        scratch_shapes=[
            pltpu.VMEM((bsz, 8 + tt, width), F32),
            pltpu.VMEM((bsz * DN_HEADS, DN_HEAD_DIM, DN_HEAD_DIM), F32),
        ],
        compiler_params=pltpu.CompilerParams(
            dimension_semantics=("arbitrary",), vmem_limit_bytes=VMEM_LIMIT),
        name="deltanet",
    )(qkv, ba, conv_w, alog_row, dtb_row, norm_w)


MB_SUPER = 1
MB_GANG = 2
MB_UNROLL = 4
MB_AUX_MASK = 16
MB_SUM_ROWS = 16
ALIBI_STEP = int(ALIBI_MAX_BIAS) // MB_HEADS
assert ALIBI_STEP * MB_HEADS == ALIBI_MAX_BIAS
LOG2E = math.log2(math.e)
LOG2E_PIECES = (1.4453125, -0.00262451171875, 7.063150405883789e-06, -1.05355866253376e-08)


def _moba_kernel(qt_ref, k_ref, kmean_ref, vt_ref, o_ref, kaug_ref, sa_ref, sb_ref, *, n_blk):
    bs = MB_BLOCK
    hd = MB_HEAD_DIM
    sup = MB_SUPER * bs
    nbp = -(-n_blk // SUBLANES) * SUBLANES
    gang = pl.program_id(1)
    own = pl.program_id(2)
    pairs = range(MB_GANG)
    lane = lax.broadcasted_iota(jnp.int32, (bs, LANES), 1)
    row = lax.broadcasted_iota(jnp.int32, (bs, LANES), 0)

    @pl.when(own == 0)
    def _():
        def build(j, carry):
            off = pl.multiple_of(j * bs, bs)
            kstart = jnp.full((bs, LANES), j * bs, jnp.int32).astype(F32)
            aux = jnp.where(lane < 2, 1.0,
                            jnp.where(lane < 6, row.astype(F32),
                                      jnp.where(lane < 10, kstart,
                                                jnp.where(lane == MB_AUX_MASK + j, 1.0, 0.0))))
            for pp in pairs:
                kaug_ref[pp, pl.ds(off, bs), 0:LANES] = k_ref[0, pl.ds(off, bs),
                                                              pp * LANES:(pp + 1) * LANES]
                kaug_ref[pp, pl.ds(off, bs), LANES:2 * LANES] = aux.astype(BF16)
            return carry

        lax.fori_loop(0, n_blk, build, 0)

    chan = lax.broadcasted_iota(jnp.int32, (LANES, bs), 0)
    blk = lax.broadcasted_iota(jnp.int32, (nbp, bs), 0)
    blk_f = blk.astype(F32)
    aux_row = lax.broadcasted_iota(jnp.int32, (MB_AUX_MASK, bs), 0)
    qpos = (lax.broadcasted_iota(jnp.int32, (MB_AUX_MASK, bs), 1) + own * bs).astype(F32)
    aux_pad = jnp.zeros((LANES - MB_AUX_MASK - nbp, bs), F32)

    heads = [(pp, hh) for pp in pairs for hh in range(2)]
    qth, gate = [], []
    for pp, hh in heads:
        qt = qt_ref[0, pp * LANES:(pp + 1) * LANES, :]
        km_hi, km_lo = _split2(kmean_ref[0, :, pp * LANES:(pp + 1) * LANES])
        qth.append(jnp.where((chan >= hh * hd) & (chan < (hh + 1) * hd), qt, 0.0))
        q_hi, q_lo = _split2(qth[-1])
        g = _dot(km_hi, q_hi) + _dot(km_hi, q_lo) + _dot(km_lo, q_hi)
        gate.append(jnp.where(blk < own, g, -jnp.inf))

    piece_id = (aux_row + 2) & 3
    piece = jnp.where(piece_id == 0, LOG2E_PIECES[0],
                      jnp.where(piece_id == 1, LOG2E_PIECES[1],
                                jnp.where(piece_id == 2, LOG2E_PIECES[2], LOG2E_PIECES[3])))
    q_rows = []
    for n, (pp, hh) in enumerate(heads):
        head = 2 * (MB_GANG * gang + pp) + hh
        slope_bits = (127 - ALIBI_STEP * (head + 1)) << 23
        slope = lax.bitcast_convert_type(jnp.full((MB_AUX_MASK, bs), slope_bits, jnp.int32), F32)
        qconst = -(slope * LOG2E) * qpos
        qconst_hi = qconst.astype(BF16).astype(F32)
        bias_rows = jnp.where(aux_row == 0, qconst_hi,
                              jnp.where(aux_row == 1, qconst - qconst_hi,
                                        jnp.where(aux_row < 10, slope * piece, 0.0)))
        q_rows.append(jnp.concatenate([qth[n] * (hd ** -0.5 * LOG2E), bias_rows], axis=0))

    def query_operand(mask_rows):
        ops = [jnp.concatenate([q_rows[n], mask_rows[n], aux_pad], axis=0).astype(BF16)
               for n in range(len(heads))]
        return [jnp.concatenate(ops[2 * pp:2 * pp + 2], axis=1) for pp in pairs]

    def keys(pp, i):
        return kaug_ref[pp, pl.ds(pl.multiple_of(i * sup, sup), sup), :]

    def values_t(i, pp, hh):
        lo = pp * LANES + hh * hd
        return jnp.concatenate([vt_ref[i * MB_SUPER + u, lo:lo + hd, :]
                                for u in range(MB_SUPER)], axis=1)

    def produce(s_ref, pp, g, q_op, mask=None):
        s2 = _dot(keys(pp, g), q_op[pp])
        if mask is not None:
            s2 = jnp.where(mask, NEG_BIG, s2)
        s_ref[pp] = s2
        return jnp.max(s2, axis=0, keepdims=True)

    n_grp = n_blk // MB_SUPER
    grp = own // MB_SUPER

    assert MB_SUPER == 1
    rel = (lax.broadcasted_iota(jnp.int32, (sup, 2 * bs), 0) - (own - grp * MB_SUPER) * bs)
    qi = lax.broadcasted_iota(jnp.int32, (sup, 2 * bs), 1) & (bs - 1)
    future = (rel > qi) & (rel < bs)
    q_own = query_operand([jnp.zeros((nbp, bs), F32)] * len(heads))
    smax_a0 = [produce(sa_ref, pp, grp, q_own, future) for pp in pairs]

    sel = [jnp.zeros((nbp, bs), F32) for _ in heads]
    for _ in range(MB_TOPK):
        mx = [jnp.max(g, axis=0, keepdims=True) for g in gate]
        first = [jnp.min(jnp.where(g == m, blk_f, float(nbp)), axis=0, keepdims=True)
                 for g, m in zip(gate, mx)]
        hit = [blk_f == f for f in first]
        sel = [jnp.where(h, 1.0, s) for h, s in zip(hit, sel)]
        gate = [jnp.where(h, -jnp.inf, g) for h, g in zip(hit, gate)]
    keep = [jnp.where(blk < own, s, jnp.where(blk == own, 1.0, 0.0)) for s in sel]
    qaug2 = query_operand([jnp.where(k > 0.5, 0.0, NEG_BIG) for k in keep])

    def group_at(t):
        g = jnp.where(t == 0, grp, jnp.where(t > grp, grp + 1, t - 1))
        return jnp.minimum(g, n_grp - 1)

    ones_rows = jnp.ones((MB_SUM_ROWS, sup), BF16)

    def softmax_step(s_ref, pp, smax, g, carry):
        hs = range(2)
        m_i = [carry[2 * hh] for hh in hs]
        m_new = [jnp.maximum(m_i[hh], smax[:, hh * bs:(hh + 1) * bs]) for hh in hs]
        alpha = [jnp.exp2(m_i[hh] - m_new[hh]) for hh in hs]
        pexp = [jnp.exp2((s_ref[pp, :, hh * bs:(hh + 1) * bs] - m_new[hh]).astype(BF16))
                for hh in hs]
        acc_new = [carry[2 * hh + 1] * alpha[hh]
                   + _dot(jnp.concatenate([values_t(g, pp, hh), ones_rows], axis=0), pexp[hh])
                   for hh in hs]
        return (m_new[0], acc_new[0], m_new[1], acc_new[1])

    def steps(u, carry, unroll, t0):
        t = t0 + unroll * u
        smax_a = list(carry[:MB_GANG])
        stats = [carry[MB_GANG + 4 * pp:MB_GANG + 4 * pp + 4] for pp in pairs]
        for v in range(0, unroll, 2):
            smax_b = [produce(sb_ref, pp, group_at(t + v + 1), qaug2) for pp in pairs]
            stats = [softmax_step(sa_ref, pp, smax_a[pp], group_at(t + v), stats[pp])
                     for pp in pairs]
            smax_a = [produce(sa_ref, pp, group_at(t + v + 2), qaug2) for pp in pairs]
            stats = [softmax_step(sb_ref, pp, smax_b[pp], group_at(t + v + 1), stats[pp])
                     for pp in pairs]
        out = tuple(smax_a)
        for pp in pairs:
            out += tuple(stats[pp])
        return out

    stat0 = jnp.full((1, bs), -jnp.inf, F32)
    acc0 = jnp.zeros((hd + MB_SUM_ROWS, bs), F32)
    n_full = (grp + 1) // MB_UNROLL
    n_tail = (grp + 2 - n_full * MB_UNROLL) // 2
    fin = lax.fori_loop(0, n_full, functools.partial(steps, unroll=MB_UNROLL, t0=0),
                        tuple(smax_a0) + (stat0, acc0) * len(heads))
    fin = lax.fori_loop(0, n_tail, functools.partial(steps, unroll=2, t0=n_full * MB_UNROLL), fin)
    accs = [fin[MB_GANG + 2 * n + 1] for n in range(len(heads))]
    out_t = jnp.concatenate([a[:hd] / a[hd:hd + 1] for a in accs], axis=0)
    o_ref[0] = out_t.T


def _moba(qt, k, kmean, vt):
    bsz, t_len, _ = k.shape
    bs = MB_BLOCK
    n_blk = t_len // bs
    assert MB_UNROLL % 2 == 0 and n_blk % (MB_UNROLL * MB_SUPER) == 0
    assert MB_AUX_MASK + n_blk <= LANES
    assert n_blk % SUBLANES == 0 and MB_PAIRS % MB_GANG == 0
    gw = MB_GANG * LANES
    return pl.pallas_call(
        functools.partial(_moba_kernel, n_blk=n_blk),
        grid=(bsz, MB_PAIRS // MB_GANG, n_blk),
        in_specs=[
            pl.BlockSpec((1, gw, bs), lambda b, p, i: (b * n_blk + i, p, 0)),
            pl.BlockSpec((1, t_len, gw), lambda b, p, i: (b, 0, p)),
            pl.BlockSpec((1, n_blk, gw), lambda b, p, i: (b, 0, p)),
            pl.BlockSpec((n_blk, gw, bs), lambda b, p, i: (b, p, 0)),
        ],
        out_specs=pl.BlockSpec((1, bs, gw), lambda b, p, i: (b, i, p)),
        out_shape=jax.ShapeDtypeStruct((bsz, t_len, MB_WIDTH), F32),
        scratch_shapes=[
            pltpu.VMEM((MB_GANG, t_len, 2 * LANES), BF16),
            pltpu.VMEM((MB_GANG, MB_SUPER * bs, 2 * bs), F32),
            pltpu.VMEM((MB_GANG, MB_SUPER * bs, 2 * bs), F32),
        ],
        compiler_params=pltpu.CompilerParams(
            dimension_semantics=("arbitrary", "arbitrary", "arbitrary"),
            vmem_limit_bytes=VMEM_LIMIT),
        name="moba",
    )(qt, k, kmean, vt)


def _mixout_kernel(x_ref, odn_ref, omb_ref, prew_ref, wz_ref, wgd_ref, wgm_ref,
                   wbd_ref, wbm_ref, wo_ref, postw_ref, o_ref):
    x = x_ref[...]
    h = _rms(x, prew_ref[...]).astype(BF16)
    z = _dot_nt(h, wz_ref[...])
    gate_dn = jax.nn.sigmoid(_dot_nt(h, wgd_ref[...]))
    gate_mb = jax.nn.sigmoid(_dot_nt(h, wgm_ref[...]))
    o_dn = odn_ref[...] * (z * jax.nn.sigmoid(z))
    y_dn = _dot(o_dn.astype(BF16), wbd_ref[...])
    y_mb = _dot(omb_ref[...].astype(BF16), wbm_ref[...])
    merged = gate_dn * y_dn + gate_mb * y_mb
    y = _dot(merged.astype(BF16), wo_ref[...])
    o_ref[...] = x + _rms(y, postw_ref[...])


def _mix_out(x, o_dn, o_mb, pre_w, w_z, w_gd, w_gm, w_bd, w_bm, w_o, post_w, tm=512):
    n, d = x.shape
    return pl.pallas_call(
        _mixout_kernel,
        grid=(n // tm,),
        in_specs=[
            pl.BlockSpec((tm, d), lambda i: (i, 0)),
            pl.BlockSpec((tm, o_dn.shape[1]), lambda i: (i, 0)),
            pl.BlockSpec((tm, o_mb.shape[1]), lambda i: (i, 0)),
            _const_spec((1, d)),
            _const_spec(w_z.shape),
            _const_spec(w_gd.shape),
            _const_spec(w_gm.shape),
            _const_spec(w_bd.shape),
            _const_spec(w_bm.shape),
            _const_spec(w_o.shape),
            _const_spec((1, d)),
        ],
        out_specs=pl.BlockSpec((tm, d), lambda i: (i, 0)),
        out_shape=jax.ShapeDtypeStruct((n, d), F32),
        compiler_params=pltpu.CompilerParams(
            dimension_semantics=("arbitrary",), vmem_limit_bytes=VMEM_LIMIT),
        name="mix_out",
    )(x, o_dn, o_mb, pre_w, w_z, w_gd, w_gm, w_bd, w_bm, w_o, post_w)


def _layer(x, ffn1_pre_w, ffn1_w_gate, ffn1_w_up, ffn1_w_down, ffn1_post_w,
           mix_pre_w, w_in, dn_conv_w, dn_a_log, dn_dt_bias, dn_norm_w,
           w_branch_dn, w_branch_mb, w_out, mix_post_w,
           ffn2_pre_w, ffn2_w_gate, ffn2_w_up, ffn2_w_down, ffn2_post_w):
    bsz, t_len, d = x.shape
    n = bsz * t_len
    row = lambda w: w.reshape(1, -1).astype(F32)
    b16 = lambda w: w.astype(BF16)

    x = x.reshape(n, d)
    x = _ffn_block(x, row(ffn1_pre_w), b16(ffn1_w_gate), b16(ffn1_w_up), b16(ffn1_w_down),
                   row(ffn1_post_w))

    w_in_t = w_in.T
    o = 0
    w_dn = w_in_t[o:o + 3 * DN_WIDTH]; o += 3 * DN_WIDTH
    w_z = w_in_t[o:o + DN_WIDTH]; o += DN_WIDTH
    w_ba = w_in_t[o:o + 2 * DN_HEADS]; o += 2 * DN_HEADS
    w_q = w_in_t[o:o + MB_WIDTH]; o += MB_WIDTH
    w_k = w_in_t[o:o + MB_WIDTH]; o += MB_WIDTH
    w_v = w_in_t[o:o + MB_WIDTH]; o += MB_WIDTH
    w_gd = w_in_t[o:o + d]; o += d
    w_gm = w_in_t[o:o + d]; o += d
    w_ba = jnp.pad(w_ba, ((0, LANES - 2 * DN_HEADS), (0, 0)))

    def hi_lo(w):
        hi = w.astype(BF16)
        return jnp.stack([hi, (w - hi.astype(F32)).astype(BF16)])

    dn_qkv, ba, mb_k, mb_kmean, mb_qt, mb_vt = _in_proj(
        x, row(mix_pre_w), b16(w_dn), jnp.concatenate(list(hi_lo(w_ba)), axis=0), hi_lo(w_k),
        b16(w_q), b16(w_v))

    pad_heads = lambda p: jnp.pad(p.astype(F32), (DN_HEADS, LANES - 2 * DN_HEADS)).reshape(1, LANES)
    o_dn = _deltanet(dn_qkv.reshape(bsz, t_len, -1), ba.reshape(bsz, t_len, LANES),
                     dn_conv_w.astype(F32), pad_heads(dn_a_log), pad_heads(dn_dt_bias),
                     row(dn_norm_w))

    o_mb = _moba(mb_qt, mb_k.reshape(bsz, t_len, MB_WIDTH),
                 mb_kmean.reshape(bsz, t_len // MB_BLOCK, MB_WIDTH), mb_vt)

    x = _mix_out(x, o_dn.reshape(n, DN_WIDTH), o_mb.reshape(n, MB_WIDTH), row(mix_pre_w),
                 b16(w_z), b16(w_gd), b16(w_gm), b16(w_branch_dn), b16(w_branch_mb), b16(w_out),
                 row(mix_post_w))

    x = _ffn_block(x, row(ffn2_pre_w), b16(ffn2_w_gate), b16(ffn2_w_up), b16(ffn2_w_down),
                   row(ffn2_post_w))
    return x.reshape(bsz, t_len, d)


def kernel(x, ffn1_pre_w, ffn1_w_gate, ffn1_w_up, ffn1_w_down, ffn1_post_w, mix_pre_w, w_in, dn_conv_w, dn_a_log, dn_dt_bias, dn_norm_w, w_branch_dn, w_branch_mb, w_out, mix_post_w, ffn2_pre_w, ffn2_w_gate, ffn2_w_up, ffn2_w_down, ffn2_post_w):
    depth = w_in.shape[0]
    for l in range(depth):
        x = _layer(x, ffn1_pre_w[l], ffn1_w_gate[l], ffn1_w_up[l], ffn1_w_down[l], ffn1_post_w[l],
                   mix_pre_w[l], w_in[l], dn_conv_w[l], dn_a_log[l], dn_dt_bias[l], dn_norm_w[l],
                   w_branch_dn[l], w_branch_mb[l], w_out[l], mix_post_w[l],
                   ffn2_pre_w[l], ffn2_w_gate[l], ffn2_w_up[l], ffn2_w_down[l], ffn2_post_w[l])
    return x
```

```python
import functools
import math

import jax
import jax.numpy as jnp
from jax import lax
from jax.experimental import pallas as pl
from jax.experimental.pallas import tpu as pltpu

F32 = jnp.float32
BF16 = jnp.bfloat16

NORM_EPS = 1e-6
MACARON_WEIGHT = 0.5

DN_HEADS = 4
DN_HEAD_DIM = 128
DN_WIDTH = DN_HEADS * DN_HEAD_DIM
DN_CONV = 4
DN_CHUNK = 64
DN_TILE = 256
DN_STAGE_LAG = 1 + DN_HEADS + (DN_CHUNK.bit_length() - 2)

MB_HEADS = 8
MB_HEAD_DIM = 64
MB_WIDTH = MB_HEADS * MB_HEAD_DIM
MB_BLOCK = 256
MB_TOPK = 3
ALIBI_MAX_BIAS = 8.0
LANES = 128
SUBLANES = 8
MB_PAIRS = MB_WIDTH // LANES
NEG_BIG = -1e30

VMEM_LIMIT = 56 * 1024 * 1024


def _rms(x, w):
    ms = jnp.mean(x * x, axis=-1, keepdims=True)
    return x * lax.rsqrt(ms + NORM_EPS) * w


def _dot(a, b):
    return jnp.dot(a, b, preferred_element_type=F32)


def _dot_nt(a, b):
    return lax.dot_general(a, b, (((1,), (1,)), ((), ())), preferred_element_type=F32)


def _dot_tn(a, b):
    return lax.dot_general(a, b, (((0,), (0,)), ((), ())), preferred_element_type=F32)


def _split2(x):
    hi = x.astype(BF16)
    lo = (x - hi.astype(F32)).astype(BF16)
    return hi, lo


def _const_spec(shape):
    nd = len(shape)
    return pl.BlockSpec(shape, lambda *_: (0,) * nd, pipeline_mode=pl.Buffered(1))


def _ffn_kernel(x_ref, prew_ref, wg_ref, wu_ref, wd_ref, postw_ref, o_ref):
    x = x_ref[...]
    xn = _rms(x, prew_ref[...]).astype(BF16)
    g = _dot(xn, wg_ref[...])
    u = _dot(xn, wu_ref[...])
    a = (g * jax.nn.sigmoid(g) * u).astype(BF16)
    h = _dot(a, wd_ref[...])
    o_ref[...] = x + MACARON_WEIGHT * _rms(h, postw_ref[...])


def _ffn_block(x, pre_w, w_gate, w_up, w_down, post_w, tm=512):
    n, d = x.shape
    dff = w_gate.shape[1]
    return pl.pallas_call(
        _ffn_kernel,
        grid=(n // tm,),
        in_specs=[
            pl.BlockSpec((tm, d), lambda i: (i, 0)),
            _const_spec((1, d)),
            _const_spec((d, dff)),
            _const_spec((d, dff)),
            _const_spec((dff, d)),
            _const_spec((1, d)),
        ],
        out_specs=pl.BlockSpec((tm, d), lambda i: (i, 0)),
        out_shape=jax.ShapeDtypeStruct((n, d), F32),
        compiler_params=pltpu.CompilerParams(
            dimension_semantics=("arbitrary",), vmem_limit_bytes=VMEM_LIMIT),
        name="ffn_block",
    )(x, pre_w, w_gate, w_up, w_down, post_w)


def _inproj_kernel(x_ref, prew_ref, wdn_ref, wba_ref, wk_ref, wq_ref, wv_ref,
                   dn_ref, ba_ref, k_ref, kmean_ref, qt_ref, vt_ref):
    bs = MB_BLOCK
    nb = qt_ref.shape[0]
    h = _rms(x_ref[...], prew_ref[...])
    h16 = h.astype(BF16)
    dn_ref[...] = _dot_nt(h16, wdn_ref[...])
    ba2 = _dot_nt(h16, wba_ref[...])
    ba_ref[...] = ba2[:, :LANES] + ba2[:, LANES:]
    k_ref[...] = _dot_nt(h16, wk_ref[0]).astype(BF16)
    hbar = jnp.concatenate([jnp.mean(h[i * bs:(i + 1) * bs], axis=0, keepdims=True)
                            for i in range(nb)]
                           + [jnp.zeros((SUBLANES - nb, h.shape[1]), F32)], axis=0)
    hb_hi, hb_lo = _split2(hbar)
    kmean = _dot_nt(hb_hi, wk_ref[0]) + _dot_nt(hb_lo, wk_ref[0]) + _dot_nt(hb_hi, wk_ref[1])
    kmean_ref[0] = kmean[:nb]
    qt = _dot_nt(wq_ref[...], h16)
    vt = _dot_nt(wv_ref[...], h16).astype(BF16)
    for i in range(nb):
        qt_ref[i] = qt[:, i * bs:(i + 1) * bs]
        vt_ref[i] = vt[:, i * bs:(i + 1) * bs]


def _in_proj(x, pre_w, w_dn, w_ba, w_k, w_qt, w_vt, tm=512):
    n, d = x.shape
    bs = MB_BLOCK
    return pl.pallas_call(
        _inproj_kernel,
        grid=(n // tm,),
        in_specs=[
            pl.BlockSpec((tm, d), lambda i: (i, 0)),
            _const_spec((1, d)),
            _const_spec(w_dn.shape),
            _const_spec(w_ba.shape),
            _const_spec(w_k.shape),
            _const_spec(w_qt.shape),
            _const_spec(w_vt.shape),
        ],
        out_specs=[
            pl.BlockSpec((tm, w_dn.shape[0]), lambda i: (i, 0)),
            pl.BlockSpec((tm, LANES), lambda i: (i, 0)),
            pl.BlockSpec((tm, MB_WIDTH), lambda i: (i, 0)),
            pl.BlockSpec((1, tm // bs, MB_WIDTH), lambda i: (i, 0, 0)),
            pl.BlockSpec((tm // bs, MB_WIDTH, bs), lambda i: (i, 0, 0)),
            pl.BlockSpec((tm // bs, MB_WIDTH, bs), lambda i: (i, 0, 0)),
        ],
        out_shape=[
            jax.ShapeDtypeStruct((n, w_dn.shape[0]), F32),
            jax.ShapeDtypeStruct((n, LANES), F32),
            jax.ShapeDtypeStruct((n, MB_WIDTH), BF16),
            jax.ShapeDtypeStruct((n // tm, tm // bs, MB_WIDTH), F32),
            jax.ShapeDtypeStruct((n // bs, MB_WIDTH, bs), F32),
            jax.ShapeDtypeStruct((n // bs, MB_WIDTH, bs), BF16),
        ],
        compiler_params=pltpu.CompilerParams(
            dimension_semantics=("arbitrary",), vmem_limit_bytes=VMEM_LIMIT),
        name="in_proj",
    )(x, pre_w, w_dn, w_ba, w_k, w_qt, w_vt)


def _dn_kernel(qkv_ref, ba_ref, convw_ref, alog_ref, dtb_ref, normw_ref, o_ref,
               xbuf_ref, state_ref):
    tt = DN_TILE
    c = DN_CHUNK
    dk = DN_HEAD_DIM
    n_batch = qkv_ref.shape[0]
    heads = range(DN_HEADS)
    pad = SUBLANES
    assert DN_CONV - 1 <= pad

    @pl.when(pl.program_id(0) == 0)
    def _():
        xbuf_ref[:, 0:pad, :] = jnp.zeros((n_batch, pad, 3 * DN_WIDTH), F32)
        state_ref[...] = jnp.zeros_like(state_ref)

    ri = lax.broadcasted_iota(jnp.int32, (tt, tt), 0)
    ci = lax.broadcasted_iota(jnp.int32, (tt, tt), 1)
    same_chunk = (ri // c) == (ci // c)
    tril = jnp.where(same_chunk & (ri >= ci), 1.0, 0.0).astype(BF16)
    ones_bd = jnp.where(same_chunk, 1.0, 0.0).astype(BF16)
    cc_row = lax.broadcasted_iota(jnp.int32, (c, tt), 0)
    cc_lane = lax.broadcasted_iota(jnp.int32, (c, tt), 1)
    cc_chunk = cc_lane // c
    cc_diag = cc_row == cc_lane % c
    cc_incl = cc_row >= cc_lane % c

    def batch_program(b):
        x = qkv_ref[b]
        xbuf_ref[b, pad:pad + tt, :] = x
        cw = convw_ref[...]
        y = x * cw[DN_CONV - 1:DN_CONV, :]
        for s in range(1, DN_CONV):
            y = y + xbuf_ref[b, pad - s:pad - s + tt, :] * cw[DN_CONV - 1 - s:DN_CONV - s, :]
        xbuf_ref[b, 0:pad, :] = x[tt - pad:tt, :]
        y = y * jax.nn.sigmoid(y)

        ba = ba_ref[b]
        beta_all = jax.nn.sigmoid(ba)
        g_all = -jnp.exp(alog_ref[...]) * jax.nn.softplus(ba + dtb_ref[...])

        g1 = g_all.astype(BF16)
        r1 = g_all - g1.astype(F32)
        g2 = r1.astype(BF16)
        g3 = (r1 - g2.astype(F32)).astype(BF16)
        gcs_all = _dot(tril, g1) + _dot(tril, g2) + _dot(tril, g3)
        gtot_all = _dot(ones_bd, g1) + _dot(ones_bd, g2) + _dot(ones_bd, g3)
        yield

        def compact(a):
            out = jnp.broadcast_to(a[:c], (c, tt))
            for ch in range(1, tt // c):
                out = jnp.where(cc_chunk == ch,
                                jnp.broadcast_to(a[ch * c:(ch + 1) * c], (c, tt)), out)
            return out

        def expand(a):
            return jnp.where(same_chunk, jnp.concatenate([a] * (tt // c), axis=0), 0.0)

        lmat_c, lmat16, attn16, rhs, qd, kd, gtot = [], [], [], [], [], [], []
        for h in heads:
            qr = y[:, h * dk:(h + 1) * dk]
            kr = y[:, DN_WIDTH + h * dk:DN_WIDTH + (h + 1) * dk]
            v = y[:, 2 * DN_WIDTH + h * dk:2 * DN_WIDTH + (h + 1) * dk]
            q = qr * lax.rsqrt(jnp.sum(qr * qr, axis=-1, keepdims=True) + NORM_EPS) * (dk ** -0.5)
            k = kr * lax.rsqrt(jnp.sum(kr * kr, axis=-1, keepdims=True) + NORM_EPS)
            beta = beta_all[:, h:h + 1]
            gcs = gcs_all[:, DN_HEADS + h:DN_HEADS + h + 1]
            gtot.append(gtot_all[:, DN_HEADS + h:DN_HEADS + h + 1])
            eg = jnp.exp(gcs)

            g_i = compact(gcs)
            g_j = jnp.sum(jnp.where(cc_diag, g_i, 0.0), axis=0, keepdims=True)
            decay = jnp.exp(jnp.where(cc_incl, g_i - g_j, NEG_BIG))

            kb = k * beta
            k16 = k.astype(BF16)
            lmat_c.append(jnp.where(cc_diag, 0.0, compact(_dot_nt(kb.astype(BF16), k16)) * decay))
            lmat16.append(expand(lmat_c[h]).astype(BF16))
            attn16.append(expand(compact(_dot_nt(q.astype(BF16), k16)) * decay).astype(BF16))
            rhs.append(jnp.concatenate([v * beta, kb * eg], axis=1).astype(BF16))
            qd.append(q * eg)
            kd.append((k * jnp.exp(gtot[h] - gcs)).astype(BF16))
            yield

        xc = [jnp.where(cc_diag, 1.0, -lmat_c[h]) for h in heads]
        mc = [_dot(lmat_c[h].astype(BF16), lmat16[h]) for h in heads]
        power = 2
        while power < c:
            m_bd = [expand(mc[h]).astype(BF16) for h in heads]
            if 2 * power < c:
                xm = [_dot(jnp.concatenate([xc[h], mc[h]], axis=0).astype(BF16), m_bd[h])
                      for h in heads]
                xc = [xc[h] + xm[h][:c] for h in heads]
                mc = [xm[h][c:] for h in heads]
            else:
                xc = [xc[h] + _dot(xc[h].astype(BF16), m_bd[h]) for h in heads]
            power *= 2
            yield

        uw16 = [_dot(expand(xc[h]).astype(BF16), rhs[h]).astype(BF16) for h in heads]
        au_aw = [_dot(attn16[h], uw16[h]) for h in heads]
        au = [au_aw[h][:, :dk] for h in heads]
        e16 = [(qd[h] - au_aw[h][:, dk:]).astype(BF16) for h in heads]
        yield

        s = [state_ref[b * DN_HEADS + h] for h in heads]
        outs = [[] for _ in heads]
        for ch in range(tt // c):
            lo, hi = ch * c, (ch + 1) * c
            bc = [_dot_tn(kd[h][lo:hi], uw16[h][lo:hi]) for h in heads]
            for h in heads:
                s16 = s[h].astype(BF16)
                outs[h].append(_dot(e16[h][lo:hi], s16) + au[h][lo:hi])
                s[h] = (s[h] * jnp.exp(gtot[h][lo:lo + 1, :]) + bc[h][:, :dk]
                        - _dot(bc[h][:, dk:].astype(BF16), s16))
            yield
        for h in heads:
            state_ref[b * DN_HEADS + h] = s[h]
            o = jnp.concatenate(outs[h], axis=0)
            o_ref[b, :, h * dk:(h + 1) * dk] = _rms(o, normw_ref[...])
        yield

    programs = [batch_program(b) for b in range(n_batch)]
    live = [True] * n_batch
    wave = 0
    while any(live):
        for b in range(n_batch):
            if live[b] and wave >= DN_STAGE_LAG * b:
                live[b] = next(programs[b], "done") != "done"
        wave += 1


def _deltanet(qkv, ba, conv_w, alog_row, dtb_row, norm_w):
    bsz, t_len, width = qkv.shape
    tt = DN_TILE
    return pl.pallas_call(
        _dn_kernel,
        grid=(t_len // tt,),
        in_specs=[
            pl.BlockSpec((bsz, tt, width), lambda t: (0, t, 0)),
            pl.BlockSpec((bsz, tt, LANES), lambda t: (0, t, 0)),
            _const_spec(conv_w.shape),
            _const_spec((1, LANES)),
            _const_spec((1, LANES)),
            _const_spec((1, DN_HEAD_DIM)),
        ],
        out_specs=pl.BlockSpec((bsz, tt, DN_WIDTH), lambda t: (0, t, 0)),
        out_shape=jax.ShapeDtypeStruct((bsz, t_len, DN_WIDTH), F32),
        scratch_shapes=[
            pltpu.VMEM((bsz, SUBLANES + tt, width), F32),
            pltpu.VMEM((bsz * DN_HEADS, DN_HEAD_DIM, DN_HEAD_DIM), F32),
        ],
        compiler_params=pltpu.CompilerParams(
            dimension_semantics=("arbitrary",), vmem_limit_bytes=VMEM_LIMIT),
        name="deltanet",
    )(qkv, ba, conv_w, alog_row, dtb_row, norm_w)


MB_SUPER = 1
MB_GANG = 2
MB_UNROLL = 4
N_PIECES = 4
MB_AUX_KOFF = 2
MB_AUX_KSTART = MB_AUX_KOFF + N_PIECES
MB_AUX_BIAS_END = MB_AUX_KSTART + N_PIECES
MB_AUX_MASK = 16
assert MB_AUX_BIAS_END <= MB_AUX_MASK
MB_SUM_ROWS = 16
ALIBI_STEP = int(ALIBI_MAX_BIAS) // MB_HEADS
assert ALIBI_STEP * MB_HEADS == ALIBI_MAX_BIAS
LOG2E = math.log2(math.e)
LOG2E_PIECES = (1.4453125, -0.00262451171875, 7.063150405883789e-06, -1.05355866253376e-08)
assert len(LOG2E_PIECES) == N_PIECES


def _moba_kernel(qt_ref, k_ref, kmean_ref, vt_ref, o_ref, kaug_ref, sa_ref, sb_ref, *, n_blk):
    bs = MB_BLOCK
    hd = MB_HEAD_DIM
    sup = MB_SUPER * bs
    nbp = -(-n_blk // SUBLANES) * SUBLANES
    gang = pl.program_id(1)
    own = pl.program_id(2)
    pairs = range(MB_GANG)
    lane = lax.broadcasted_iota(jnp.int32, (bs, LANES), 1)
    row = lax.broadcasted_iota(jnp.int32, (bs, LANES), 0)

    @pl.when(own == 0)
    def _():
        def build(j, carry):
            off = pl.multiple_of(j * bs, bs)
            kstart = jnp.full((bs, LANES), j * bs, jnp.int32).astype(F32)
            aux = jnp.where(lane < MB_AUX_KOFF, 1.0,
                            jnp.where(lane < MB_AUX_KSTART, row.astype(F32),
                                      jnp.where(lane < MB_AUX_BIAS_END, kstart,
                                                jnp.where(lane == MB_AUX_MASK + j, 1.0, 0.0))))
            for pp in pairs:
                kaug_ref[pp, pl.ds(off, bs), 0:LANES] = k_ref[0, pl.ds(off, bs),
                                                              pp * LANES:(pp + 1) * LANES]
                kaug_ref[pp, pl.ds(off, bs), LANES:2 * LANES] = aux.astype(BF16)
            return carry

        lax.fori_loop(0, n_blk, build, 0)

    chan = lax.broadcasted_iota(jnp.int32, (LANES, bs), 0)
    blk = lax.broadcasted_iota(jnp.int32, (nbp, bs), 0)
    blk_f = blk.astype(F32)
    aux_row = lax.broadcasted_iota(jnp.int32, (MB_AUX_MASK, bs), 0)
    qpos = (lax.broadcasted_iota(jnp.int32, (MB_AUX_MASK, bs), 1) + own * bs).astype(F32)
    aux_pad = jnp.zeros((LANES - MB_AUX_MASK - nbp, bs), F32)

    heads = [(pp, hh) for pp in pairs for hh in range(2)]
    qth, gate = [], []
    for pp, hh in heads:
        qt = qt_ref[0, pp * LANES:(pp + 1) * LANES, :]
        km_hi, km_lo = _split2(kmean_ref[0, :, pp * LANES:(pp + 1) * LANES])
        qth.append(jnp.where((chan >= hh * hd) & (chan < (hh + 1) * hd), qt, 0.0))
        q_hi, q_lo = _split2(qth[-1])
        g = _dot(km_hi, q_hi) + _dot(km_hi, q_lo) + _dot(km_lo, q_hi)
        gate.append(jnp.where(blk < own, g, -jnp.inf))

    piece_id = (aux_row - MB_AUX_KOFF) & (N_PIECES - 1)
    piece = jnp.where(piece_id == 0, LOG2E_PIECES[0],
                      jnp.where(piece_id == 1, LOG2E_PIECES[1],
                                jnp.where(piece_id == 2, LOG2E_PIECES[2], LOG2E_PIECES[3])))
    q_rows = []
    for n, (pp, hh) in enumerate(heads):
        head = 2 * (MB_GANG * gang + pp) + hh
        slope_bits = (127 - ALIBI_STEP * (head + 1)) << 23
        slope = lax.bitcast_convert_type(jnp.full((MB_AUX_MASK, bs), slope_bits, jnp.int32), F32)
        qconst = -(slope * LOG2E) * qpos
        qconst_hi = qconst.astype(BF16).astype(F32)
        bias_rows = jnp.where(aux_row == 0, qconst_hi,
                              jnp.where(aux_row == 1, qconst - qconst_hi,
                                        jnp.where(aux_row < MB_AUX_BIAS_END, slope * piece, 0.0)))
        q_rows.append(jnp.concatenate([qth[n] * (hd ** -0.5 * LOG2E), bias_rows], axis=0))

    def query_operand(mask_rows):
        ops = [jnp.concatenate([q_rows[n], mask_rows[n], aux_pad], axis=0).astype(BF16)
               for n in range(len(heads))]
        return [jnp.concatenate(ops[2 * pp:2 * pp + 2], axis=1) for pp in pairs]

    def keys(pp, i):
        return kaug_ref[pp, pl.ds(pl.multiple_of(i * sup, sup), sup), :]

    def values_t(i, pp, hh):
        lo = pp * LANES + hh * hd
        return jnp.concatenate([vt_ref[i * MB_SUPER + u, lo:lo + hd, :]
                                for u in range(MB_SUPER)], axis=1)

    def produce(s_ref, pp, g, q_op, mask=None):
        s2 = _dot(keys(pp, g), q_op[pp])
        if mask is not None:
            s2 = jnp.where(mask, NEG_BIG, s2)
        s_ref[pp] = s2
        return jnp.max(s2, axis=0, keepdims=True)

    n_grp = n_blk // MB_SUPER
    grp = own // MB_SUPER

    assert MB_SUPER == 1
    rel = (lax.broadcasted_iota(jnp.int32, (sup, 2 * bs), 0) - (own - grp * MB_SUPER) * bs)
    qi = lax.broadcasted_iota(jnp.int32, (sup, 2 * bs), 1) & (bs - 1)
    future = (rel > qi) & (rel < bs)
    q_own = query_operand([jnp.zeros((nbp, bs), F32)] * len(heads))
    smax_a0 = [produce(sa_ref, pp, grp, q_own, future) for pp in pairs]

    sel = [jnp.zeros((nbp, bs), F32) for _ in heads]
    for _ in range(MB_TOPK):
        mx = [jnp.max(g, axis=0, keepdims=True) for g in gate]
        first = [jnp.min(jnp.where(g == m, blk_f, float(nbp)), axis=0, keepdims=True)
                 for g, m in zip(gate, mx)]
        hit = [blk_f == f for f in first]
        sel = [jnp.where(h, 1.0, s) for h, s in zip(hit, sel)]
        gate = [jnp.where(h, -jnp.inf, g) for h, g in zip(hit, gate)]
    keep = [jnp.where(blk < own, s, jnp.where(blk == own, 1.0, 0.0)) for s in sel]
    qaug2 = query_operand([jnp.where(k > 0.5, 0.0, NEG_BIG) for k in keep])

    def group_at(t):
        g = jnp.where(t == 0, grp, jnp.where(t > grp, grp + 1, t - 1))
        return jnp.minimum(g, n_grp - 1)

    ones_rows = jnp.ones((MB_SUM_ROWS, sup), BF16)

    def softmax_step(s_ref, pp, smax, g, carry):
        hs = range(2)
        m_i = [carry[2 * hh] for hh in hs]
        m_new = [jnp.maximum(m_i[hh], smax[:, hh * bs:(hh + 1) * bs]) for hh in hs]
        alpha = [jnp.exp2(m_i[hh] - m_new[hh]) for hh in hs]
        pexp = [jnp.exp2((s_ref[pp, :, hh * bs:(hh + 1) * bs] - m_new[hh]).astype(BF16))
                for hh in hs]
        acc_new = [carry[2 * hh + 1] * alpha[hh]
                   + _dot(jnp.concatenate([values_t(g, pp, hh), ones_rows], axis=0), pexp[hh])
                   for hh in hs]
        return (m_new[0], acc_new[0], m_new[1], acc_new[1])

    def steps(u, carry, unroll, t0):
        t = t0 + unroll * u
        smax_a = list(carry[:MB_GANG])
        stats = [carry[MB_GANG + 4 * pp:MB_GANG + 4 * pp + 4] for pp in pairs]
        for v in range(0, unroll, 2):
            smax_b = [produce(sb_ref, pp, group_at(t + v + 1), qaug2) for pp in pairs]
            stats = [softmax_step(sa_ref, pp, smax_a[pp], group_at(t + v), stats[pp])
                     for pp in pairs]
            smax_a = [produce(sa_ref, pp, group_at(t + v + 2), qaug2) for pp in pairs]
            stats = [softmax_step(sb_ref, pp, smax_b[pp], group_at(t + v + 1), stats[pp])
                     for pp in pairs]
        out = tuple(smax_a)
        for pp in pairs:
            out += tuple(stats[pp])
        return out

    stat0 = jnp.full((1, bs), -jnp.inf, F32)
    acc0 = jnp.zeros((hd + MB_SUM_ROWS, bs), F32)
    n_full = (grp + 1) // MB_UNROLL
    n_tail = (grp + 2 - n_full * MB_UNROLL) // 2
    fin = lax.fori_loop(0, n_full, functools.partial(steps, unroll=MB_UNROLL, t0=0),
                        tuple(smax_a0) + (stat0, acc0) * len(heads))
    fin = lax.fori_loop(0, n_tail, functools.partial(steps, unroll=2, t0=n_full * MB_UNROLL), fin)
    accs = [fin[MB_GANG + 2 * n + 1] for n in range(len(heads))]
    out_t = jnp.concatenate([a[:hd] / a[hd:hd + 1] for a in accs], axis=0)
    o_ref[0] = out_t.T


def _moba(qt, k, kmean, vt):
    bsz, t_len, _ = k.shape
    bs = MB_BLOCK
    n_blk = t_len // bs
    assert MB_UNROLL % 2 == 0 and n_blk % (MB_UNROLL * MB_SUPER) == 0
    assert MB_AUX_MASK + n_blk <= LANES
    assert n_blk % SUBLANES == 0 and MB_PAIRS % MB_GANG == 0
    gw = MB_GANG * LANES
    return pl.pallas_call(
        functools.partial(_moba_kernel, n_blk=n_blk),
        grid=(bsz, MB_PAIRS // MB_GANG, n_blk),
        in_specs=[
            pl.BlockSpec((1, gw, bs), lambda b, p, i: (b * n_blk + i, p, 0)),
            pl.BlockSpec((1, t_len, gw), lambda b, p, i: (b, 0, p)),
            pl.BlockSpec((1, n_blk, gw), lambda b, p, i: (b, 0, p)),
            pl.BlockSpec((n_blk, gw, bs), lambda b, p, i: (b, p, 0)),
        ],
        out_specs=pl.BlockSpec((1, bs, gw), lambda b, p, i: (b, i, p)),
        out_shape=jax.ShapeDtypeStruct((bsz, t_len, MB_WIDTH), F32),
        scratch_shapes=[
            pltpu.VMEM((MB_GANG, t_len, 2 * LANES), BF16),
            pltpu.VMEM((MB_GANG, MB_SUPER * bs, 2 * bs), F32),
            pltpu.VMEM((MB_GANG, MB_SUPER * bs, 2 * bs), F32),
        ],
        compiler_params=pltpu.CompilerParams(
            dimension_semantics=("arbitrary", "arbitrary", "arbitrary"),
            vmem_limit_bytes=VMEM_LIMIT),
        name="moba",
    )(qt, k, kmean, vt)


def _mixout_kernel(x_ref, odn_ref, omb_ref, prew_ref, wz_ref, wgd_ref, wgm_ref,
                   wbd_ref, wbm_ref, wo_ref, postw_ref, o_ref):
    x = x_ref[...]
    h = _rms(x, prew_ref[...]).astype(BF16)
    z = _dot_nt(h, wz_ref[...])
    gate_dn = jax.nn.sigmoid(_dot_nt(h, wgd_ref[...]))
    gate_mb = jax.nn.sigmoid(_dot_nt(h, wgm_ref[...]))
    o_dn = odn_ref[...] * (z * jax.nn.sigmoid(z))
    y_dn = _dot(o_dn.astype(BF16), wbd_ref[...])
    y_mb = _dot(omb_ref[...].astype(BF16), wbm_ref[...])
    merged = gate_dn * y_dn + gate_mb * y_mb
    y = _dot(merged.astype(BF16), wo_ref[...])
    o_ref[...] = x + _rms(y, postw_ref[...])


def _mix_out(x, o_dn, o_mb, pre_w, w_z, w_gd, w_gm, w_bd, w_bm, w_o, post_w, tm=512):
    n, d = x.shape
    return pl.pallas_call(
        _mixout_kernel,
        grid=(n // tm,),
        in_specs=[
            pl.BlockSpec((tm, d), lambda i: (i, 0)),
            pl.BlockSpec((tm, o_dn.shape[1]), lambda i: (i, 0)),
            pl.BlockSpec((tm, o_mb.shape[1]), lambda i: (i, 0)),
            _const_spec((1, d)),
            _const_spec(w_z.shape),
            _const_spec(w_gd.shape),
            _const_spec(w_gm.shape),
            _const_spec(w_bd.shape),
            _const_spec(w_bm.shape),
            _const_spec(w_o.shape),
            _const_spec((1, d)),
        ],
        out_specs=pl.BlockSpec((tm, d), lambda i: (i, 0)),
        out_shape=jax.ShapeDtypeStruct((n, d), F32),
        compiler_params=pltpu.CompilerParams(
            dimension_semantics=("arbitrary",), vmem_limit_bytes=VMEM_LIMIT),
        name="mix_out",
    )(x, o_dn, o_mb, pre_w, w_z, w_gd, w_gm, w_bd, w_bm, w_o, post_w)


def _layer(x, ffn1_pre_w, ffn1_w_gate, ffn1_w_up, ffn1_w_down, ffn1_post_w,
           mix_pre_w, w_in, dn_conv_w, dn_a_log, dn_dt_bias, dn_norm_w,
           w_branch_dn, w_branch_mb, w_out, mix_post_w,
           ffn2_pre_w, ffn2_w_gate, ffn2_w_up, ffn2_w_down, ffn2_post_w):
    bsz, t_len, d = x.shape
    n = bsz * t_len
    row = lambda w: w.reshape(1, -1).astype(F32)
    b16 = lambda w: w.astype(BF16)

    x = x.reshape(n, d)
    x = _ffn_block(x, row(ffn1_pre_w), b16(ffn1_w_gate), b16(ffn1_w_up), b16(ffn1_w_down),
                   row(ffn1_post_w))

    w_in_t = w_in.T
    o = 0
    w_dn = w_in_t[o:o + 3 * DN_WIDTH]; o += 3 * DN_WIDTH
    w_z = w_in_t[o:o + DN_WIDTH]; o += DN_WIDTH
    w_ba = w_in_t[o:o + 2 * DN_HEADS]; o += 2 * DN_HEADS
    w_q = w_in_t[o:o + MB_WIDTH]; o += MB_WIDTH
    w_k = w_in_t[o:o + MB_WIDTH]; o += MB_WIDTH
    w_v = w_in_t[o:o + MB_WIDTH]; o += MB_WIDTH
    w_gd = w_in_t[o:o + d]; o += d
    w_gm = w_in_t[o:o + d]; o += d
    w_ba = jnp.pad(w_ba, ((0, LANES - 2 * DN_HEADS), (0, 0)))

    def hi_lo(w):
        hi = w.astype(BF16)
        return jnp.stack([hi, (w - hi.astype(F32)).astype(BF16)])

    dn_qkv, ba, mb_k, mb_kmean, mb_qt, mb_vt = _in_proj(
        x, row(mix_pre_w), b16(w_dn), jnp.concatenate(list(hi_lo(w_ba)), axis=0), hi_lo(w_k),
        b16(w_q), b16(w_v))

    pad_heads = lambda p: jnp.pad(p.astype(F32), (DN_HEADS, LANES - 2 * DN_HEADS)).reshape(1, LANES)
    o_dn = _deltanet(dn_qkv.reshape(bsz, t_len, -1), ba.reshape(bsz, t_len, LANES),
                     dn_conv_w.astype(F32), pad_heads(dn_a_log), pad_heads(dn_dt_bias),
                     row(dn_norm_w))

    o_mb = _moba(mb_qt, mb_k.reshape(bsz, t_len, MB_WIDTH),
                 mb_kmean.reshape(bsz, t_len // MB_BLOCK, MB_WIDTH), mb_vt)

    x = _mix_out(x, o_dn.reshape(n, DN_WIDTH), o_mb.reshape(n, MB_WIDTH), row(mix_pre_w),
                 b16(w_z), b16(w_gd), b16(w_gm), b16(w_branch_dn), b16(w_branch_mb), b16(w_out),
                 row(mix_post_w))

    x = _ffn_block(x, row(ffn2_pre_w), b16(ffn2_w_gate), b16(ffn2_w_up), b16(ffn2_w_down),
                   row(ffn2_post_w))
    return x.reshape(bsz, t_len, d)


def kernel(x, ffn1_pre_w, ffn1_w_gate, ffn1_w_up, ffn1_w_down, ffn1_post_w, mix_pre_w, w_in, dn_conv_w, dn_a_log, dn_dt_bias, dn_norm_w, w_branch_dn, w_branch_mb, w_out, mix_post_w, ffn2_pre_w, ffn2_w_gate, ffn2_w_up, ffn2_w_down, ffn2_post_w):
    depth = w_in.shape[0]
    for l in range(depth):
        x = _layer(x, ffn1_pre_w[l], ffn1_w_gate[l], ffn1_w_up[l], ffn1_w_down[l], ffn1_post_w[l],
                   mix_pre_w[l], w_in[l], dn_conv_w[l], dn_a_log[l], dn_dt_bias[l], dn_norm_w[l],
                   w_branch_dn[l], w_branch_mb[l], w_out[l], mix_post_w[l],
                   ffn2_pre_w[l], ffn2_w_gate[l], ffn2_w_up[l], ffn2_w_down[l], ffn2_post_w[l])
    return x
```

```python
import functools
import math

import jax
import jax.numpy as jnp
from jax import lax
from jax.experimental import pallas as pl
from jax.experimental.pallas import tpu as pltpu

F32 = jnp.float32
BF16 = jnp.bfloat16

NORM_EPS = 1e-6
MACARON_WEIGHT = 0.5

DN_HEADS = 4
DN_HEAD_DIM = 128
DN_WIDTH = DN_HEADS * DN_HEAD_DIM
DN_CONV = 4
DN_CHUNK = 64
DN_TILE = 256
DN_STAGE_LAG = 1 + DN_HEADS + (DN_CHUNK.bit_length() - 2)

MB_HEADS = 8
MB_HEAD_DIM = 64
MB_WIDTH = MB_HEADS * MB_HEAD_DIM
MB_BLOCK = 256
MB_TOPK = 3
ALIBI_MAX_BIAS = 8.0
LANES = 128
SUBLANES = 8
MB_PAIRS = MB_WIDTH // LANES
NEG_BIG = -1e30

VMEM_LIMIT = 56 * 1024 * 1024


def _rms(x, w):
    ms = jnp.mean(x * x, axis=-1, keepdims=True)
    return x * lax.rsqrt(ms + NORM_EPS) * w


def _dot(a, b):
    return jnp.dot(a, b, preferred_element_type=F32)


def _dot_nt(a, b):
    return lax.dot_general(a, b, (((1,), (1,)), ((), ())), preferred_element_type=F32)


def _dot_tn(a, b):
    return lax.dot_general(a, b, (((0,), (0,)), ((), ())), preferred_element_type=F32)


def _split2(x):
    hi = x.astype(BF16)
    lo = (x - hi.astype(F32)).astype(BF16)
    return hi, lo


def _const_spec(shape):
    nd = len(shape)
    return pl.BlockSpec(shape, lambda *_: (0,) * nd, pipeline_mode=pl.Buffered(1))


def _ffn_kernel(x_ref, prew_ref, wg_ref, wu_ref, wd_ref, postw_ref, o_ref):
    x = x_ref[...]
    xn = _rms(x, prew_ref[...]).astype(BF16)
    g = _dot(xn, wg_ref[...])
    u = _dot(xn, wu_ref[...])
    a = (g * jax.nn.sigmoid(g) * u).astype(BF16)
    h = _dot(a, wd_ref[...])
    o_ref[...] = x + MACARON_WEIGHT * _rms(h, postw_ref[...])


def _ffn_block(x, pre_w, w_gate, w_up, w_down, post_w, tm=512):
    n, d = x.shape
    dff = w_gate.shape[1]
    return pl.pallas_call(
        _ffn_kernel,
        grid=(n // tm,),
        in_specs=[
            pl.BlockSpec((tm, d), lambda i: (i, 0)),
            _const_spec((1, d)),
            _const_spec((d, dff)),
            _const_spec((d, dff)),
            _const_spec((dff, d)),
            _const_spec((1, d)),
        ],
        out_specs=pl.BlockSpec((tm, d), lambda i: (i, 0)),
        out_shape=jax.ShapeDtypeStruct((n, d), F32),
        compiler_params=pltpu.CompilerParams(
            dimension_semantics=("arbitrary",), vmem_limit_bytes=VMEM_LIMIT),
        name="ffn_block",
    )(x, pre_w, w_gate, w_up, w_down, post_w)


def _inproj_kernel(x_ref, prew_ref, wdn_ref, wba_ref, wk_ref, wq_ref, wv_ref,
                   dn_ref, ba_ref, k_ref, kmean_ref, qt_ref, vt_ref):
    bs = MB_BLOCK
    nb = qt_ref.shape[0]
    h = _rms(x_ref[...], prew_ref[...])
    h16 = h.astype(BF16)
    dn_ref[...] = _dot_nt(h16, wdn_ref[...])
    ba2 = _dot_nt(h16, wba_ref[...])
    ba_ref[...] = ba2[:, :LANES] + ba2[:, LANES:]
    k_ref[...] = _dot_nt(h16, wk_ref[0]).astype(BF16)
    hbar = jnp.concatenate([jnp.mean(h[i * bs:(i + 1) * bs], axis=0, keepdims=True)
                            for i in range(nb)]
                           + [jnp.zeros((SUBLANES - nb, h.shape[1]), F32)], axis=0)
    hb_hi, hb_lo = _split2(hbar)
    kmean = _dot_nt(hb_hi, wk_ref[0]) + _dot_nt(hb_lo, wk_ref[0]) + _dot_nt(hb_hi, wk_ref[1])
    kmean_ref[0] = kmean[:nb]
    qt = _dot_nt(wq_ref[...], h16)
    vt = _dot_nt(wv_ref[...], h16).astype(BF16)
    for i in range(nb):
        qt_ref[i] = qt[:, i * bs:(i + 1) * bs]
        vt_ref[i] = vt[:, i * bs:(i + 1) * bs]


def _in_proj(x, pre_w, w_dn, w_ba, w_k, w_qt, w_vt, tm=512):
    n, d = x.shape
    bs = MB_BLOCK
    return pl.pallas_call(
        _inproj_kernel,
        grid=(n // tm,),
        in_specs=[
            pl.BlockSpec((tm, d), lambda i: (i, 0)),
            _const_spec((1, d)),
            _const_spec(w_dn.shape),
            _const_spec(w_ba.shape),
            _const_spec(w_k.shape),
            _const_spec(w_qt.shape),
            _const_spec(w_vt.shape),
        ],
        out_specs=[
            pl.BlockSpec((tm, w_dn.shape[0]), lambda i: (i, 0)),
            pl.BlockSpec((tm, LANES), lambda i: (i, 0)),
            pl.BlockSpec((tm, MB_WIDTH), lambda i: (i, 0)),
            pl.BlockSpec((1, tm // bs, MB_WIDTH), lambda i: (i, 0, 0)),
            pl.BlockSpec((tm // bs, MB_WIDTH, bs), lambda i: (i, 0, 0)),
            pl.BlockSpec((tm // bs, MB_WIDTH, bs), lambda i: (i, 0, 0)),
        ],
        out_shape=[
            jax.ShapeDtypeStruct((n, w_dn.shape[0]), F32),
            jax.ShapeDtypeStruct((n, LANES), F32),
            jax.ShapeDtypeStruct((n, MB_WIDTH), BF16),
            jax.ShapeDtypeStruct((n // tm, tm // bs, MB_WIDTH), F32),
            jax.ShapeDtypeStruct((n // bs, MB_WIDTH, bs), F32),
            jax.ShapeDtypeStruct((n // bs, MB_WIDTH, bs), BF16),
        ],
        compiler_params=pltpu.CompilerParams(
            dimension_semantics=("arbitrary",), vmem_limit_bytes=VMEM_LIMIT),
        name="in_proj",
    )(x, pre_w, w_dn, w_ba, w_k, w_qt, w_vt)


def _dn_kernel(qkv_ref, ba_ref, convw_ref, alog_ref, dtb_ref, normw_ref, o_ref,
               xbuf_ref, state_ref):
    tt = DN_TILE
    c = DN_CHUNK
    dk = DN_HEAD_DIM
    n_batch = qkv_ref.shape[0]
    heads = range(DN_HEADS)
    pad = SUBLANES
    assert DN_CONV - 1 <= pad

    @pl.when(pl.program_id(0) == 0)
    def _():
        xbuf_ref[:, 0:pad, :] = jnp.zeros((n_batch, pad, 3 * DN_WIDTH), F32)
        state_ref[...] = jnp.zeros_like(state_ref)

    ri = lax.broadcasted_iota(jnp.int32, (tt, tt), 0)
    ci = lax.broadcasted_iota(jnp.int32, (tt, tt), 1)
    same_chunk = (ri // c) == (ci // c)
    tril = jnp.where(same_chunk & (ri >= ci), 1.0, 0.0).astype(BF16)
    ones_bd = jnp.where(same_chunk, 1.0, 0.0).astype(BF16)
    cc_row = lax.broadcasted_iota(jnp.int32, (c, tt), 0)
    cc_lane = lax.broadcasted_iota(jnp.int32, (c, tt), 1)
    cc_chunk = cc_lane // c
    cc_diag = cc_row == cc_lane % c
    cc_incl = cc_row >= cc_lane % c

    def batch_program(b):
        x = qkv_ref[b]
        xbuf_ref[b, pad:pad + tt, :] = x
        cw = convw_ref[...]
        y = x * cw[DN_CONV - 1:DN_CONV, :]
        for s in range(1, DN_CONV):
            y = y + xbuf_ref[b, pad - s:pad - s + tt, :] * cw[DN_CONV - 1 - s:DN_CONV - s, :]
        xbuf_ref[b, 0:pad, :] = x[tt - pad:tt, :]
        y = y * jax.nn.sigmoid(y)

        ba = ba_ref[b]
        beta_all = jax.nn.sigmoid(ba)
        g_all = -jnp.exp(alog_ref[...]) * jax.nn.softplus(ba + dtb_ref[...])

        g1 = g_all.astype(BF16)
        r1 = g_all - g1.astype(F32)
        g2 = r1.astype(BF16)
        g3 = (r1 - g2.astype(F32)).astype(BF16)
        gcs_all = _dot(tril, g1) + _dot(tril, g2) + _dot(tril, g3)
        gtot_all = _dot(ones_bd, g1) + _dot(ones_bd, g2) + _dot(ones_bd, g3)
        yield

        def compact(a):
            out = jnp.broadcast_to(a[:c], (c, tt))
            for ch in range(1, tt // c):
                out = jnp.where(cc_chunk == ch,
                                jnp.broadcast_to(a[ch * c:(ch + 1) * c], (c, tt)), out)
            return out

        def expand(a):
            return jnp.where(same_chunk, jnp.concatenate([a] * (tt // c), axis=0), 0.0)

        lmat_c, lmat16, attn16, rhs, qd, kd, gtot = [], [], [], [], [], [], []
        for h in heads:
            qr = y[:, h * dk:(h + 1) * dk]
            kr = y[:, DN_WIDTH + h * dk:DN_WIDTH + (h + 1) * dk]
            v = y[:, 2 * DN_WIDTH + h * dk:2 * DN_WIDTH + (h + 1) * dk]
            q = qr * lax.rsqrt(jnp.sum(qr * qr, axis=-1, keepdims=True) + NORM_EPS) * (dk ** -0.5)
            k = kr * lax.rsqrt(jnp.sum(kr * kr, axis=-1, keepdims=True) + NORM_EPS)
            beta = beta_all[:, h:h + 1]
            gcs = gcs_all[:, DN_HEADS + h:DN_HEADS + h + 1]
            gtot.append(gtot_all[:, DN_HEADS + h:DN_HEADS + h + 1])
            eg = jnp.exp(gcs)

            g_i = compact(gcs)
            g_j = jnp.sum(jnp.where(cc_diag, g_i, 0.0), axis=0, keepdims=True)
            decay = jnp.exp(jnp.where(cc_incl, g_i - g_j, NEG_BIG))

            kb = k * beta
            k16 = k.astype(BF16)
            lmat_c.append(jnp.where(cc_diag, 0.0, compact(_dot_nt(kb.astype(BF16), k16)) * decay))
            lmat16.append(expand(lmat_c[h]).astype(BF16))
            attn16.append(expand(compact(_dot_nt(q.astype(BF16), k16)) * decay).astype(BF16))
            rhs.append(jnp.concatenate([v * beta, kb * eg], axis=1).astype(BF16))
            qd.append(q * eg)
            kd.append((k * jnp.exp(gtot[h] - gcs)).astype(BF16))
            yield

        xc = [jnp.where(cc_diag, 1.0, -lmat_c[h]) for h in heads]
        mc = [_dot(lmat_c[h].astype(BF16), lmat16[h]) for h in heads]
        power = 2
        while power < c:
            m_bd = [expand(mc[h]).astype(BF16) for h in heads]
            if 2 * power < c:
                xm = [_dot(jnp.concatenate([xc[h], mc[h]], axis=0).astype(BF16), m_bd[h])
                      for h in heads]
                xc = [xc[h] + xm[h][:c] for h in heads]
                mc = [xm[h][c:] for h in heads]
            else:
                xc = [xc[h] + _dot(xc[h].astype(BF16), m_bd[h]) for h in heads]
            power *= 2
            yield

        uw16 = [_dot(expand(xc[h]).astype(BF16), rhs[h]).astype(BF16) for h in heads]
        au_aw = [_dot(attn16[h], uw16[h]) for h in heads]
        au = [au_aw[h][:, :dk] for h in heads]
        e16 = [(qd[h] - au_aw[h][:, dk:]).astype(BF16) for h in heads]
        yield

        s = [state_ref[b * DN_HEADS + h] for h in heads]
        outs = [[] for _ in heads]
        for ch in range(tt // c):
            lo, hi = ch * c, (ch + 1) * c
            bc = [_dot_tn(kd[h][lo:hi], uw16[h][lo:hi]) for h in heads]
            for h in heads:
                s16 = s[h].astype(BF16)
                outs[h].append(_dot(e16[h][lo:hi], s16) + au[h][lo:hi])
                s[h] = (s[h] * jnp.exp(gtot[h][lo:lo + 1, :]) + bc[h][:, :dk]
                        - _dot(bc[h][:, dk:].astype(BF16), s16))
            yield
        for h in heads:
            state_ref[b * DN_HEADS + h] = s[h]
            o = jnp.concatenate(outs[h], axis=0)
            o_ref[b, :, h * dk:(h + 1) * dk] = _rms(o, normw_ref[...])
        yield

    programs = [batch_program(b) for b in range(n_batch)]
    live = [True] * n_batch
    wave = 0
    while any(live):
        for b in range(n_batch):
            if live[b] and wave >= DN_STAGE_LAG * b:
                live[b] = next(programs[b], "done") != "done"
        wave += 1


def _deltanet(qkv, ba, conv_w, alog_row, dtb_row, norm_w):
    bsz, t_len, width = qkv.shape
    tt = DN_TILE
    return pl.pallas_call(
        _dn_kernel,
        grid=(t_len // tt,),
        in_specs=[
            pl.BlockSpec((bsz, tt, width), lambda t: (0, t, 0)),
            pl.BlockSpec((bsz, tt, LANES), lambda t: (0, t, 0)),
            _const_spec(conv_w.shape),
            _const_spec((1, LANES)),
            _const_spec((1, LANES)),
            _const_spec((1, DN_HEAD_DIM)),
        ],
        out_specs=pl.BlockSpec((bsz, tt, DN_WIDTH), lambda t: (0, t, 0)),
        out_shape=jax.ShapeDtypeStruct((bsz, t_len, DN_WIDTH), F32),
        scratch_shapes=[
            pltpu.VMEM((bsz, SUBLANES + tt, width), F32),
            pltpu.VMEM((bsz * DN_HEADS, DN_HEAD_DIM, DN_HEAD_DIM), F32),
        ],
        compiler_params=pltpu.CompilerParams(
            dimension_semantics=("arbitrary",), vmem_limit_bytes=VMEM_LIMIT),
        name="deltanet",
    )(qkv, ba, conv_w, alog_row, dtb_row, norm_w)


MB_SUPER = 1
MB_GANG = 2
MB_UNROLL = 4
N_PIECES = 4
MB_AUX_KOFF = 2
MB_AUX_KSTART = MB_AUX_KOFF + N_PIECES
MB_AUX_BIAS_END = MB_AUX_KSTART + N_PIECES
MB_AUX_MASK = 16
assert MB_AUX_BIAS_END <= MB_AUX_MASK
MB_SUM_ROWS = 16
ALIBI_STEP = int(ALIBI_MAX_BIAS) // MB_HEADS
assert ALIBI_STEP * MB_HEADS == ALIBI_MAX_BIAS
LOG2E = math.log2(math.e)
LOG2E_PIECES = (1.4453125, -0.00262451171875, 7.063150405883789e-06, -1.05355866253376e-08)
assert len(LOG2E_PIECES) == N_PIECES


def _moba_kernel(qt_ref, k_ref, kmean_ref, vt_ref, o_ref, kaug_ref, sa_ref, sb_ref, *, n_blk):
    bs = MB_BLOCK
    hd = MB_HEAD_DIM
    sup = MB_SUPER * bs
    nbp = -(-n_blk // SUBLANES) * SUBLANES
    gang = pl.program_id(1)
    own = pl.program_id(2)
    pairs = range(MB_GANG)
    lane = lax.broadcasted_iota(jnp.int32, (bs, LANES), 1)
    row = lax.broadcasted_iota(jnp.int32, (bs, LANES), 0)

    @pl.when(own == 0)
    def _():
        def build(j, carry):
            off = pl.multiple_of(j * bs, bs)
            kstart = jnp.full((bs, LANES), j * bs, jnp.int32).astype(F32)
            aux = jnp.where(lane < MB_AUX_KOFF, 1.0,
                            jnp.where(lane < MB_AUX_KSTART, row.astype(F32),
                                      jnp.where(lane < MB_AUX_BIAS_END, kstart,
                                                jnp.where(lane == MB_AUX_MASK + j, 1.0, 0.0))))
            for pp in pairs:
                kaug_ref[pp, pl.ds(off, bs), 0:LANES] = k_ref[0, pl.ds(off, bs),
                                                              pp * LANES:(pp + 1) * LANES]
                kaug_ref[pp, pl.ds(off, bs), LANES:2 * LANES] = aux.astype(BF16)
            return carry

        lax.fori_loop(0, n_blk, build, 0)

    chan = lax.broadcasted_iota(jnp.int32, (LANES, bs), 0)
    blk = lax.broadcasted_iota(jnp.int32, (nbp, bs), 0)
    blk_f = blk.astype(F32)
    aux_row = lax.broadcasted_iota(jnp.int32, (MB_AUX_MASK, bs), 0)
    qpos = (lax.broadcasted_iota(jnp.int32, (MB_AUX_MASK, bs), 1) + own * bs).astype(F32)
    aux_pad = jnp.zeros((LANES - MB_AUX_MASK - nbp, bs), F32)

    heads = [(pp, hh) for pp in pairs for hh in range(2)]
    qth, gate = [], []
    for pp, hh in heads:
        qt = qt_ref[0, pp * LANES:(pp + 1) * LANES, :]
        km_hi, km_lo = _split2(kmean_ref[0, :, pp * LANES:(pp + 1) * LANES])
        qth.append(jnp.where((chan >= hh * hd) & (chan < (hh + 1) * hd), qt, 0.0))
        q_hi, q_lo = _split2(qth[-1])
        g = _dot(km_hi, q_hi) + _dot(km_hi, q_lo) + _dot(km_lo, q_hi)
        gate.append(jnp.where(blk < own, g, -jnp.inf))

    piece_id = (aux_row - MB_AUX_KOFF) & (N_PIECES - 1)
    piece = jnp.where(piece_id == 0, LOG2E_PIECES[0],
                      jnp.where(piece_id == 1, LOG2E_PIECES[1],
                                jnp.where(piece_id == 2, LOG2E_PIECES[2], LOG2E_PIECES[3])))
    q_rows = []
    for n, (pp, hh) in enumerate(heads):
        head = 2 * (MB_GANG * gang + pp) + hh
        slope_bits = (127 - ALIBI_STEP * (head + 1)) << 23
        slope = lax.bitcast_convert_type(jnp.full((MB_AUX_MASK, bs), slope_bits, jnp.int32), F32)
        qconst = -(slope * LOG2E) * qpos
        qconst_hi = qconst.astype(BF16).astype(F32)
        bias_rows = jnp.where(aux_row == 0, qconst_hi,
                              jnp.where(aux_row == 1, qconst - qconst_hi,
                                        jnp.where(aux_row < MB_AUX_BIAS_END, slope * piece, 0.0)))
        q_rows.append(jnp.concatenate([qth[n] * (hd ** -0.5 * LOG2E), bias_rows], axis=0))

    def query_operand(mask_rows):
        ops = [jnp.concatenate([q_rows[n], mask_rows[n], aux_pad], axis=0).astype(BF16)
               for n in range(len(heads))]
        return [jnp.concatenate(ops[2 * pp:2 * pp + 2], axis=1) for pp in pairs]

    def keys(pp, i):
        return kaug_ref[pp, pl.ds(pl.multiple_of(i * sup, sup), sup), :]

    def values_t(i, pp, hh):
        lo = pp * LANES + hh * hd
        return jnp.concatenate([vt_ref[i * MB_SUPER + u, lo:lo + hd, :]
                                for u in range(MB_SUPER)], axis=1)

    def produce(s_ref, pp, g, q_op, mask=None):
        s2 = _dot(keys(pp, g), q_op[pp])
        if mask is not None:
            s2 = jnp.where(mask, NEG_BIG, s2)
        s_ref[pp] = s2
        return jnp.max(s2, axis=0, keepdims=True)

    n_grp = n_blk // MB_SUPER
    grp = own // MB_SUPER

    assert MB_SUPER == 1
    rel = (lax.broadcasted_iota(jnp.int32, (sup, 2 * bs), 0) - (own - grp * MB_SUPER) * bs)
    qi = lax.broadcasted_iota(jnp.int32, (sup, 2 * bs), 1) & (bs - 1)
    future = (rel > qi) & (rel < bs)
    q_own = query_operand([jnp.zeros((nbp, bs), F32)] * len(heads))
    smax_a0 = [produce(sa_ref, pp, grp, q_own, future) for pp in pairs]

    sel = [jnp.zeros((nbp, bs), F32) for _ in heads]
    for _ in range(MB_TOPK):
        mx = [jnp.max(g, axis=0, keepdims=True) for g in gate]
        first = [jnp.min(jnp.where(g == m, blk_f, float(nbp)), axis=0, keepdims=True)
                 for g, m in zip(gate, mx)]
        hit = [blk_f == f for f in first]
        sel = [jnp.where(h, 1.0, s) for h, s in zip(hit, sel)]
        gate = [jnp.where(h, -jnp.inf, g) for h, g in zip(hit, gate)]
    keep = [jnp.where(blk < own, s, jnp.where(blk == own, 1.0, 0.0)) for s in sel]
    qaug2 = query_operand([jnp.where(k > 0.5, 0.0, NEG_BIG) for k in keep])

    def group_at(t):
        g = jnp.where(t == 0, grp, jnp.where(t > grp, grp + 1, t - 1))
        return jnp.minimum(g, n_grp - 1)

    ones_rows = jnp.ones((MB_SUM_ROWS, sup), BF16)

    def softmax_step(s_ref, pp, smax, g, carry):
        hs = range(2)
        m_i = [carry[2 * hh] for hh in hs]
        m_new = [jnp.maximum(m_i[hh], smax[:, hh * bs:(hh + 1) * bs]) for hh in hs]
        alpha = [jnp.exp2(m_i[hh] - m_new[hh]) for hh in hs]
        pexp = [jnp.exp2((s_ref[pp, :, hh * bs:(hh + 1) * bs] - m_new[hh]).astype(BF16))
                for hh in hs]
        acc_new = [carry[2 * hh + 1] * alpha[hh]
                   + _dot(jnp.concatenate([values_t(g, pp, hh), ones_rows], axis=0), pexp[hh])
                   for hh in hs]
        return (m_new[0], acc_new[0], m_new[1], acc_new[1])

    def steps(u, carry, unroll, t0):
        t = t0 + unroll * u
        smax_a = list(carry[:MB_GANG])
        stats = [carry[MB_GANG + 4 * pp:MB_GANG + 4 * pp + 4] for pp in pairs]
        for v in range(0, unroll, 2):
            smax_b = [produce(sb_ref, pp, group_at(t + v + 1), qaug2) for pp in pairs]
            stats = [softmax_step(sa_ref, pp, smax_a[pp], group_at(t + v), stats[pp])
                     for pp in pairs]
            smax_a = [produce(sa_ref, pp, group_at(t + v + 2), qaug2) for pp in pairs]
            stats = [softmax_step(sb_ref, pp, smax_b[pp], group_at(t + v + 1), stats[pp])
                     for pp in pairs]
        out = tuple(smax_a)
        for pp in pairs:
            out += tuple(stats[pp])
        return out

    stat0 = jnp.full((1, bs), -jnp.inf, F32)
    acc0 = jnp.zeros((hd + MB_SUM_ROWS, bs), F32)
    n_full = (grp + 1) // MB_UNROLL
    n_tail = (grp + 2 - n_full * MB_UNROLL) // 2
    fin = lax.fori_loop(0, n_full, functools.partial(steps, unroll=MB_UNROLL, t0=0),
                        tuple(smax_a0) + (stat0, acc0) * len(heads))
    fin = lax.fori_loop(0, n_tail, functools.partial(steps, unroll=2, t0=n_full * MB_UNROLL), fin)
    accs = [fin[MB_GANG + 2 * n + 1] for n in range(len(heads))]
    out_t = jnp.concatenate([a[:hd] / a[hd:hd + 1] for a in accs], axis=0)
    o_ref[0] = out_t


def _moba(qt, k, kmean, vt):
    bsz, t_len, _ = k.shape
    bs = MB_BLOCK
    n_blk = t_len // bs
    assert MB_UNROLL % 2 == 0 and n_blk % (MB_UNROLL * MB_SUPER) == 0
    assert MB_AUX_MASK + n_blk <= LANES
    assert n_blk % SUBLANES == 0 and MB_PAIRS % MB_GANG == 0
    gw = MB_GANG * LANES
    return pl.pallas_call(
        functools.partial(_moba_kernel, n_blk=n_blk),
        grid=(bsz, MB_PAIRS // MB_GANG, n_blk),
        in_specs=[
            pl.BlockSpec((1, gw, bs), lambda b, p, i: (b * n_blk + i, p, 0)),
            pl.BlockSpec((1, t_len, gw), lambda b, p, i: (b, 0, p)),
            pl.BlockSpec((1, n_blk, gw), lambda b, p, i: (b, 0, p)),
            pl.BlockSpec((n_blk, gw, bs), lambda b, p, i: (b, p, 0)),
        ],
        out_specs=pl.BlockSpec((1, gw, bs), lambda b, p, i: (b * n_blk + i, p, 0)),
        out_shape=jax.ShapeDtypeStruct((bsz * n_blk, MB_WIDTH, bs), F32),
        scratch_shapes=[
            pltpu.VMEM((MB_GANG, t_len, 2 * LANES), BF16),
            pltpu.VMEM((MB_GANG, MB_SUPER * bs, 2 * bs), F32),
            pltpu.VMEM((MB_GANG, MB_SUPER * bs, 2 * bs), F32),
        ],
        compiler_params=pltpu.CompilerParams(
            dimension_semantics=("arbitrary", "arbitrary", "arbitrary"),
            vmem_limit_bytes=VMEM_LIMIT),
        name="moba",
    )(qt, k, kmean, vt)


def _mixout_kernel(x_ref, odn_ref, omb_ref, prew_ref, wz_ref, wgd_ref, wgm_ref,
                   wbd_ref, wbm_ref, wo_ref, postw_ref, o_ref):
    x = x_ref[...]
    h = _rms(x, prew_ref[...]).astype(BF16)
    z = _dot_nt(h, wz_ref[...])
    gate_dn = jax.nn.sigmoid(_dot_nt(h, wgd_ref[...]))
    gate_mb = jax.nn.sigmoid(_dot_nt(h, wgm_ref[...]))
    o_dn = odn_ref[...] * (z * jax.nn.sigmoid(z))
    y_dn = _dot(o_dn.astype(BF16), wbd_ref[...])
    y_mb = jnp.concatenate([_dot_tn(omb_ref[i].astype(BF16), wbm_ref[...])
                            for i in range(omb_ref.shape[0])], axis=0)
    merged = gate_dn * y_dn + gate_mb * y_mb
    y = _dot(merged.astype(BF16), wo_ref[...])
    o_ref[...] = x + _rms(y, postw_ref[...])


def _mix_out(x, o_dn, o_mb, pre_w, w_z, w_gd, w_gm, w_bd, w_bm, w_o, post_w, tm=512):
    n, d = x.shape
    return pl.pallas_call(
        _mixout_kernel,
        grid=(n // tm,),
        in_specs=[
            pl.BlockSpec((tm, d), lambda i: (i, 0)),
            pl.BlockSpec((tm, o_dn.shape[1]), lambda i: (i, 0)),
            pl.BlockSpec((tm // MB_BLOCK,) + o_mb.shape[1:], lambda i: (i, 0, 0)),
            _const_spec((1, d)),
            _const_spec(w_z.shape),
            _const_spec(w_gd.shape),
            _const_spec(w_gm.shape),
            _const_spec(w_bd.shape),
            _const_spec(w_bm.shape),
            _const_spec(w_o.shape),
            _const_spec((1, d)),
        ],
        out_specs=pl.BlockSpec((tm, d), lambda i: (i, 0)),
        out_shape=jax.ShapeDtypeStruct((n, d), F32),
        compiler_params=pltpu.CompilerParams(
            dimension_semantics=("arbitrary",), vmem_limit_bytes=VMEM_LIMIT),
        name="mix_out",
    )(x, o_dn, o_mb, pre_w, w_z, w_gd, w_gm, w_bd, w_bm, w_o, post_w)


def _layer(x, ffn1_pre_w, ffn1_w_gate, ffn1_w_up, ffn1_w_down, ffn1_post_w,
           mix_pre_w, w_in, dn_conv_w, dn_a_log, dn_dt_bias, dn_norm_w,
           w_branch_dn, w_branch_mb, w_out, mix_post_w,
           ffn2_pre_w, ffn2_w_gate, ffn2_w_up, ffn2_w_down, ffn2_post_w):
    bsz, t_len, d = x.shape
    n = bsz * t_len
    row = lambda w: w.reshape(1, -1).astype(F32)
    b16 = lambda w: w.astype(BF16)

    x = x.reshape(n, d)
    x = _ffn_block(x, row(ffn1_pre_w), b16(ffn1_w_gate), b16(ffn1_w_up), b16(ffn1_w_down),
                   row(ffn1_post_w))

    w_in_t = w_in.T
    o = 0
    w_dn = w_in_t[o:o + 3 * DN_WIDTH]; o += 3 * DN_WIDTH
    w_z = w_in_t[o:o + DN_WIDTH]; o += DN_WIDTH
    w_ba = w_in_t[o:o + 2 * DN_HEADS]; o += 2 * DN_HEADS
    w_q = w_in_t[o:o + MB_WIDTH]; o += MB_WIDTH
    w_k = w_in_t[o:o + MB_WIDTH]; o += MB_WIDTH
    w_v = w_in_t[o:o + MB_WIDTH]; o += MB_WIDTH
    w_gd = w_in_t[o:o + d]; o += d
    w_gm = w_in_t[o:o + d]; o += d
    w_ba = jnp.pad(w_ba, ((0, LANES - 2 * DN_HEADS), (0, 0)))

    def hi_lo(w):
        hi = w.astype(BF16)
        return jnp.stack([hi, (w - hi.astype(F32)).astype(BF16)])

    dn_qkv, ba, mb_k, mb_kmean, mb_qt, mb_vt = _in_proj(
        x, row(mix_pre_w), b16(w_dn), jnp.concatenate(list(hi_lo(w_ba)), axis=0), hi_lo(w_k),
        b16(w_q), b16(w_v))

    pad_heads = lambda p: jnp.pad(p.astype(F32), (DN_HEADS, LANES - 2 * DN_HEADS)).reshape(1, LANES)
    o_dn = _deltanet(dn_qkv.reshape(bsz, t_len, -1), ba.reshape(bsz, t_len, LANES),
                     dn_conv_w.astype(F32), pad_heads(dn_a_log), pad_heads(dn_dt_bias),
                     row(dn_norm_w))

    o_mb = _moba(mb_qt, mb_k.reshape(bsz, t_len, MB_WIDTH),
                 mb_kmean.reshape(bsz, t_len // MB_BLOCK, MB_WIDTH), mb_vt)

    x = _mix_out(x, o_dn.reshape(n, DN_WIDTH), o_mb, row(mix_pre_w),
                 b16(w_z), b16(w_gd), b16(w_gm), b16(w_branch_dn), b16(w_branch_mb), b16(w_out),
                 row(mix_post_w))

    x = _ffn_block(x, row(ffn2_pre_w), b16(ffn2_w_gate), b16(ffn2_w_up), b16(ffn2_w_down),
                   row(ffn2_post_w))
    return x.reshape(bsz, t_len, d)


def kernel(x, ffn1_pre_w, ffn1_w_gate, ffn1_w_up, ffn1_w_down, ffn1_post_w, mix_pre_w, w_in, dn_conv_w, dn_a_log, dn_dt_bias, dn_norm_w, w_branch_dn, w_branch_mb, w_out, mix_post_w, ffn2_pre_w, ffn2_w_gate, ffn2_w_up, ffn2_w_down, ffn2_post_w):
    depth = w_in.shape[0]
    for l in range(depth):
        x = _layer(x, ffn1_pre_w[l], ffn1_w_gate[l], ffn1_w_up[l], ffn1_w_down[l], ffn1_post_w[l],
                   mix_pre_w[l], w_in[l], dn_conv_w[l], dn_a_log[l], dn_dt_bias[l], dn_norm_w[l],
                   w_branch_dn[l], w_branch_mb[l], w_out[l], mix_post_w[l],
                   ffn2_pre_w[l], ffn2_w_gate[l], ffn2_w_up[l], ffn2_w_down[l], ffn2_post_w[l])
    return x
```

```python
import functools
import math

import jax
import jax.numpy as jnp
from jax import lax
from jax.experimental import pallas as pl
from jax.experimental.pallas import tpu as pltpu

F32 = jnp.float32
BF16 = jnp.bfloat16

NORM_EPS = 1e-6
MACARON_WEIGHT = 0.5

DN_HEADS = 4
DN_HEAD_DIM = 128
DN_WIDTH = DN_HEADS * DN_HEAD_DIM
DN_CONV = 4
DN_CHUNK = 64
DN_TILE = 256
DN_STAGE_LAG = 1 + DN_HEADS + (DN_CHUNK.bit_length() - 2)

MB_HEADS = 8
MB_HEAD_DIM = 64
MB_WIDTH = MB_HEADS * MB_HEAD_DIM
MB_BLOCK = 256
MB_TOPK = 3
ALIBI_MAX_BIAS = 8.0
LANES = 128
SUBLANES = 8
MB_PAIRS = MB_WIDTH // LANES
NEG_BIG = -1e30

VMEM_LIMIT = 56 * 1024 * 1024


def _rms(x, w):
    ms = jnp.mean(x * x, axis=-1, keepdims=True)
    return x * lax.rsqrt(ms + NORM_EPS) * w


def _dot(a, b):
    return jnp.dot(a, b, preferred_element_type=F32)


def _dot_nt(a, b):
    return lax.dot_general(a, b, (((1,), (1,)), ((), ())), preferred_element_type=F32)


def _dot_tn(a, b):
    return lax.dot_general(a, b, (((0,), (0,)), ((), ())), preferred_element_type=F32)


def _split2(x):
    hi = x.astype(BF16)
    lo = (x - hi.astype(F32)).astype(BF16)
    return hi, lo


def _const_spec(shape):
    nd = len(shape)
    return pl.BlockSpec(shape, lambda *_: (0,) * nd, pipeline_mode=pl.Buffered(1))


def _ffn_kernel(x_ref, prew_ref, wg_ref, wu_ref, wd_ref, postw_ref, o_ref):
    x = x_ref[...]
    xn = _rms(x, prew_ref[...]).astype(BF16)
    g = _dot(xn, wg_ref[...])
    u = _dot(xn, wu_ref[...])
    a = (g * jax.nn.sigmoid(g) * u).astype(BF16)
    h = _dot(a, wd_ref[...])
    o_ref[...] = x + MACARON_WEIGHT * _rms(h, postw_ref[...])


def _ffn_block(x, pre_w, w_gate, w_up, w_down, post_w, tm=1024):
    n, d = x.shape
    dff = w_gate.shape[1]
    return pl.pallas_call(
        _ffn_kernel,
        grid=(n // tm,),
        in_specs=[
            pl.BlockSpec((tm, d), lambda i: (i, 0)),
            _const_spec((1, d)),
            _const_spec((d, dff)),
            _const_spec((d, dff)),
            _const_spec((dff, d)),
            _const_spec((1, d)),
        ],
        out_specs=pl.BlockSpec((tm, d), lambda i: (i, 0)),
        out_shape=jax.ShapeDtypeStruct((n, d), F32),
        compiler_params=pltpu.CompilerParams(
            dimension_semantics=("arbitrary",), vmem_limit_bytes=VMEM_LIMIT),
        name="ffn_block",
    )(x, pre_w, w_gate, w_up, w_down, post_w)


def _inproj_kernel(x_ref, prew_ref, wdn_ref, wba_ref, wk_ref, wq_ref, wv_ref,
                   dn_ref, ba_ref, k_ref, kmean_ref, qt_ref, vt_ref):
    bs = MB_BLOCK
    nb = qt_ref.shape[0]
    h = _rms(x_ref[...], prew_ref[...])
    h16 = h.astype(BF16)
    dn_ref[...] = _dot_nt(h16, wdn_ref[...])
    ba2 = _dot_nt(h16, wba_ref[...])
    ba_ref[...] = ba2[:, :LANES] + ba2[:, LANES:]
    k_ref[...] = _dot_nt(h16, wk_ref[0]).astype(BF16)
    hbar = jnp.concatenate([jnp.mean(h[i * bs:(i + 1) * bs], axis=0, keepdims=True)
                            for i in range(nb)]
                           + [jnp.zeros((SUBLANES - nb, h.shape[1]), F32)], axis=0)
    hb_hi, hb_lo = _split2(hbar)
    kmean = _dot_nt(hb_hi, wk_ref[0]) + _dot_nt(hb_lo, wk_ref[0]) + _dot_nt(hb_hi, wk_ref[1])
    kmean_ref[0] = kmean[:nb]
    qt = _dot_nt(wq_ref[...], h16)
    vt = _dot_nt(wv_ref[...], h16).astype(BF16)
    for i in range(nb):
        qt_ref[i] = qt[:, i * bs:(i + 1) * bs]
        vt_ref[i] = vt[:, i * bs:(i + 1) * bs]


def _in_proj(x, pre_w, w_dn, w_ba, w_k, w_qt, w_vt, tm=512):
    n, d = x.shape
    bs = MB_BLOCK
    return pl.pallas_call(
        _inproj_kernel,
        grid=(n // tm,),
        in_specs=[
            pl.BlockSpec((tm, d), lambda i: (i, 0)),
            _const_spec((1, d)),
            _const_spec(w_dn.shape),
            _const_spec(w_ba.shape),
            _const_spec(w_k.shape),
            _const_spec(w_qt.shape),
            _const_spec(w_vt.shape),
        ],
        out_specs=[
            pl.BlockSpec((tm, w_dn.shape[0]), lambda i: (i, 0)),
            pl.BlockSpec((tm, LANES), lambda i: (i, 0)),
            pl.BlockSpec((tm, MB_WIDTH), lambda i: (i, 0)),
            pl.BlockSpec((1, tm // bs, MB_WIDTH), lambda i: (i, 0, 0)),
            pl.BlockSpec((tm // bs, MB_WIDTH, bs), lambda i: (i, 0, 0)),
            pl.BlockSpec((tm // bs, MB_WIDTH, bs), lambda i: (i, 0, 0)),
        ],
        out_shape=[
            jax.ShapeDtypeStruct((n, w_dn.shape[0]), F32),
            jax.ShapeDtypeStruct((n, LANES), F32),
            jax.ShapeDtypeStruct((n, MB_WIDTH), BF16),
            jax.ShapeDtypeStruct((n // tm, tm // bs, MB_WIDTH), F32),
            jax.ShapeDtypeStruct((n // bs, MB_WIDTH, bs), F32),
            jax.ShapeDtypeStruct((n // bs, MB_WIDTH, bs), BF16),
        ],
        compiler_params=pltpu.CompilerParams(
            dimension_semantics=("arbitrary",), vmem_limit_bytes=VMEM_LIMIT),
        name="in_proj",
    )(x, pre_w, w_dn, w_ba, w_k, w_qt, w_vt)


def _dn_kernel(qkv_ref, ba_ref, convw_ref, alog_ref, dtb_ref, normw_ref, o_ref,
               xbuf_ref, state_ref):
    tt = DN_TILE
    c = DN_CHUNK
    dk = DN_HEAD_DIM
    n_batch = qkv_ref.shape[0]
    heads = range(DN_HEADS)
    pad = SUBLANES
    assert DN_CONV - 1 <= pad

    @pl.when(pl.program_id(0) == 0)
    def _():
        xbuf_ref[:, 0:pad, :] = jnp.zeros((n_batch, pad, 3 * DN_WIDTH), F32)
        state_ref[...] = jnp.zeros_like(state_ref)

    ri = lax.broadcasted_iota(jnp.int32, (tt, tt), 0)
    ci = lax.broadcasted_iota(jnp.int32, (tt, tt), 1)
    same_chunk = (ri // c) == (ci // c)
    tril = jnp.where(same_chunk & (ri >= ci), 1.0, 0.0).astype(BF16)
    ones_bd = jnp.where(same_chunk, 1.0, 0.0).astype(BF16)
    cc_row = lax.broadcasted_iota(jnp.int32, (c, tt), 0)
    cc_lane = lax.broadcasted_iota(jnp.int32, (c, tt), 1)
    cc_chunk = cc_lane // c
    cc_diag = cc_row == cc_lane % c
    cc_incl = cc_row >= cc_lane % c

    def batch_program(b):
        x = qkv_ref[b]
        xbuf_ref[b, pad:pad + tt, :] = x
        cw = convw_ref[...]
        y = x * cw[DN_CONV - 1:DN_CONV, :]
        for s in range(1, DN_CONV):
            y = y + xbuf_ref[b, pad - s:pad - s + tt, :] * cw[DN_CONV - 1 - s:DN_CONV - s, :]
        xbuf_ref[b, 0:pad, :] = x[tt - pad:tt, :]
        y = y * jax.nn.sigmoid(y)

        ba = ba_ref[b]
        beta_all = jax.nn.sigmoid(ba)
        g_all = -jnp.exp(alog_ref[...]) * jax.nn.softplus(ba + dtb_ref[...])

        g1 = g_all.astype(BF16)
        r1 = g_all - g1.astype(F32)
        g2 = r1.astype(BF16)
        g3 = (r1 - g2.astype(F32)).astype(BF16)
        gcs_all = _dot(tril, g1) + _dot(tril, g2) + _dot(tril, g3)
        gtot_all = _dot(ones_bd, g1) + _dot(ones_bd, g2) + _dot(ones_bd, g3)
        yield

        def compact(a):
            out = jnp.broadcast_to(a[:c], (c, tt))
            for ch in range(1, tt // c):
                out = jnp.where(cc_chunk == ch,
                                jnp.broadcast_to(a[ch * c:(ch + 1) * c], (c, tt)), out)
            return out

        def expand(a):
            return jnp.where(same_chunk, jnp.concatenate([a] * (tt // c), axis=0), 0.0)

        lmat_c, lmat16, attn16, rhs, qd, kd, gtot = [], [], [], [], [], [], []
        for h in heads:
            qr = y[:, h * dk:(h + 1) * dk]
            kr = y[:, DN_WIDTH + h * dk:DN_WIDTH + (h + 1) * dk]
            v = y[:, 2 * DN_WIDTH + h * dk:2 * DN_WIDTH + (h + 1) * dk]
            q = qr * lax.rsqrt(jnp.sum(qr * qr, axis=-1, keepdims=True) + NORM_EPS) * (dk ** -0.5)
            k = kr * lax.rsqrt(jnp.sum(kr * kr, axis=-1, keepdims=True) + NORM_EPS)
            beta = beta_all[:, h:h + 1]
            gcs = gcs_all[:, DN_HEADS + h:DN_HEADS + h + 1]
            gtot.append(gtot_all[:, DN_HEADS + h:DN_HEADS + h + 1])
            eg = jnp.exp(gcs)

            g_i = compact(gcs)
            g_j = jnp.sum(jnp.where(cc_diag, g_i, 0.0), axis=0, keepdims=True)
            decay = jnp.exp(jnp.where(cc_incl, g_i - g_j, NEG_BIG))

            kb = k * beta
            k16 = k.astype(BF16)
            lmat_c.append(jnp.where(cc_diag, 0.0, compact(_dot_nt(kb.astype(BF16), k16)) * decay))
            lmat16.append(expand(lmat_c[h]).astype(BF16))
            attn16.append(expand(compact(_dot_nt(q.astype(BF16), k16)) * decay).astype(BF16))
            rhs.append(jnp.concatenate([v * beta, kb * eg], axis=1).astype(BF16))
            qd.append(q * eg)
            kd.append((k * jnp.exp(gtot[h] - gcs)).astype(BF16))
            yield

        xc = [jnp.where(cc_diag, 1.0, -lmat_c[h]) for h in heads]
        mc = [_dot(lmat_c[h].astype(BF16), lmat16[h]) for h in heads]
        power = 2
        while power < c:
            m_bd = [expand(mc[h]).astype(BF16) for h in heads]
            if 2 * power < c:
                xm = [_dot(jnp.concatenate([xc[h], mc[h]], axis=0).astype(BF16), m_bd[h])
                      for h in heads]
                xc = [xc[h] + xm[h][:c] for h in heads]
                mc = [xm[h][c:] for h in heads]
            else:
                xc = [xc[h] + _dot(xc[h].astype(BF16), m_bd[h]) for h in heads]
            power *= 2
            yield

        uw16 = [_dot(expand(xc[h]).astype(BF16), rhs[h]).astype(BF16) for h in heads]
        au_aw = [_dot(attn16[h], uw16[h]) for h in heads]
        au = [au_aw[h][:, :dk] for h in heads]
        e16 = [(qd[h] - au_aw[h][:, dk:]).astype(BF16) for h in heads]
        yield

        s = [state_ref[b * DN_HEADS + h] for h in heads]
        outs = [[] for _ in heads]
        for ch in range(tt // c):
            lo, hi = ch * c, (ch + 1) * c
            bc = [_dot_tn(kd[h][lo:hi], uw16[h][lo:hi]) for h in heads]
            for h in heads:
                s16 = s[h].astype(BF16)
                outs[h].append(_dot(e16[h][lo:hi], s16) + au[h][lo:hi])
                s[h] = (s[h] * jnp.exp(gtot[h][lo:lo + 1, :]) + bc[h][:, :dk]
                        - _dot(bc[h][:, dk:].astype(BF16), s16))
            yield
        for h in heads:
            state_ref[b * DN_HEADS + h] = s[h]
            o = jnp.concatenate(outs[h], axis=0)
            o_ref[b, :, h * dk:(h + 1) * dk] = _rms(o, normw_ref[...])
        yield

    programs = [batch_program(b) for b in range(n_batch)]
    live = [True] * n_batch
    wave = 0
    while any(live):
        for b in range(n_batch):
            if live[b] and wave >= DN_STAGE_LAG * b:
                live[b] = next(programs[b], "done") != "done"
        wave += 1


def _deltanet(qkv, ba, conv_w, alog_row, dtb_row, norm_w):
    bsz, t_len, width = qkv.shape
    tt = DN_TILE
    return pl.pallas_call(
        _dn_kernel,
        grid=(t_len // tt,),
        in_specs=[
            pl.BlockSpec((bsz, tt, width), lambda t: (0, t, 0)),
            pl.BlockSpec((bsz, tt, LANES), lambda t: (0, t, 0)),
            _const_spec(conv_w.shape),
            _const_spec((1, LANES)),
            _const_spec((1, LANES)),
            _const_spec((1, DN_HEAD_DIM)),
        ],
        out_specs=pl.BlockSpec((bsz, tt, DN_WIDTH), lambda t: (0, t, 0)),
        out_shape=jax.ShapeDtypeStruct((bsz, t_len, DN_WIDTH), F32),
        scratch_shapes=[
            pltpu.VMEM((bsz, SUBLANES + tt, width), F32),
            pltpu.VMEM((bsz * DN_HEADS, DN_HEAD_DIM, DN_HEAD_DIM), F32),
        ],
        compiler_params=pltpu.CompilerParams(
            dimension_semantics=("arbitrary",), vmem_limit_bytes=VMEM_LIMIT),
        name="deltanet",
    )(qkv, ba, conv_w, alog_row, dtb_row, norm_w)


MB_SUPER = 1
MB_GANG = 2
MB_UNROLL = 4
N_PIECES = 4
MB_AUX_KOFF = 2
MB_AUX_KSTART = MB_AUX_KOFF + N_PIECES
MB_AUX_BIAS_END = MB_AUX_KSTART + N_PIECES
MB_AUX_MASK = 16
assert MB_AUX_BIAS_END <= MB_AUX_MASK
MB_SUM_ROWS = 16
ALIBI_STEP = int(ALIBI_MAX_BIAS) // MB_HEADS
assert ALIBI_STEP * MB_HEADS == ALIBI_MAX_BIAS
LOG2E = math.log2(math.e)
LOG2E_PIECES = (1.4453125, -0.00262451171875, 7.063150405883789e-06, -1.05355866253376e-08)
assert len(LOG2E_PIECES) == N_PIECES


def _moba_kernel(qt_ref, k_ref, kmean_ref, vt_ref, o_ref, kaug_ref, sa_ref, sb_ref, *, n_blk):
    bs = MB_BLOCK
    hd = MB_HEAD_DIM
    sup = MB_SUPER * bs
    nbp = -(-n_blk // SUBLANES) * SUBLANES
    gang = pl.program_id(1)
    own = pl.program_id(2)
    pairs = range(MB_GANG)
    lane = lax.broadcasted_iota(jnp.int32, (bs, LANES), 1)
    row = lax.broadcasted_iota(jnp.int32, (bs, LANES), 0)

    @pl.when(own == 0)
    def _():
        def build(j, carry):
            off = pl.multiple_of(j * bs, bs)
            kstart = jnp.full((bs, LANES), j * bs, jnp.int32).astype(F32)
            aux = jnp.where(lane < MB_AUX_KOFF, 1.0,
                            jnp.where(lane < MB_AUX_KSTART, row.astype(F32),
                                      jnp.where(lane < MB_AUX_BIAS_END, kstart,
                                                jnp.where(lane == MB_AUX_MASK + j, 1.0, 0.0))))
            for pp in pairs:
                kaug_ref[pp, pl.ds(off, bs), 0:LANES] = k_ref[0, pl.ds(off, bs),
                                                              pp * LANES:(pp + 1) * LANES]
                kaug_ref[pp, pl.ds(off, bs), LANES:2 * LANES] = aux.astype(BF16)
            return carry

        lax.fori_loop(0, n_blk, build, 0)

    chan = lax.broadcasted_iota(jnp.int32, (LANES, bs), 0)
    blk = lax.broadcasted_iota(jnp.int32, (nbp, bs), 0)
    blk_f = blk.astype(F32)
    aux_row = lax.broadcasted_iota(jnp.int32, (MB_AUX_MASK, bs), 0)
    qpos = (lax.broadcasted_iota(jnp.int32, (MB_AUX_MASK, bs), 1) + own * bs).astype(F32)
    aux_pad = jnp.zeros((LANES - MB_AUX_MASK - nbp, bs), F32)

    heads = [(pp, hh) for pp in pairs for hh in range(2)]
    qth, gate = [], []
    for pp, hh in heads:
        qt = qt_ref[0, pp * LANES:(pp + 1) * LANES, :]
        km_hi, km_lo = _split2(kmean_ref[0, :, pp * LANES:(pp + 1) * LANES])
        qth.append(jnp.where((chan >= hh * hd) & (chan < (hh + 1) * hd), qt, 0.0))
        q_hi, q_lo = _split2(qth[-1])
        g = _dot(km_hi, q_hi) + _dot(km_hi, q_lo) + _dot(km_lo, q_hi)
        gate.append(jnp.where(blk < own, g, -jnp.inf))

    piece_id = (aux_row - MB_AUX_KOFF) & (N_PIECES - 1)
    piece = jnp.where(piece_id == 0, LOG2E_PIECES[0],
                      jnp.where(piece_id == 1, LOG2E_PIECES[1],
                                jnp.where(piece_id == 2, LOG2E_PIECES[2], LOG2E_PIECES[3])))
    q_rows = []
    for n, (pp, hh) in enumerate(heads):
        head = 2 * (MB_GANG * gang + pp) + hh
        slope_bits = (127 - ALIBI_STEP * (head + 1)) << 23
        slope = lax.bitcast_convert_type(jnp.full((MB_AUX_MASK, bs), slope_bits, jnp.int32), F32)
        qconst = -(slope * LOG2E) * qpos
        qconst_hi = qconst.astype(BF16).astype(F32)
        bias_rows = jnp.where(aux_row == 0, qconst_hi,
                              jnp.where(aux_row == 1, qconst - qconst_hi,
                                        jnp.where(aux_row < MB_AUX_BIAS_END, slope * piece, 0.0)))
        q_rows.append(jnp.concatenate([qth[n] * (hd ** -0.5 * LOG2E), bias_rows], axis=0))

    def query_operand(mask_rows):
        ops = [jnp.concatenate([q_rows[n], mask_rows[n], aux_pad], axis=0).astype(BF16)
               for n in range(len(heads))]
        return [jnp.concatenate(ops[2 * pp:2 * pp + 2], axis=1) for pp in pairs]

    def keys(pp, i):
        return kaug_ref[pp, pl.ds(pl.multiple_of(i * sup, sup), sup), :]

    def values_t(i, pp, hh):
        lo = pp * LANES + hh * hd
        return jnp.concatenate([vt_ref[i * MB_SUPER + u, lo:lo + hd, :]
                                for u in range(MB_SUPER)], axis=1)

    def produce(s_ref, pp, g, q_op, mask=None):
        s2 = _dot(keys(pp, g), q_op[pp])
        if mask is not None:
            s2 = jnp.where(mask, NEG_BIG, s2)
        s_ref[pp] = s2
        return jnp.max(s2, axis=0, keepdims=True)

    n_grp = n_blk // MB_SUPER
    grp = own // MB_SUPER

    assert MB_SUPER == 1
    rel = (lax.broadcasted_iota(jnp.int32, (sup, 2 * bs), 0) - (own - grp * MB_SUPER) * bs)
    qi = lax.broadcasted_iota(jnp.int32, (sup, 2 * bs), 1) & (bs - 1)
    future = (rel > qi) & (rel < bs)
    q_own = query_operand([jnp.zeros((nbp, bs), F32)] * len(heads))
    smax_a0 = [produce(sa_ref, pp, grp, q_own, future) for pp in pairs]

    sel = [jnp.zeros((nbp, bs), F32) for _ in heads]
    for _ in range(MB_TOPK):
        mx = [jnp.max(g, axis=0, keepdims=True) for g in gate]
        first = [jnp.min(jnp.where(g == m, blk_f, float(nbp)), axis=0, keepdims=True)
                 for g, m in zip(gate, mx)]
        hit = [blk_f == f for f in first]
        sel = [jnp.where(h, 1.0, s) for h, s in zip(hit, sel)]
        gate = [jnp.where(h, -jnp.inf, g) for h, g in zip(hit, gate)]
    keep = [jnp.where(blk < own, s, jnp.where(blk == own, 1.0, 0.0)) for s in sel]
    qaug2 = query_operand([jnp.where(k > 0.5, 0.0, NEG_BIG) for k in keep])

    def group_at(t):
        g = jnp.where(t == 0, grp, jnp.where(t > grp, grp + 1, t - 1))
        return jnp.minimum(g, n_grp - 1)

    ones_rows = jnp.ones((MB_SUM_ROWS, sup), BF16)

    def softmax_step(s_ref, pp, smax, g, carry):
        hs = range(2)
        m_i = [carry[2 * hh] for hh in hs]
        m_new = [jnp.maximum(m_i[hh], smax[:, hh * bs:(hh + 1) * bs]) for hh in hs]
        alpha = [jnp.exp2(m_i[hh] - m_new[hh]) for hh in hs]
        pexp = [jnp.exp2((s_ref[pp, :, hh * bs:(hh + 1) * bs] - m_new[hh]).astype(BF16))
                for hh in hs]
        acc_new = [carry[2 * hh + 1] * alpha[hh]
                   + _dot(jnp.concatenate([values_t(g, pp, hh), ones_rows], axis=0), pexp[hh])
                   for hh in hs]
        return (m_new[0], acc_new[0], m_new[1], acc_new[1])

    def steps(u, carry, unroll, t0):
        t = t0 + unroll * u
        smax_a = list(carry[:MB_GANG])
        stats = [carry[MB_GANG + 4 * pp:MB_GANG + 4 * pp + 4] for pp in pairs]
        for v in range(0, unroll, 2):
            smax_b = [produce(sb_ref, pp, group_at(t + v + 1), qaug2) for pp in pairs]
            stats = [softmax_step(sa_ref, pp, smax_a[pp], group_at(t + v), stats[pp])
                     for pp in pairs]
            smax_a = [produce(sa_ref, pp, group_at(t + v + 2), qaug2) for pp in pairs]
            stats = [softmax_step(sb_ref, pp, smax_b[pp], group_at(t + v + 1), stats[pp])
                     for pp in pairs]
        out = tuple(smax_a)
        for pp in pairs:
            out += tuple(stats[pp])
        return out

    stat0 = jnp.full((1, bs), -jnp.inf, F32)
    acc0 = jnp.zeros((hd + MB_SUM_ROWS, bs), F32)
    n_full = (grp + 1) // MB_UNROLL
    n_tail = (grp + 2 - n_full * MB_UNROLL) // 2
    fin = lax.fori_loop(0, n_full, functools.partial(steps, unroll=MB_UNROLL, t0=0),
                        tuple(smax_a0) + (stat0, acc0) * len(heads))
    fin = lax.fori_loop(0, n_tail, functools.partial(steps, unroll=2, t0=n_full * MB_UNROLL), fin)
    accs = [fin[MB_GANG + 2 * n + 1] for n in range(len(heads))]
    out_t = jnp.concatenate([a[:hd] / a[hd:hd + 1] for a in accs], axis=0)
    o_ref[0] = out_t


def _moba(qt, k, kmean, vt):
    bsz, t_len, _ = k.shape
    bs = MB_BLOCK
    n_blk = t_len // bs
    assert MB_UNROLL % 2 == 0 and n_blk % (MB_UNROLL * MB_SUPER) == 0
    assert MB_AUX_MASK + n_blk <= LANES
    assert n_blk % SUBLANES == 0 and MB_PAIRS % MB_GANG == 0
    gw = MB_GANG * LANES
    return pl.pallas_call(
        functools.partial(_moba_kernel, n_blk=n_blk),
        grid=(bsz, MB_PAIRS // MB_GANG, n_blk),
        in_specs=[
            pl.BlockSpec((1, gw, bs), lambda b, p, i: (b * n_blk + i, p, 0)),
            pl.BlockSpec((1, t_len, gw), lambda b, p, i: (b, 0, p)),
            pl.BlockSpec((1, n_blk, gw), lambda b, p, i: (b, 0, p)),
            pl.BlockSpec((n_blk, gw, bs), lambda b, p, i: (b, p, 0)),
        ],
        out_specs=pl.BlockSpec((1, gw, bs), lambda b, p, i: (b * n_blk + i, p, 0)),
        out_shape=jax.ShapeDtypeStruct((bsz * n_blk, MB_WIDTH, bs), F32),
        scratch_shapes=[
            pltpu.VMEM((MB_GANG, t_len, 2 * LANES), BF16),
            pltpu.VMEM((MB_GANG, MB_SUPER * bs, 2 * bs), F32),
            pltpu.VMEM((MB_GANG, MB_SUPER * bs, 2 * bs), F32),
        ],
        compiler_params=pltpu.CompilerParams(
            dimension_semantics=("arbitrary", "arbitrary", "arbitrary"),
            vmem_limit_bytes=VMEM_LIMIT),
        name="moba",
    )(qt, k, kmean, vt)


def _mixout_kernel(x_ref, odn_ref, omb_ref, prew_ref, wz_ref, wgd_ref, wgm_ref,
                   wbd_ref, wbm_ref, wo_ref, postw_ref, o_ref):
    x = x_ref[...]
    h = _rms(x, prew_ref[...]).astype(BF16)
    z = _dot_nt(h, wz_ref[...])
    gate_dn = jax.nn.sigmoid(_dot_nt(h, wgd_ref[...]))
    gate_mb = jax.nn.sigmoid(_dot_nt(h, wgm_ref[...]))
    o_dn = odn_ref[...] * (z * jax.nn.sigmoid(z))
    y_dn = _dot(o_dn.astype(BF16), wbd_ref[...])
    y_mb = jnp.concatenate([_dot_tn(omb_ref[i].astype(BF16), wbm_ref[...])
                            for i in range(omb_ref.shape[0])], axis=0)
    merged = gate_dn * y_dn + gate_mb * y_mb
    y = _dot(merged.astype(BF16), wo_ref[...])
    o_ref[...] = x + _rms(y, postw_ref[...])


def _mix_out(x, o_dn, o_mb, pre_w, w_z, w_gd, w_gm, w_bd, w_bm, w_o, post_w, tm=512):
    n, d = x.shape
    return pl.pallas_call(
        _mixout_kernel,
        grid=(n // tm,),
        in_specs=[
            pl.BlockSpec((tm, d), lambda i: (i, 0)),
            pl.BlockSpec((tm, o_dn.shape[1]), lambda i: (i, 0)),
            pl.BlockSpec((tm // MB_BLOCK,) + o_mb.shape[1:], lambda i: (i, 0, 0)),
            _const_spec((1, d)),
            _const_spec(w_z.shape),
            _const_spec(w_gd.shape),
            _const_spec(w_gm.shape),
            _const_spec(w_bd.shape),
            _const_spec(w_bm.shape),
            _const_spec(w_o.shape),
            _const_spec((1, d)),
        ],
        out_specs=pl.BlockSpec((tm, d), lambda i: (i, 0)),
        out_shape=jax.ShapeDtypeStruct((n, d), F32),
        compiler_params=pltpu.CompilerParams(
            dimension_semantics=("arbitrary",), vmem_limit_bytes=VMEM_LIMIT),
        name="mix_out",
    )(x, o_dn, o_mb, pre_w, w_z, w_gd, w_gm, w_bd, w_bm, w_o, post_w)


def _layer(x, ffn1_pre_w, ffn1_w_gate, ffn1_w_up, ffn1_w_down, ffn1_post_w,
           mix_pre_w, w_in, dn_conv_w, dn_a_log, dn_dt_bias, dn_norm_w,
           w_branch_dn, w_branch_mb, w_out, mix_post_w,
           ffn2_pre_w, ffn2_w_gate, ffn2_w_up, ffn2_w_down, ffn2_post_w):
    bsz, t_len, d = x.shape
    n = bsz * t_len
    row = lambda w: w.reshape(1, -1).astype(F32)
    b16 = lambda w: w.astype(BF16)

    x = x.reshape(n, d)
    x = _ffn_block(x, row(ffn1_pre_w), b16(ffn1_w_gate), b16(ffn1_w_up), b16(ffn1_w_down),
                   row(ffn1_post_w))

    w_in_t = w_in.T
    o = 0
    w_dn = w_in_t[o:o + 3 * DN_WIDTH]; o += 3 * DN_WIDTH
    w_z = w_in_t[o:o + DN_WIDTH]; o += DN_WIDTH
    w_ba = w_in_t[o:o + 2 * DN_HEADS]; o += 2 * DN_HEADS
    w_q = w_in_t[o:o + MB_WIDTH]; o += MB_WIDTH
    w_k = w_in_t[o:o + MB_WIDTH]; o += MB_WIDTH
    w_v = w_in_t[o:o + MB_WIDTH]; o += MB_WIDTH
    w_gd = w_in_t[o:o + d]; o += d
    w_gm = w_in_t[o:o + d]; o += d
    w_ba = jnp.pad(w_ba, ((0, LANES - 2 * DN_HEADS), (0, 0)))

    def hi_lo(w):
        hi = w.astype(BF16)
        return jnp.stack([hi, (w - hi.astype(F32)).astype(BF16)])

    dn_qkv, ba, mb_k, mb_kmean, mb_qt, mb_vt = _in_proj(
        x, row(mix_pre_w), b16(w_dn), jnp.concatenate(list(hi_lo(w_ba)), axis=0), hi_lo(w_k),
        b16(w_q), b16(w_v))

    pad_heads = lambda p: jnp.pad(p.astype(F32), (DN_HEADS, LANES - 2 * DN_HEADS)).reshape(1, LANES)
    o_dn = _deltanet(dn_qkv.reshape(bsz, t_len, -1), ba.reshape(bsz, t_len, LANES),
                     dn_conv_w.astype(F32), pad_heads(dn_a_log), pad_heads(dn_dt_bias),
                     row(dn_norm_w))

    o_mb = _moba(mb_qt, mb_k.reshape(bsz, t_len, MB_WIDTH),
                 mb_kmean.reshape(bsz, t_len // MB_BLOCK, MB_WIDTH), mb_vt)

    x = _mix_out(x, o_dn.reshape(n, DN_WIDTH), o_mb, row(mix_pre_w),
                 b16(w_z), b16(w_gd), b16(w_gm), b16(w_branch_dn), b16(w_branch_mb), b16(w_out),
                 row(mix_post_w))

    x = _ffn_block(x, row(ffn2_pre_w), b16(ffn2_w_gate), b16(ffn2_w_up), b16(ffn2_w_down),
                   row(ffn2_post_w))
    return x.reshape(bsz, t_len, d)


def kernel(x, ffn1_pre_w, ffn1_w_gate, ffn1_w_up, ffn1_w_down, ffn1_post_w, mix_pre_w, w_in, dn_conv_w, dn_a_log, dn_dt_bias, dn_norm_w, w_branch_dn, w_branch_mb, w_out, mix_post_w, ffn2_pre_w, ffn2_w_gate, ffn2_w_up, ffn2_w_down, ffn2_post_w):
    depth = w_in.shape[0]
    for l in range(depth):
        x = _layer(x, ffn1_pre_w[l], ffn1_w_gate[l], ffn1_w_up[l], ffn1_w_down[l], ffn1_post_w[l],
                   mix_pre_w[l], w_in[l], dn_conv_w[l], dn_a_log[l], dn_dt_bias[l], dn_norm_w[l],
                   w_branch_dn[l], w_branch_mb[l], w_out[l], mix_post_w[l],
                   ffn2_pre_w[l], ffn2_w_gate[l], ffn2_w_up[l], ffn2_w_down[l], ffn2_post_w[l])
    return x
```

```python
import functools
import math

import jax
import jax.numpy as jnp
from jax import lax
from jax.experimental import pallas as pl
from jax.experimental.pallas import tpu as pltpu

F32 = jnp.float32
BF16 = jnp.bfloat16

NORM_EPS = 1e-6
MACARON_WEIGHT = 0.5

DN_HEADS = 4
DN_HEAD_DIM = 128
DN_WIDTH = DN_HEADS * DN_HEAD_DIM
DN_CONV = 4
DN_CHUNK = 64
DN_TILE = 256
DN_STAGE_LAG = 1 + DN_HEADS + (DN_CHUNK.bit_length() - 2)

MB_HEADS = 8
MB_HEAD_DIM = 64
MB_WIDTH = MB_HEADS * MB_HEAD_DIM
MB_BLOCK = 256
MB_TOPK = 3
ALIBI_MAX_BIAS = 8.0
LANES = 128
SUBLANES = 8
MB_PAIRS = MB_WIDTH // LANES
NEG_BIG = -1e30

VMEM_LIMIT = 56 * 1024 * 1024


def _rms(x, w):
    ms = jnp.mean(x * x, axis=-1, keepdims=True)
    return x * lax.rsqrt(ms + NORM_EPS) * w


def _dot(a, b):
    return jnp.dot(a, b, preferred_element_type=F32)


def _dot_nt(a, b):
    return lax.dot_general(a, b, (((1,), (1,)), ((), ())), preferred_element_type=F32)


def _dot_tn(a, b):
    return lax.dot_general(a, b, (((0,), (0,)), ((), ())), preferred_element_type=F32)


def _split2(x):
    hi = x.astype(BF16)
    lo = (x - hi.astype(F32)).astype(BF16)
    return hi, lo


def _const_spec(shape):
    nd = len(shape)
    return pl.BlockSpec(shape, lambda *_: (0,) * nd, pipeline_mode=pl.Buffered(1))


def _ffn_kernel(x_ref, prew_ref, wg_ref, wu_ref, wd_ref, postw_ref, o_ref):
    x = x_ref[...]
    xn = _rms(x, prew_ref[...]).astype(BF16)
    g = _dot(xn, wg_ref[...])
    u = _dot(xn, wu_ref[...])
    a = (g * jax.nn.sigmoid(g) * u).astype(BF16)
    h = _dot(a, wd_ref[...])
    o_ref[...] = x + MACARON_WEIGHT * _rms(h, postw_ref[...])


def _ffn_block(x, pre_w, w_gate, w_up, w_down, post_w, tm=512):
    n, d = x.shape
    dff = w_gate.shape[1]
    return pl.pallas_call(
        _ffn_kernel,
        grid=(n // tm,),
        in_specs=[
            pl.BlockSpec((tm, d), lambda i: (i, 0)),
            _const_spec((1, d)),
            _const_spec((d, dff)),
            _const_spec((d, dff)),
            _const_spec((dff, d)),
            _const_spec((1, d)),
        ],
        out_specs=pl.BlockSpec((tm, d), lambda i: (i, 0)),
        out_shape=jax.ShapeDtypeStruct((n, d), F32),
        compiler_params=pltpu.CompilerParams(
            dimension_semantics=("arbitrary",), vmem_limit_bytes=VMEM_LIMIT),
        name="ffn_block",
    )(x, pre_w, w_gate, w_up, w_down, post_w)


def _inproj_kernel(x_ref, prew_ref, wdn_ref, wba_ref, wk_ref, wq_ref, wv_ref,
                   dn_ref, ba_ref, k_ref, kmean_ref, qt_ref, vt_ref):
    bs = MB_BLOCK
    nb = qt_ref.shape[0]
    h = _rms(x_ref[...], prew_ref[...])
    h16 = h.astype(BF16)
    dn_ref[...] = _dot_nt(h16, wdn_ref[...])
    ba2 = _dot_nt(h16, wba_ref[...])
    ba_ref[...] = ba2[:, :LANES] + ba2[:, LANES:]
    k_ref[...] = _dot_nt(h16, wk_ref[0]).astype(BF16)
    hbar = jnp.concatenate([jnp.mean(h[i * bs:(i + 1) * bs], axis=0, keepdims=True)
                            for i in range(nb)]
                           + [jnp.zeros((SUBLANES - nb, h.shape[1]), F32)], axis=0)
    hb_hi, hb_lo = _split2(hbar)
    kmean = _dot_nt(hb_hi, wk_ref[0]) + _dot_nt(hb_lo, wk_ref[0]) + _dot_nt(hb_hi, wk_ref[1])
    kmean_ref[0] = kmean[:nb]
    qt = _dot_nt(wq_ref[...], h16)
    vt = _dot_nt(wv_ref[...], h16).astype(BF16)
    for i in range(nb):
        qt_ref[i] = qt[:, i * bs:(i + 1) * bs]
        vt_ref[i] = vt[:, i * bs:(i + 1) * bs]


def _in_proj(x, pre_w, w_dn, w_ba, w_k, w_qt, w_vt, tm=512):
    n, d = x.shape
    bs = MB_BLOCK
    return pl.pallas_call(
        _inproj_kernel,
        grid=(n // tm,),
        in_specs=[
            pl.BlockSpec((tm, d), lambda i: (i, 0)),
            _const_spec((1, d)),
            _const_spec(w_dn.shape),
            _const_spec(w_ba.shape),
            _const_spec(w_k.shape),
            _const_spec(w_qt.shape),
            _const_spec(w_vt.shape),
        ],
        out_specs=[
            pl.BlockSpec((tm, w_dn.shape[0]), lambda i: (i, 0)),
            pl.BlockSpec((tm, LANES), lambda i: (i, 0)),
            pl.BlockSpec((tm, MB_WIDTH), lambda i: (i, 0)),
            pl.BlockSpec((1, tm // bs, MB_WIDTH), lambda i: (i, 0, 0)),
            pl.BlockSpec((tm // bs, MB_WIDTH, bs), lambda i: (i, 0, 0)),
            pl.BlockSpec((tm // bs, MB_WIDTH, bs), lambda i: (i, 0, 0)),
        ],
        out_shape=[
            jax.ShapeDtypeStruct((n, w_dn.shape[0]), F32),
            jax.ShapeDtypeStruct((n, LANES), F32),
            jax.ShapeDtypeStruct((n, MB_WIDTH), BF16),
            jax.ShapeDtypeStruct((n // tm, tm // bs, MB_WIDTH), F32),
            jax.ShapeDtypeStruct((n // bs, MB_WIDTH, bs), F32),
            jax.ShapeDtypeStruct((n // bs, MB_WIDTH, bs), BF16),
        ],
        compiler_params=pltpu.CompilerParams(
            dimension_semantics=("arbitrary",), vmem_limit_bytes=VMEM_LIMIT),
        name="in_proj",
    )(x, pre_w, w_dn, w_ba, w_k, w_qt, w_vt)


def _dn_kernel(qkv_ref, ba_ref, convw_ref, alog_ref, dtb_ref, normw_ref, o_ref,
               xbuf_ref, state_ref):
    tt = DN_TILE
    c = DN_CHUNK
    dk = DN_HEAD_DIM
    n_batch = qkv_ref.shape[0]
    heads = range(DN_HEADS)
    pad = SUBLANES
    assert DN_CONV - 1 <= pad

    @pl.when(pl.program_id(0) == 0)
    def _():
        xbuf_ref[:, 0:pad, :] = jnp.zeros((n_batch, pad, 3 * DN_WIDTH), F32)
        state_ref[...] = jnp.zeros_like(state_ref)

    ri = lax.broadcasted_iota(jnp.int32, (tt, tt), 0)
    ci = lax.broadcasted_iota(jnp.int32, (tt, tt), 1)
    same_chunk = (ri // c) == (ci // c)
    tril = jnp.where(same_chunk & (ri >= ci), 1.0, 0.0).astype(BF16)
    ones_bd = jnp.where(same_chunk, 1.0, 0.0).astype(BF16)
    cc_row = lax.broadcasted_iota(jnp.int32, (c, tt), 0)
    cc_lane = lax.broadcasted_iota(jnp.int32, (c, tt), 1)
    cc_chunk = cc_lane // c
    cc_diag = cc_row == cc_lane % c
    cc_incl = cc_row >= cc_lane % c

    def batch_program(b):
        x = qkv_ref[b]
        xbuf_ref[b, pad:pad + tt, :] = x
        cw = convw_ref[...]
        y = x * cw[DN_CONV - 1:DN_CONV, :]
        for s in range(1, DN_CONV):
            y = y + xbuf_ref[b, pad - s:pad - s + tt, :] * cw[DN_CONV - 1 - s:DN_CONV - s, :]
        xbuf_ref[b, 0:pad, :] = x[tt - pad:tt, :]
        y = y * jax.nn.sigmoid(y)

        ba = ba_ref[b]
        beta_all = jax.nn.sigmoid(ba)
        g_all = -jnp.exp(alog_ref[...]) * jax.nn.softplus(ba + dtb_ref[...])

        g1 = g_all.astype(BF16)
        r1 = g_all - g1.astype(F32)
        g2 = r1.astype(BF16)
        g3 = (r1 - g2.astype(F32)).astype(BF16)
        gcs_all = _dot(tril, g1) + _dot(tril, g2) + _dot(tril, g3)
        gtot_all = _dot(ones_bd, g1) + _dot(ones_bd, g2) + _dot(ones_bd, g3)
        yield

        def compact(a):
            out = jnp.broadcast_to(a[:c], (c, tt))
            for ch in range(1, tt // c):
                out = jnp.where(cc_chunk == ch,
                                jnp.broadcast_to(a[ch * c:(ch + 1) * c], (c, tt)), out)
            return out

        def expand(a):
            return jnp.where(same_chunk, jnp.concatenate([a] * (tt // c), axis=0), 0.0)

        lmat_c, lmat16, attn16, rhs, qd, kd, gtot = [], [], [], [], [], [], []
        for h in heads:
            qr = y[:, h * dk:(h + 1) * dk]
            kr = y[:, DN_WIDTH + h * dk:DN_WIDTH + (h + 1) * dk]
            v = y[:, 2 * DN_WIDTH + h * dk:2 * DN_WIDTH + (h + 1) * dk]
            q = qr * lax.rsqrt(jnp.sum(qr * qr, axis=-1, keepdims=True) + NORM_EPS) * (dk ** -0.5)
            k = kr * lax.rsqrt(jnp.sum(kr * kr, axis=-1, keepdims=True) + NORM_EPS)
            beta = beta_all[:, h:h + 1]
            gcs = gcs_all[:, DN_HEADS + h:DN_HEADS + h + 1]
            gtot.append(gtot_all[:, DN_HEADS + h:DN_HEADS + h + 1])
            eg = jnp.exp(gcs)

            g_i = compact(gcs)
            g_j = jnp.sum(jnp.where(cc_diag, g_i, 0.0), axis=0, keepdims=True)
            decay = jnp.exp(jnp.where(cc_incl, g_i - g_j, NEG_BIG))

            kb = k * beta
            k16 = k.astype(BF16)
            lmat_c.append(jnp.where(cc_diag, 0.0, compact(_dot_nt(kb.astype(BF16), k16)) * decay))
            lmat16.append(expand(lmat_c[h]).astype(BF16))
            attn16.append(expand(compact(_dot_nt(q.astype(BF16), k16)) * decay).astype(BF16))
            rhs.append(jnp.concatenate([v * beta, kb * eg], axis=1).astype(BF16))
            qd.append(q * eg)
            kd.append((k * jnp.exp(gtot[h] - gcs)).astype(BF16))
            yield

        xc = [jnp.where(cc_diag, 1.0, -lmat_c[h]) for h in heads]
        mc = [_dot(lmat_c[h].astype(BF16), lmat16[h]) for h in heads]
        power = 2
        while power < c:
            m_bd = [expand(mc[h]).astype(BF16) for h in heads]
            if 2 * power < c:
                xm = [_dot(jnp.concatenate([xc[h], mc[h]], axis=0).astype(BF16), m_bd[h])
                      for h in heads]
                xc = [xc[h] + xm[h][:c] for h in heads]
                mc = [xm[h][c:] for h in heads]
            else:
                xc = [xc[h] + _dot(xc[h].astype(BF16), m_bd[h]) for h in heads]
            power *= 2
            yield

        uw16 = [_dot(expand(xc[h]).astype(BF16), rhs[h]).astype(BF16) for h in heads]
        au_aw = [_dot(attn16[h], uw16[h]) for h in heads]
        au = [au_aw[h][:, :dk] for h in heads]
        e16 = [(qd[h] - au_aw[h][:, dk:]).astype(BF16) for h in heads]
        yield

        s = [state_ref[b * DN_HEADS + h] for h in heads]
        outs = [[] for _ in heads]
        for ch in range(tt // c):
            lo, hi = ch * c, (ch + 1) * c
            bc = [_dot_tn(kd[h][lo:hi], uw16[h][lo:hi]) for h in heads]
            for h in heads:
                s16 = s[h].astype(BF16)
                outs[h].append(_dot(e16[h][lo:hi], s16) + au[h][lo:hi])
                s[h] = (s[h] * jnp.exp(gtot[h][lo:lo + 1, :]) + bc[h][:, :dk]
                        - _dot(bc[h][:, dk:].astype(BF16), s16))
            yield
        for h in heads:
            state_ref[b * DN_HEADS + h] = s[h]
            o = jnp.concatenate(outs[h], axis=0)
            o_ref[b, :, h * dk:(h + 1) * dk] = _rms(o, normw_ref[...])
        yield

    programs = [batch_program(b) for b in range(n_batch)]
    live = [True] * n_batch
    wave = 0
    while any(live):
        for b in range(n_batch):
            if live[b] and wave >= DN_STAGE_LAG * b:
                live[b] = next(programs[b], "done") != "done"
        wave += 1


def _deltanet(qkv, ba, conv_w, alog_row, dtb_row, norm_w):
    bsz, t_len, width = qkv.shape
    tt = DN_TILE
    return pl.pallas_call(
        _dn_kernel,
        grid=(t_len // tt,),
        in_specs=[
            pl.BlockSpec((bsz, tt, width), lambda t: (0, t, 0)),
            pl.BlockSpec((bsz, tt, LANES), lambda t: (0, t, 0)),
            _const_spec(conv_w.shape),
            _const_spec((1, LANES)),
            _const_spec((1, LANES)),
            _const_spec((1, DN_HEAD_DIM)),
        ],
        out_specs=pl.BlockSpec((bsz, tt, DN_WIDTH), lambda t: (0, t, 0)),
        out_shape=jax.ShapeDtypeStruct((bsz, t_len, DN_WIDTH), F32),
        scratch_shapes=[
            pltpu.VMEM((bsz, SUBLANES + tt, width), F32),
            pltpu.VMEM((bsz * DN_HEADS, DN_HEAD_DIM, DN_HEAD_DIM), F32),
        ],
        compiler_params=pltpu.CompilerParams(
            dimension_semantics=("arbitrary",), vmem_limit_bytes=VMEM_LIMIT),
        name="deltanet",
    )(qkv, ba, conv_w, alog_row, dtb_row, norm_w)


MB_SUPER = 1
MB_GANG = 2
MB_UNROLL = 4
N_PIECES = 4
MB_AUX_KOFF = 2
MB_AUX_KSTART = MB_AUX_KOFF + N_PIECES
MB_AUX_BIAS_END = MB_AUX_KSTART + N_PIECES
MB_AUX_MASK = 16
assert MB_AUX_BIAS_END <= MB_AUX_MASK
MB_SUM_ROWS = 16
ALIBI_STEP = int(ALIBI_MAX_BIAS) // MB_HEADS
assert ALIBI_STEP * MB_HEADS == ALIBI_MAX_BIAS
LOG2E = math.log2(math.e)
LOG2E_PIECES = (1.4453125, -0.00262451171875, 7.063150405883789e-06, -1.05355866253376e-08)
assert len(LOG2E_PIECES) == N_PIECES


def _moba_kernel(qt_ref, k_ref, kmean_ref, vt_ref, o_ref, kaug_ref, sa_ref, sb_ref, *, n_blk):
    bs = MB_BLOCK
    hd = MB_HEAD_DIM
    sup = MB_SUPER * bs
    nbp = -(-n_blk // SUBLANES) * SUBLANES
    gang = pl.program_id(1)
    own = pl.program_id(2)
    pairs = range(MB_GANG)
    lane = lax.broadcasted_iota(jnp.int32, (bs, LANES), 1)
    row = lax.broadcasted_iota(jnp.int32, (bs, LANES), 0)

    @pl.when(own == 0)
    def _():
        def build(j, carry):
            off = pl.multiple_of(j * bs, bs)
            kstart = jnp.full((bs, LANES), j * bs, jnp.int32).astype(F32)
            aux = jnp.where(lane < MB_AUX_KOFF, 1.0,
                            jnp.where(lane < MB_AUX_KSTART, row.astype(F32),
                                      jnp.where(lane < MB_AUX_BIAS_END, kstart,
                                                jnp.where(lane == MB_AUX_MASK + j, 1.0, 0.0))))
            for pp in pairs:
                kaug_ref[pp, pl.ds(off, bs), 0:LANES] = k_ref[0, pl.ds(off, bs),
                                                              pp * LANES:(pp + 1) * LANES]
                kaug_ref[pp, pl.ds(off, bs), LANES:2 * LANES] = aux.astype(BF16)
            return carry

        lax.fori_loop(0, n_blk, build, 0)

    chan = lax.broadcasted_iota(jnp.int32, (LANES, bs), 0)
    blk = lax.broadcasted_iota(jnp.int32, (nbp, bs), 0)
    blk_f = blk.astype(F32)
    aux_row = lax.broadcasted_iota(jnp.int32, (MB_AUX_MASK, bs), 0)
    qpos = (lax.broadcasted_iota(jnp.int32, (MB_AUX_MASK, bs), 1) + own * bs).astype(F32)
    aux_pad = jnp.zeros((LANES - MB_AUX_MASK - nbp, bs), F32)

    heads = [(pp, hh) for pp in pairs for hh in range(2)]
    qth, gate = [], []
    for pp, hh in heads:
        qt = qt_ref[0, pp * LANES:(pp + 1) * LANES, :]
        km_hi, km_lo = _split2(kmean_ref[0, :, pp * LANES:(pp + 1) * LANES])
        qth.append(jnp.where((chan >= hh * hd) & (chan < (hh + 1) * hd), qt, 0.0))
        q_hi, q_lo = _split2(qth[-1])
        g = _dot(km_hi, q_hi) + _dot(km_hi, q_lo) + _dot(km_lo, q_hi)
        gate.append(jnp.where(blk < own, g, -jnp.inf))

    piece_id = (aux_row - MB_AUX_KOFF) & (N_PIECES - 1)
    piece = jnp.where(piece_id == 0, LOG2E_PIECES[0],
                      jnp.where(piece_id == 1, LOG2E_PIECES[1],
                                jnp.where(piece_id == 2, LOG2E_PIECES[2], LOG2E_PIECES[3])))
    q_rows = []
    for n, (pp, hh) in enumerate(heads):
        head = 2 * (MB_GANG * gang + pp) + hh
        slope_bits = (127 - ALIBI_STEP * (head + 1)) << 23
        slope = lax.bitcast_convert_type(jnp.full((MB_AUX_MASK, bs), slope_bits, jnp.int32), F32)
        qconst = -(slope * LOG2E) * qpos
        qconst_hi = qconst.astype(BF16).astype(F32)
        bias_rows = jnp.where(aux_row == 0, qconst_hi,
                              jnp.where(aux_row == 1, qconst - qconst_hi,
                                        jnp.where(aux_row < MB_AUX_BIAS_END, slope * piece, 0.0)))
        q_rows.append(jnp.concatenate([qth[n] * (hd ** -0.5 * LOG2E), bias_rows], axis=0))

    def query_operand(mask_rows):
        ops = [jnp.concatenate([q_rows[n], mask_rows[n], aux_pad], axis=0).astype(BF16)
               for n in range(len(heads))]
        return [jnp.concatenate(ops[2 * pp:2 * pp + 2], axis=1) for pp in pairs]

    def keys(pp, i):
        return kaug_ref[pp, pl.ds(pl.multiple_of(i * sup, sup), sup), :]

    def values_t(i, pp, hh):
        lo = pp * LANES + hh * hd
        return jnp.concatenate([vt_ref[i * MB_SUPER + u, lo:lo + hd, :]
                                for u in range(MB_SUPER)], axis=1)

    def produce(s_ref, pp, g, q_op, mask=None):
        s2 = _dot(keys(pp, g), q_op[pp])
        if mask is not None:
            s2 = jnp.where(mask, NEG_BIG, s2)
        s_ref[pp] = s2
        return jnp.max(s2, axis=0, keepdims=True)

    n_grp = n_blk // MB_SUPER
    grp = own // MB_SUPER

    assert MB_SUPER == 1
    rel = (lax.broadcasted_iota(jnp.int32, (sup, 2 * bs), 0) - (own - grp * MB_SUPER) * bs)
    qi = lax.broadcasted_iota(jnp.int32, (sup, 2 * bs), 1) & (bs - 1)
    future = (rel > qi) & (rel < bs)
    q_own = query_operand([jnp.zeros((nbp, bs), F32)] * len(heads))
    smax_a0 = [produce(sa_ref, pp, grp, q_own, future) for pp in pairs]

    sel = [jnp.zeros((nbp, bs), F32) for _ in heads]
    for _ in range(MB_TOPK):
        mx = [jnp.max(g, axis=0, keepdims=True) for g in gate]
        first = [jnp.min(jnp.where(g == m, blk_f, float(nbp)), axis=0, keepdims=True)
                 for g, m in zip(gate, mx)]
        hit = [blk_f == f for f in first]
        sel = [jnp.where(h, 1.0, s) for h, s in zip(hit, sel)]
        gate = [jnp.where(h, -jnp.inf, g) for h, g in zip(hit, gate)]
    keep = [jnp.where(blk < own, s, jnp.where(blk == own, 1.0, 0.0)) for s in sel]
    qaug2 = query_operand([jnp.where(k > 0.5, 0.0, NEG_BIG) for k in keep])

    def group_at(t):
        g = jnp.where(t == 0, grp, jnp.where(t > grp, grp + 1, t - 1))
        return jnp.minimum(g, n_grp - 1)

    ones_rows = jnp.ones((MB_SUM_ROWS, sup), BF16)

    def softmax_step(s_ref, pp, smax, g, carry):
        hs = range(2)
        m_i = [carry[2 * hh] for hh in hs]
        m_new = [jnp.maximum(m_i[hh], smax[:, hh * bs:(hh + 1) * bs]) for hh in hs]
        alpha = [jnp.exp2(m_i[hh] - m_new[hh]) for hh in hs]
        pexp = [jnp.exp2((s_ref[pp, :, hh * bs:(hh + 1) * bs] - m_new[hh]).astype(BF16))
                for hh in hs]
        acc_new = [carry[2 * hh + 1] * alpha[hh]
                   + _dot(jnp.concatenate([values_t(g, pp, hh), ones_rows], axis=0), pexp[hh])
                   for hh in hs]
        return (m_new[0], acc_new[0], m_new[1], acc_new[1])

    def steps(u, carry, unroll, t0):
        t = t0 + unroll * u
        smax_a = list(carry[:MB_GANG])
        stats = [carry[MB_GANG + 4 * pp:MB_GANG + 4 * pp + 4] for pp in pairs]
        for v in range(0, unroll, 2):
            smax_b = [produce(sb_ref, pp, group_at(t + v + 1), qaug2) for pp in pairs]
            stats = [softmax_step(sa_ref, pp, smax_a[pp], group_at(t + v), stats[pp])
                     for pp in pairs]
            smax_a = [produce(sa_ref, pp, group_at(t + v + 2), qaug2) for pp in pairs]
            stats = [softmax_step(sb_ref, pp, smax_b[pp], group_at(t + v + 1), stats[pp])
                     for pp in pairs]
        out = tuple(smax_a)
        for pp in pairs:
            out += tuple(stats[pp])
        return out

    stat0 = jnp.full((1, bs), -jnp.inf, F32)
    acc0 = jnp.zeros((hd + MB_SUM_ROWS, bs), F32)
    n_full = (grp + 1) // MB_UNROLL
    rem = grp + 1 - n_full * MB_UNROLL
    fin = lax.fori_loop(0, n_full, functools.partial(steps, unroll=MB_UNROLL, t0=0),
                        tuple(smax_a0) + (stat0, acc0) * len(heads))
    fin = lax.fori_loop(0, rem // 2, functools.partial(steps, unroll=2, t0=n_full * MB_UNROLL), fin)

    def last_step(carry):
        out = tuple(carry[:MB_GANG])
        for pp in pairs:
            out += softmax_step(sa_ref, pp, carry[pp], group_at(grp),
                                carry[MB_GANG + 4 * pp:MB_GANG + 4 * pp + 4])
        return out

    fin = lax.cond(rem % 2 == 1, last_step, lambda carry: carry, fin)
    accs = [fin[MB_GANG + 2 * n + 1] for n in range(len(heads))]
    out_t = jnp.concatenate([a[:hd] / a[hd:hd + 1] for a in accs], axis=0)
    o_ref[0] = out_t


def _moba(qt, k, kmean, vt):
    bsz, t_len, _ = k.shape
    bs = MB_BLOCK
    n_blk = t_len // bs
    assert MB_UNROLL % 2 == 0 and n_blk % (MB_UNROLL * MB_SUPER) == 0
    assert MB_AUX_MASK + n_blk <= LANES
    assert n_blk % SUBLANES == 0 and MB_PAIRS % MB_GANG == 0
    gw = MB_GANG * LANES
    return pl.pallas_call(
        functools.partial(_moba_kernel, n_blk=n_blk),
        grid=(bsz, MB_PAIRS // MB_GANG, n_blk),
        in_specs=[
            pl.BlockSpec((1, gw, bs), lambda b, p, i: (b * n_blk + i, p, 0)),
            pl.BlockSpec((1, t_len, gw), lambda b, p, i: (b, 0, p)),
            pl.BlockSpec((1, n_blk, gw), lambda b, p, i: (b, 0, p)),
            pl.BlockSpec((n_blk, gw, bs), lambda b, p, i: (b, p, 0)),
        ],
        out_specs=pl.BlockSpec((1, gw, bs), lambda b, p, i: (b * n_blk + i, p, 0)),
        out_shape=jax.ShapeDtypeStruct((bsz * n_blk, MB_WIDTH, bs), F32),
        scratch_shapes=[
            pltpu.VMEM((MB_GANG, t_len, 2 * LANES), BF16),
            pltpu.VMEM((MB_GANG, MB_SUPER * bs, 2 * bs), F32),
            pltpu.VMEM((MB_GANG, MB_SUPER * bs, 2 * bs), F32),
        ],
        compiler_params=pltpu.CompilerParams(
            dimension_semantics=("arbitrary", "arbitrary", "arbitrary"),
            vmem_limit_bytes=VMEM_LIMIT),
        name="moba",
    )(qt, k, kmean, vt)


def _mixout_kernel(x_ref, odn_ref, omb_ref, prew_ref, wz_ref, wgd_ref, wgm_ref,
                   wbd_ref, wbm_ref, wo_ref, postw_ref, o_ref):
    x = x_ref[...]
    h = _rms(x, prew_ref[...]).astype(BF16)
    z = _dot_nt(h, wz_ref[...])
    gate_dn = jax.nn.sigmoid(_dot_nt(h, wgd_ref[...]))
    gate_mb = jax.nn.sigmoid(_dot_nt(h, wgm_ref[...]))
    o_dn = odn_ref[...] * (z * jax.nn.sigmoid(z))
    y_dn = _dot(o_dn.astype(BF16), wbd_ref[...])
    y_mb = jnp.concatenate([_dot_tn(omb_ref[i].astype(BF16), wbm_ref[...])
                            for i in range(omb_ref.shape[0])], axis=0)
    merged = gate_dn * y_dn + gate_mb * y_mb
    y = _dot(merged.astype(BF16), wo_ref[...])
    o_ref[...] = x + _rms(y, postw_ref[...])


def _mix_out(x, o_dn, o_mb, pre_w, w_z, w_gd, w_gm, w_bd, w_bm, w_o, post_w, tm=512):
    n, d = x.shape
    return pl.pallas_call(
        _mixout_kernel,
        grid=(n // tm,),
        in_specs=[
            pl.BlockSpec((tm, d), lambda i: (i, 0)),
            pl.BlockSpec((tm, o_dn.shape[1]), lambda i: (i, 0)),
            pl.BlockSpec((tm // MB_BLOCK,) + o_mb.shape[1:], lambda i: (i, 0, 0)),
            _const_spec((1, d)),
            _const_spec(w_z.shape),
            _const_spec(w_gd.shape),
            _const_spec(w_gm.shape),
            _const_spec(w_bd.shape),
            _const_spec(w_bm.shape),
            _const_spec(w_o.shape),
            _const_spec((1, d)),
        ],
        out_specs=pl.BlockSpec((tm, d), lambda i: (i, 0)),
        out_shape=jax.ShapeDtypeStruct((n, d), F32),
        compiler_params=pltpu.CompilerParams(
            dimension_semantics=("arbitrary",), vmem_limit_bytes=VMEM_LIMIT),
        name="mix_out",
    )(x, o_dn, o_mb, pre_w, w_z, w_gd, w_gm, w_bd, w_bm, w_o, post_w)


def _layer(x, ffn1_pre_w, ffn1_w_gate, ffn1_w_up, ffn1_w_down, ffn1_post_w,
           mix_pre_w, w_in, dn_conv_w, dn_a_log, dn_dt_bias, dn_norm_w,
           w_branch_dn, w_branch_mb, w_out, mix_post_w,
           ffn2_pre_w, ffn2_w_gate, ffn2_w_up, ffn2_w_down, ffn2_post_w):
    bsz, t_len, d = x.shape
    n = bsz * t_len
    row = lambda w: w.reshape(1, -1).astype(F32)
    b16 = lambda w: w.astype(BF16)

    x = x.reshape(n, d)
    x = _ffn_block(x, row(ffn1_pre_w), b16(ffn1_w_gate), b16(ffn1_w_up), b16(ffn1_w_down),
                   row(ffn1_post_w))

    w_in_t = w_in.T
    o = 0
    w_dn = w_in_t[o:o + 3 * DN_WIDTH]; o += 3 * DN_WIDTH
    w_z = w_in_t[o:o + DN_WIDTH]; o += DN_WIDTH
    w_ba = w_in_t[o:o + 2 * DN_HEADS]; o += 2 * DN_HEADS
    w_q = w_in_t[o:o + MB_WIDTH]; o += MB_WIDTH
    w_k = w_in_t[o:o + MB_WIDTH]; o += MB_WIDTH
    w_v = w_in_t[o:o + MB_WIDTH]; o += MB_WIDTH
    w_gd = w_in_t[o:o + d]; o += d
    w_gm = w_in_t[o:o + d]; o += d
    w_ba = jnp.pad(w_ba, ((0, LANES - 2 * DN_HEADS), (0, 0)))

    def hi_lo(w):
        hi = w.astype(BF16)
        return jnp.stack([hi, (w - hi.astype(F32)).astype(BF16)])

    dn_qkv, ba, mb_k, mb_kmean, mb_qt, mb_vt = _in_proj(
        x, row(mix_pre_w), b16(w_dn), jnp.concatenate(list(hi_lo(w_ba)), axis=0), hi_lo(w_k),
        b16(w_q), b16(w_v))

    pad_heads = lambda p: jnp.pad(p.astype(F32), (DN_HEADS, LANES - 2 * DN_HEADS)).reshape(1, LANES)
    o_dn = _deltanet(dn_qkv.reshape(bsz, t_len, -1), ba.reshape(bsz, t_len, LANES),
                     dn_conv_w.astype(F32), pad_heads(dn_a_log), pad_heads(dn_dt_bias),
                     row(dn_norm_w))

    o_mb = _moba(mb_qt, mb_k.reshape(bsz, t_len, MB_WIDTH),
                 mb_kmean.reshape(bsz, t_len // MB_BLOCK, MB_WIDTH), mb_vt)

    x = _mix_out(x, o_dn.reshape(n, DN_WIDTH), o_mb, row(mix_pre_w),
                 b16(w_z), b16(w_gd), b16(w_gm), b16(w_branch_dn), b16(w_branch_mb), b16(w_out),
                 row(mix_post_w))

    x = _ffn_block(x, row(ffn2_pre_w), b16(ffn2_w_gate), b16(ffn2_w_up), b16(ffn2_w_down),
                   row(ffn2_post_w))
    return x.reshape(bsz, t_len, d)


def kernel(x, ffn1_pre_w, ffn1_w_gate, ffn1_w_up, ffn1_w_down, ffn1_post_w, mix_pre_w, w_in, dn_conv_w, dn_a_log, dn_dt_bias, dn_norm_w, w_branch_dn, w_branch_mb, w_out, mix_post_w, ffn2_pre_w, ffn2_w_gate, ffn2_w_up, ffn2_w_down, ffn2_post_w):
    depth = w_in.shape[0]
    for l in range(depth):
        x = _layer(x, ffn1_pre_w[l], ffn1_w_gate[l], ffn1_w_up[l], ffn1_w_down[l], ffn1_post_w[l],
                   mix_pre_w[l], w_in[l], dn_conv_w[l], dn_a_log[l], dn_dt_bias[l], dn_norm_w[l],
                   w_branch_dn[l], w_branch_mb[l], w_out[l], mix_post_w[l],
                   ffn2_pre_w[l], ffn2_w_gate[l], ffn2_w_up[l], ffn2_w_down[l], ffn2_post_w[l])
    return x
```
